```python
import jax, jax.numpy as jnp
from jax import lax
import numpy as np

D_MODEL = 2048
BATCH = 4
SEQ = 2048
DEPTH = 1

GLA_VALUE_WIDTH = D_MODEL // 2
GLA_HEADS = 4
GLA_DV = GLA_VALUE_WIDTH // GLA_HEADS
GLA_DK = GLA_DV // 2
GLA_KEY_WIDTH = GLA_HEADS * GLA_DK
GLA_GATE_RANK = 16
GLA_GATE_NORMALIZER = 16.0
HGRN_WIDTH = D_MODEL // 2
HGRN_EXPAND = 128
HGRN_HEADS = HGRN_WIDTH // HGRN_EXPAND
HGRN_DV = HGRN_WIDTH // HGRN_HEADS
CHUNK = 64
D_FF = 5504
EPS = 1e-6

SPLIT_SIZES = (
    GLA_KEY_WIDTH,
    GLA_KEY_WIDTH,
    GLA_VALUE_WIDTH,
    GLA_VALUE_WIDTH,
    GLA_GATE_RANK,
    HGRN_WIDTH,
    HGRN_WIDTH,
    HGRN_WIDTH,
    HGRN_WIDTH,
    D_MODEL,
    D_MODEL,
)
IN_WIDTH = 2 * GLA_KEY_WIDTH + 2 * GLA_VALUE_WIDTH + GLA_GATE_RANK + 4 * HGRN_WIDTH + 2 * D_MODEL

kernel_name = "hybrid_gla_hgrn2_macaron_sandwich"


def rms_norm(x, w):
    xf = x.astype(jnp.float32)
    y = xf * lax.rsqrt(jnp.mean(xf * xf, axis=-1, keepdims=True) + EPS)
    return (y * w.astype(jnp.float32)).astype(x.dtype)


def swiglu(x, w_gate, w_up, w_down):
    return (jax.nn.silu(x @ w_gate) * (x @ w_up)) @ w_down


def split_heads(t, n_heads):
    b, l, w = t.shape
    return t.reshape(b, l, n_heads, w // n_heads).transpose(0, 2, 1, 3)


def merge_heads(t):
    b, h, l, d = t.shape
    return t.transpose(0, 2, 1, 3).reshape(b, l, h * d)


def chunked_gated_linear_attention(q, k, v, log_a):
    q, k, v, log_a = (t.astype(jnp.float32) for t in (q, k, v, log_a))
    b, h, l, dk = q.shape
    dv = v.shape[-1]
    n = l // CHUNK

    def to_chunks(t):
        return t.reshape(b, h, n, CHUNK, t.shape[-1]).transpose(2, 0, 1, 3, 4)

    causal = jnp.tril(jnp.ones((CHUNK, CHUNK), dtype=bool))[:, :, None]

    def step(state, inp):
        qi, ki, vi, gi = inp
        cum = jnp.cumsum(gi, axis=-2)
        cum_last = cum[..., -1:, :]
        diff = cum[..., :, None, :] - cum[..., None, :, :]
        decay = jnp.exp(jnp.where(causal, diff, -jnp.inf))
        scores = jnp.einsum('bhid,bhjd,bhijd->bhij', qi, ki, decay)
        out = (jnp.einsum('bhij,bhjv->bhiv', scores, vi)
               + jnp.einsum('bhid,bhdv->bhiv', qi * jnp.exp(cum), state))
        new_state = (jnp.exp(cum_last)[..., 0, :, None] * state
                     + jnp.einsum('bhjd,bhjv->bhdv', ki * jnp.exp(cum_last - cum), vi))
        return new_state, out

    state0 = jnp.zeros((b, h, dk, dv), jnp.float32)
    _, out = lax.scan(step, state0, (to_chunks(q), to_chunks(k), to_chunks(v), to_chunks(log_a)))
    return out.transpose(1, 2, 0, 3, 4).reshape(b, h, l, dv)


def hybrid_token_mixer(u, layer, w_in, gla_w_gk_up, gla_b_gk, gla_norm, hgrn_lb_logits,
                       hgrn_norm, w_branch_gla, w_branch_hgrn, b_branch_gates, w_out):
    proj = u @ w_in
    offsets = np.cumsum(SPLIT_SIZES)[:-1].tolist()
    (g_q, g_k, g_v, g_out, g_code, h_q, h_f, h_i, h_out, z_gla, z_hgrn) = jnp.split(proj, offsets, axis=-1)

    gla_log_a = jax.nn.log_sigmoid(g_code @ gla_w_gk_up + gla_b_gk) / GLA_GATE_NORMALIZER
    o_gla = chunked_gated_linear_attention(
        split_heads(g_q * (GLA_DK ** -0.5), GLA_HEADS), split_heads(g_k, GLA_HEADS),
        split_heads(g_v, GLA_HEADS), split_heads(gla_log_a, GLA_HEADS))
    o_gla = merge_heads(rms_norm(o_gla, gla_norm)).astype(u.dtype) * jax.nn.silu(g_out)

    lb = jnp.cumsum(jax.nn.softmax(hgrn_lb_logits.astype(jnp.float32), axis=0), axis=0)[layer]
    log_f = jnp.logaddexp(jnp.log(lb), jnp.log1p(-lb) + jax.nn.log_sigmoid(h_f.astype(jnp.float32)))
    h_k = -jnp.expm1(log_f)
    o_hgrn = chunked_gated_linear_attention(
        split_heads(jax.nn.silu(h_q), HGRN_HEADS), split_heads(h_k, HGRN_HEADS),
        split_heads(h_i, HGRN_HEADS), split_heads(log_f, HGRN_HEADS))
    o_hgrn = merge_heads(rms_norm(o_hgrn, hgrn_norm)).astype(u.dtype) * jax.nn.silu(h_out)

    merged = (jax.nn.sigmoid(z_gla + b_branch_gates[0]) * (o_gla @ w_branch_gla)
              + jax.nn.sigmoid(z_hgrn + b_branch_gates[1]) * (o_hgrn @ w_branch_hgrn))
    return merged @ w_out


def setup_inputs(seed: int = 0) -> dict:
    key = jax.random.key(seed)
    ks = jax.random.split(key, 24)

    def dense(k, shape, fan_in):
        return jax.random.normal(k, shape, jnp.float32) * (fan_in ** -0.5)

    def gain(k, shape):
        return 1.0 + 0.05 * jax.random.normal(k, shape, jnp.float32)

    return {
        "x": jax.random.normal(ks[0], (BATCH, SEQ, D_MODEL), jnp.float32),
        "ffn1_pre_norm": gain(ks[1], (DEPTH, D_MODEL)),
        "ffn1_w_gate": dense(ks[2], (DEPTH, D_MODEL, D_FF), D_MODEL),
        "ffn1_w_up": dense(ks[3], (DEPTH, D_MODEL, D_FF), D_MODEL),
        "ffn1_w_down": dense(ks[4], (DEPTH, D_FF, D_MODEL), D_FF),
        "ffn1_post_norm": gain(ks[5], (DEPTH, D_MODEL)),
        "mix_pre_norm": gain(ks[6], (DEPTH, D_MODEL)),
        "w_in": dense(ks[7], (DEPTH, D_MODEL, IN_WIDTH), D_MODEL),
        "gla_w_gk_up": dense(ks[8], (DEPTH, GLA_GATE_RANK, GLA_KEY_WIDTH), GLA_GATE_RANK),
        "gla_b_gk": 0.02 * jax.random.normal(ks[9], (DEPTH, GLA_KEY_WIDTH), jnp.float32),
        "gla_norm": gain(ks[10], (DEPTH, GLA_DV)),
        "hgrn_lb_logits": 0.1 * jax.random.normal(ks[11], (DEPTH + 1, HGRN_WIDTH), jnp.float32),
        "hgrn_norm": gain(ks[12], (DEPTH, HGRN_DV)),
        "w_branch_gla": dense(ks[13], (DEPTH, GLA_VALUE_WIDTH, D_MODEL), GLA_VALUE_WIDTH),
        "w_branch_hgrn": dense(ks[14], (DEPTH, HGRN_WIDTH, D_MODEL), HGRN_WIDTH),
        "b_branch_gates": 0.02 * jax.random.normal(ks[15], (DEPTH, 2, D_MODEL), jnp.float32),
        "w_out": dense(ks[16], (DEPTH, D_MODEL, D_MODEL), D_MODEL),
        "mix_post_norm": gain(ks[17], (DEPTH, D_MODEL)),
        "ffn2_pre_norm": gain(ks[18], (DEPTH, D_MODEL)),
        "ffn2_w_gate": dense(ks[19], (DEPTH, D_MODEL, D_FF), D_MODEL),
        "ffn2_w_up": dense(ks[20], (DEPTH, D_MODEL, D_FF), D_MODEL),
        "ffn2_w_down": dense(ks[21], (DEPTH, D_FF, D_MODEL), D_FF),
        "ffn2_post_norm": gain(ks[22], (DEPTH, D_MODEL)),
    }


def reference(x, ffn1_pre_norm, ffn1_w_gate, ffn1_w_up, ffn1_w_down, ffn1_post_norm,
              mix_pre_norm, w_in, gla_w_gk_up, gla_b_gk, gla_norm, hgrn_lb_logits, hgrn_norm,
              w_branch_gla, w_branch_hgrn, b_branch_gates, w_out, mix_post_norm,
              ffn2_pre_norm, ffn2_w_gate, ffn2_w_up, ffn2_w_down, ffn2_post_norm):
    h = x
    for l in range(DEPTH):
        f1 = swiglu(rms_norm(h, ffn1_pre_norm[l]), ffn1_w_gate[l], ffn1_w_up[l], ffn1_w_down[l])
        h = h + 0.5 * rms_norm(f1, ffn1_post_norm[l])
        m = hybrid_token_mixer(rms_norm(h, mix_pre_norm[l]), l, w_in[l], gla_w_gk_up[l], gla_b_gk[l],
                               gla_norm[l], hgrn_lb_logits, hgrn_norm[l], w_branch_gla[l],
                               w_branch_hgrn[l], b_branch_gates[l], w_out[l])
        h = h + rms_norm(m, mix_post_norm[l])
        f2 = swiglu(rms_norm(h, ffn2_pre_norm[l]), ffn2_w_gate[l], ffn2_w_up[l], ffn2_w_down[l])
        h = h + 0.5 * rms_norm(f2, ffn2_post_norm[l])
    return h
```

```python
import functools

import jax
import jax.numpy as jnp
from jax import lax
from jax.experimental import pallas as pl
from jax.experimental.pallas import tpu as pltpu

F32 = jnp.float32
BF16 = jnp.bfloat16

EPS = 1e-6
CHUNK = 64
SUB = 16
GLA_HEADS = 4
GLA_GATE_RANK = 16
GLA_GATE_NORMALIZER = 16.0
HGRN_EXPAND = 128
EXP_CLAMP = 80.0

LANES = 128
VMEM_LIMIT = 56 * 1024 * 1024


def _cparams(n_axes):
    return pltpu.CompilerParams(
        dimension_semantics=("arbitrary",) * n_axes, vmem_limit_bytes=VMEM_LIMIT)


def _sigmoid(x):
    return 1.0 / (1.0 + jnp.exp(-x))


def _silu(x):
    return x * _sigmoid(x)


def _rms_scale(x):
    return lax.rsqrt(jnp.mean(x * x, axis=-1, keepdims=True) + EPS)


def _rmsnorm_kernel(x_ref, w_ref, o_ref):
    x = x_ref[...]
    o_ref[...] = (x * _rms_scale(x) * w_ref[...]).astype(o_ref.dtype)


def _rmsnorm(x, w, tm=512):
    m, d = x.shape
    return pl.pallas_call(
        _rmsnorm_kernel,
        out_shape=jax.ShapeDtypeStruct((m, d), BF16),
        grid=(m // tm,),
        in_specs=[pl.BlockSpec((tm, d), lambda i: (i, 0)),
                  pl.BlockSpec((1, d), lambda i: (0, 0))],
        out_specs=pl.BlockSpec((tm, d), lambda i: (i, 0)),
        compiler_params=_cparams(1),
        name="rmsnorm",
    )(x, w)


def _gateup_kernel(u_ref, wg_ref, wu_ref, o_ref, wg_bf, wu_bf):
    @pl.when(pl.program_id(1) == 0)
    def _():
        wg_bf[...] = wg_ref[...].astype(BF16)
        wu_bf[...] = wu_ref[...].astype(BF16)

    u = u_ref[...]
    g = jnp.dot(u, wg_bf[...], preferred_element_type=F32)
    up = jnp.dot(u, wu_bf[...], preferred_element_type=F32)
    o_ref[...] = (_silu(g) * up).astype(o_ref.dtype)


def _gateup(u, w_gate, w_up, tm=1024, tn=512):
    m, d = u.shape
    f = w_gate.shape[-1]
    w_spec = pl.BlockSpec((None, d, tn), lambda j, i: (0, 0, j))
    return pl.pallas_call(
        _gateup_kernel,
        out_shape=jax.ShapeDtypeStruct((m, f), BF16),
        grid=(pl.cdiv(f, tn), m // tm),
        in_specs=[pl.BlockSpec((tm, d), lambda j, i: (i, 0)), w_spec, w_spec],
        out_specs=pl.BlockSpec((tm, tn), lambda j, i: (i, j)),
        scratch_shapes=[pltpu.VMEM((d, tn), BF16), pltpu.VMEM((d, tn), BF16)],
        compiler_params=_cparams(2),
        name="ffn_gateup",
    )(u, w_gate, w_up)


def _proj_kernel(u_ref, w_ref, o_ref, w_bf):
    @pl.when(pl.program_id(1) == 0)
    def _():
        w_bf[...] = w_ref[...].astype(BF16)

    o_ref[...] = jnp.dot(u_ref[...], w_bf[...], preferred_element_type=F32).astype(o_ref.dtype)


def _proj(u, w, col_block0, n_cols, out_dtype, tm=1024, tn=512):
    m, d = u.shape
    if w.ndim == 3:
        w_spec = pl.BlockSpec((None, d, tn), lambda j, i: (0, 0, j + col_block0))
    else:
        w_spec = pl.BlockSpec((d, tn), lambda j, i: (0, j + col_block0))
    return pl.pallas_call(
        _proj_kernel,
        out_shape=jax.ShapeDtypeStruct((m, n_cols), out_dtype),
        grid=(n_cols // tn, m // tm),
        in_specs=[pl.BlockSpec((tm, d), lambda j, i: (i, 0)), w_spec],
        out_specs=pl.BlockSpec((tm, tn), lambda j, i: (i, j)),
        scratch_shapes=[pltpu.VMEM((d, tn), BF16)],
        compiler_params=_cparams(2),
        name="in_proj",
    )(u, w)


def _rows_kernel(x_ref, w_ref, res_ref, post_ref, *rest, n_col_tiles, tn, res_scale, emit_next):
    if emit_next:
        next_ref, h_ref, u_ref, acc_ref = rest
    else:
        h_ref, acc_ref = rest
    j = pl.program_id(1)
    acc_ref[j] = jnp.dot(x_ref[...], w_ref[...], preferred_element_type=F32)

    @pl.when(j == n_col_tiles - 1)
    def _():
        d = n_col_tiles * tn
        ssq = None
        for jj in range(n_col_tiles):
            a = acc_ref[jj]
            s = jnp.sum(a * a, axis=-1, keepdims=True)
            ssq = s if ssq is None else ssq + s
        scale = lax.rsqrt(ssq / d + EPS) * res_scale
        hsq = None
        for jj in range(n_col_tiles):
            cols = slice(jj * tn, (jj + 1) * tn)
            h = res_ref[:, cols] + acc_ref[jj] * scale * post_ref[:, cols]
            h_ref[:, cols] = h
            if emit_next:
                s = jnp.sum(h * h, axis=-1, keepdims=True)
                hsq = s if hsq is None else hsq + s
        if emit_next:
            nscale = lax.rsqrt(hsq / d + EPS)
            for jj in range(n_col_tiles):
                cols = slice(jj * tn, (jj + 1) * tn)
                u_ref[:, cols] = (h_ref[:, cols] * nscale * next_ref[:, cols]).astype(u_ref.dtype)


def _rows(x, w, res, post_w, next_w, res_scale, tm=512, tn=512):
    m, k = x.shape
    d = w.shape[-1]
    nj = d // tn
    emit_next = next_w is not None
    row_spec = pl.BlockSpec((tm, d), lambda i, j: (i, 0))
    vec_spec = pl.BlockSpec((1, d), lambda i, j: (0, 0))
    in_specs = [pl.BlockSpec((tm, k), lambda i, j: (i, 0)),
                pl.BlockSpec((k, tn), lambda i, j: (0, j)),
                row_spec, vec_spec]
    args = [x, w, res, post_w]
    out_shape = [jax.ShapeDtypeStruct((m, d), F32)]
    out_specs = [row_spec]
    if emit_next:
        in_specs.append(vec_spec)
        args.append(next_w)
        out_shape.append(jax.ShapeDtypeStruct((m, d), BF16))
        out_specs.append(row_spec)
    outs = pl.pallas_call(
        functools.partial(_rows_kernel, n_col_tiles=nj, tn=tn, res_scale=res_scale,
                          emit_next=emit_next),
        out_shape=out_shape,
        grid=(m // tm, nj),
        in_specs=in_specs,
        out_specs=out_specs,
        scratch_shapes=[pltpu.VMEM((nj, tm, tn), F32)],
        compiler_params=_cparams(2),
        name="rows_matmul_norm",
    )(*args)
    return outs if emit_next else (outs[0], None)


def _split_bf16(x):
    hi = x.astype(BF16)
    lo = (x - hi.astype(F32)).astype(BF16)
    return hi, lo


def _dot_nt(a, b):
    return lax.dot_general(a, b, (((1,), (1,)), ((), ())), preferred_element_type=F32)


def _dot_tn(a, b):
    return lax.dot_general(a, b, (((0,), (0,)), ((), ())), preferred_element_type=F32)


def _gated_chunk(q, k, v, g, st, tri, causal):
    g_hi, g_lo = _split_bf16(g)
    cum = (jnp.dot(tri, g_hi, preferred_element_type=F32)
           + jnp.dot(tri, g_lo, preferred_element_type=F32))
    cum_last = cum[CHUNK - 1:CHUNK, :]
    e_q = jnp.exp(cum)
    q_st = (q * e_q).astype(BF16)
    k_st = (k * jnp.exp(cum_last - cum)).astype(BF16)
    out = _dot_nt(q_st, st.astype(BF16))

    rows = []
    for b in range(CHUNK // SUB):
        sl = slice(b * SUB, (b + 1) * SUB)
        if b == 0:
            qe = q[sl] * e_q[sl]
            ke = k * jnp.exp(jnp.minimum(-cum, EXP_CLAMP))
        else:
            ref = cum[b * SUB - 1:b * SUB, :]
            qe = q[sl] * jnp.exp(cum[sl] - ref)
            ke = k * jnp.exp(jnp.minimum(ref - cum, EXP_CLAMP))
        rows.append(_dot_nt(qe.astype(BF16), ke.astype(BF16)))
    scores = jnp.where(causal, jnp.concatenate(rows, axis=0), 0.0).astype(BF16)
    out = out + jnp.dot(scores, v, preferred_element_type=F32)
    st_new = st * jnp.exp(cum_last) + _dot_tn(v, k_st)
    return out, st_new


def _chunk_consts():
    r = lax.broadcasted_iota(jnp.int32, (CHUNK, CHUNK), 0)
    c = lax.broadcasted_iota(jnp.int32, (CHUNK, CHUNK), 1)
    causal = r >= c
    return jnp.where(causal, 1.0, 0.0).astype(BF16), causal


def _head_norm_gate(o, norm_w, gate):
    return (o * _rms_scale(o) * norm_w * _silu(gate.astype(F32))).astype(BF16)


def _gla_kernel(q_ref, k_ref, v_ref, go_ref, code_ref, wgk_ref, bgk_ref, gn_ref, o_ref, st_ref,
                *, n_chunks, dk, dv):
    @pl.when(pl.program_id(1) == 0)
    def _():
        st_ref[...] = jnp.zeros_like(st_ref)

    tri, causal = _chunk_consts()
    w_hi, w_lo = _split_bf16(wgk_ref[...])
    bias = bgk_ref[...]
    norm_w = gn_ref[...]
    q_scale = dk ** -0.5

    def chunk_body(c, carry):
        rows = pl.ds(pl.multiple_of(c * CHUNK, CHUNK), CHUNK)
        c_hi, c_lo = _split_bf16(code_ref[rows, :])
        z = (jnp.dot(c_hi, w_hi, preferred_element_type=F32)
             + jnp.dot(c_lo, w_hi, preferred_element_type=F32)
             + jnp.dot(c_hi, w_lo, preferred_element_type=F32)) + bias
        log_a = (jnp.minimum(z, 0.0) - jnp.log1p(jnp.exp(-jnp.abs(z)))) * (1.0 / GLA_GATE_NORMALIZER)
        for h in range(GLA_HEADS):
            kc = slice(h * dk, (h + 1) * dk)
            vc = slice(h * dv, (h + 1) * dv)
            q = q_ref[rows, kc].astype(F32) * q_scale
            k = k_ref[rows, kc].astype(F32)
            out, st_new = _gated_chunk(q, k, v_ref[rows, vc], log_a[:, kc], st_ref[h], tri, causal)
            st_ref[h] = st_new
            o_ref[rows, vc] = _head_norm_gate(out, norm_w, go_ref[rows, vc])
        return carry

    lax.fori_loop(0, n_chunks, chunk_body, 0)


def _gla(p_a, p_code, wgk_pad, b_gk, gla_norm, batch, seq, t_blk=512):
    kw = wgk_pad.shape[1]
    vw = (p_a.shape[1] - 2 * kw) // 2
    dk = kw // GLA_HEADS
    dv = vw // GLA_HEADS
    nt = seq // t_blk
    row = lambda b, t: b * nt + t
    return pl.pallas_call(
        functools.partial(_gla_kernel, n_chunks=t_blk // CHUNK, dk=dk, dv=dv),
        out_shape=jax.ShapeDtypeStruct((batch * seq, vw), BF16),
        grid=(batch, nt),
        in_specs=[pl.BlockSpec((t_blk, kw), lambda b, t: (row(b, t), 0)),
                  pl.BlockSpec((t_blk, kw), lambda b, t: (row(b, t), 1)),
                  pl.BlockSpec((t_blk, vw), lambda b, t: (row(b, t), 1)),
                  pl.BlockSpec((t_blk, vw), lambda b, t: (row(b, t), 2)),
                  pl.BlockSpec((t_blk, LANES), lambda b, t: (row(b, t), 0)),
                  pl.BlockSpec((LANES, kw), lambda b, t: (0, 0)),
                  pl.BlockSpec((1, kw), lambda b, t: (0, 0)),
                  pl.BlockSpec((1, dv), lambda b, t: (0, 0))],
        out_specs=pl.BlockSpec((t_blk, vw), lambda b, t: (row(b, t), 0)),
        scratch_shapes=[pltpu.VMEM((GLA_HEADS, dv, dk), F32)],
        compiler_params=_cparams(2),
        name="gla_mixer",
    )(p_a, p_a, p_a, p_a, p_code, wgk_pad, b_gk, gla_norm)


def _hgrn_kernel(hq_ref, hi_ref, ho_ref, hf_ref, lbl_ref, hn_ref, o_ref, st_ref,
                 *, n_chunks, n_heads, dk, layer):
    @pl.when(pl.program_id(1) == 0)
    def _():
        st_ref[...] = jnp.zeros_like(st_ref)

    tri, causal = _chunk_consts()
    logits = lbl_ref[...]
    p = jnp.exp(logits - jnp.max(logits, axis=0, keepdims=True))
    p = p / jnp.sum(p, axis=0, keepdims=True)
    lb = jnp.sum(p[:layer + 1], axis=0, keepdims=True)
    one_m_lb = jnp.sum(p[layer + 1:], axis=0, keepdims=True)
    norm_w = hn_ref[...]

    def chunk_body(c, carry):
        rows = pl.ds(pl.multiple_of(c * CHUNK, CHUNK), CHUNK)
        for h in range(n_heads):
            hc = slice(h * dk, (h + 1) * dk)
            hf = hf_ref[rows, hc]
            e = jnp.exp(-jnp.abs(hf))
            inv = 1.0 / (1.0 + e)
            pos = hf >= 0.0
            sig = jnp.where(pos, inv, e * inv)
            sig_neg = jnp.where(pos, e * inv, inv)
            log_f = jnp.log(lb[:, hc] + one_m_lb[:, hc] * sig)
            k = one_m_lb[:, hc] * sig_neg
            q = _silu(hq_ref[rows, hc].astype(F32))
            out, st_new = _gated_chunk(q, k, hi_ref[rows, hc], log_f, st_ref[h], tri, causal)
            st_ref[h] = st_new
            o_ref[rows, hc] = _head_norm_gate(out, norm_w, ho_ref[rows, hc])
        return carry

    lax.fori_loop(0, n_chunks, chunk_body, 0)


def _hgrn(p_b, p_f, lb_logits, hgrn_norm, layer, batch, seq, t_blk=512):
    w = p_f.shape[1]
    dk = HGRN_EXPAND
    n_heads = w // dk
    nt = seq // t_blk
    n_lb = lb_logits.shape[0]
    row = lambda b, t: b * nt + t
    return pl.pallas_call(
        functools.partial(_hgrn_kernel, n_chunks=t_blk // CHUNK, n_heads=n_heads, dk=dk,
                          layer=layer),
        out_shape=jax.ShapeDtypeStruct((batch * seq, w), BF16),
        grid=(batch, nt),
        in_specs=[pl.BlockSpec((t_blk, w), lambda b, t: (row(b, t), 0)),
                  pl.BlockSpec((t_blk, w), lambda b, t: (row(b, t), 1)),
                  pl.BlockSpec((t_blk, w), lambda b, t: (row(b, t), 2)),
                  pl.BlockSpec((t_blk, w), lambda b, t: (row(b, t), 0)),
                  pl.BlockSpec((n_lb, w), lambda b, t: (0, 0)),
                  pl.BlockSpec((1, dk), lambda b, t: (0, 0))],
        out_specs=pl.BlockSpec((t_blk, w), lambda b, t: (row(b, t), 0)),
        scratch_shapes=[pltpu.VMEM((n_heads, dk, dk), F32)],
        compiler_params=_cparams(2),
        name="hgrn_mixer",
    )(p_b, p_b, p_b, p_f, lb_logits, hgrn_norm)


def _merge_kernel(og_ref, oh_ref, zg_ref, zh_ref, wg_ref, wh_ref, bg_ref, bh_ref, o_ref,
                  wg_bf, wh_bf):
    @pl.when(pl.program_id(1) == 0)
    def _():
        wg_bf[...] = wg_ref[...].astype(BF16)
        wh_bf[...] = wh_ref[...].astype(BF16)

    a = jnp.dot(og_ref[...], wg_bf[...], preferred_element_type=F32)
    b = jnp.dot(oh_ref[...], wh_bf[...], preferred_element_type=F32)
    o_ref[...] = (_sigmoid(zg_ref[...].astype(F32) + bg_ref[...]) * a
                  + _sigmoid(zh_ref[...].astype(F32) + bh_ref[...]) * b).astype(o_ref.dtype)


def _merge(o_gla, o_hgrn, p_b, zg_col0, zh_col0, w_bg, w_bh, b_gates, tm=1024, tn=512):
    m, kdim = o_gla.shape
    d = w_bg.shape[-1]
    x_spec = pl.BlockSpec((tm, kdim), lambda j, i: (i, 0))
    w_spec = pl.BlockSpec((None, kdim, tn), lambda j, i: (0, 0, j))
    return pl.pallas_call(
        _merge_kernel,
        out_shape=jax.ShapeDtypeStruct((m, d), BF16),
        grid=(d // tn, m // tm),
        in_specs=[x_spec, x_spec,
                  pl.BlockSpec((tm, tn), lambda j, i: (i, zg_col0 // tn + j)),
                  pl.BlockSpec((tm, tn), lambda j, i: (i, zh_col0 // tn + j)),
                  w_spec, w_spec,
                  pl.BlockSpec((None, 1, tn), lambda j, i: (0, 0, j)),
                  pl.BlockSpec((None, 1, tn), lambda j, i: (1, 0, j))],
        out_specs=pl.BlockSpec((tm, tn), lambda j, i: (i, j)),
        scratch_shapes=[pltpu.VMEM((kdim, tn), BF16), pltpu.VMEM((kdim, tn), BF16)],
        compiler_params=_cparams(2),
        name="branch_merge",
    )(o_gla, o_hgrn, p_b, p_b, w_bg, w_bh, b_gates, b_gates)


def kernel(x, ffn1_pre_norm, ffn1_w_gate, ffn1_w_up, ffn1_w_down, ffn1_post_norm, mix_pre_norm, w_in, gla_w_gk_up, gla_b_gk, gla_norm, hgrn_lb_logits, hgrn_norm, w_branch_gla, w_branch_hgrn, b_branch_gates, w_out, mix_post_norm, ffn2_pre_norm, ffn2_w_gate, ffn2_w_up, ffn2_w_down, ffn2_post_norm):
    batch, seq, d_model = x.shape
    depth = ffn1_w_gate.shape[0]
    m = batch * seq
    kw = gla_w_gk_up.shape[-1]
    vw = d_model // 2
    a_cols = 2 * kw + 2 * vw
    code0 = a_cols
    hq0 = code0 + GLA_GATE_RANK
    hf0, hi0, ho0, zg0 = hq0 + vw, hq0 + 2 * vw, hq0 + 3 * vw, hq0 + 4 * vw

    h = x.reshape(m, d_model)
    u = _rmsnorm(h, ffn1_pre_norm[0:1])
    for l in range(depth):
        mid = _gateup(u, ffn1_w_gate[l:l + 1], ffn1_w_up[l:l + 1])
        h, u = _rows(mid, ffn1_w_down[l].astype(BF16), h, ffn1_post_norm[l:l + 1],
                     mix_pre_norm[l:l + 1], 0.5)

        w_l = w_in[l]
        w_f = w_l[:, hf0:hf0 + vw].astype(BF16)
        w_b = jnp.concatenate([w_l[:, hq0:hq0 + vw], w_l[:, hi0:hi0 + vw], w_l[:, ho0:ho0 + vw],
                               w_l[:, zg0:]], axis=1).astype(BF16)
        p_a = _proj(u, w_in[l:l + 1], 0, a_cols, BF16)
        p_code = _proj(u, w_in[l:l + 1], code0 // LANES, LANES, F32, tn=LANES)
        p_f = _proj(u, w_f, 0, vw, F32)
        p_b = _proj(u, w_b, 0, w_b.shape[1], BF16)
        wgk_pad = jnp.pad(gla_w_gk_up[l], ((0, LANES - GLA_GATE_RANK), (0, 0)))
        o_gla = _gla(p_a, p_code, wgk_pad, gla_b_gk[l:l + 1], gla_norm[l:l + 1], batch, seq)
        o_hgrn = _hgrn(p_b, p_f, hgrn_lb_logits, hgrn_norm[l:l + 1], l, batch, seq)
        merged = _merge(o_gla, o_hgrn, p_b, 3 * vw, 3 * vw + d_model,
                        w_branch_gla[l:l + 1], w_branch_hgrn[l:l + 1],
                        b_branch_gates[l].reshape(2, 1, d_model))
        h, u = _rows(merged, w_out[l].astype(BF16), h, mix_post_norm[l:l + 1],
                     ffn2_pre_norm[l:l + 1], 1.0)

        mid = _gateup(u, ffn2_w_gate[l:l + 1], ffn2_w_up[l:l + 1])
        next_norm = ffn1_pre_norm[l + 1:l + 2] if l + 1 < depth else None
        h, u = _rows(mid, ffn2_w_down[l].astype(BF16), h, ffn2_post_norm[l:l + 1], next_norm, 0.5)
    return h.reshape(batch, seq, d_model)
```

```python
import functools

import jax
import jax.numpy as jnp
from jax import lax
from jax.experimental import pallas as pl
from jax.experimental.pallas import tpu as pltpu

F32 = jnp.float32
BF16 = jnp.bfloat16

EPS = 1e-6
CHUNK = 64
SUB = 16
GLA_HEADS = 4
GLA_GATE_RANK = 16
GLA_GATE_NORMALIZER = 16.0
HGRN_EXPAND = 128
EXP_CLAMP = 80.0

LANES = 128
VMEM_LIMIT = 56 * 1024 * 1024


def _cparams(n_axes):
    return pltpu.CompilerParams(
        dimension_semantics=("arbitrary",) * n_axes, vmem_limit_bytes=VMEM_LIMIT)


def _sigmoid(x):
    return 1.0 / (1.0 + jnp.exp(-x))


def _silu(x):
    return x * _sigmoid(x)


def _rms_scale(x):
    return lax.rsqrt(jnp.mean(x * x, axis=-1, keepdims=True) + EPS)


def _rmsnorm_kernel(x_ref, w_ref, o_ref):
    x = x_ref[...]
    o_ref[...] = (x * _rms_scale(x) * w_ref[...]).astype(o_ref.dtype)


def _rmsnorm(x, w, tm=512):
    m, d = x.shape
    return pl.pallas_call(
        _rmsnorm_kernel,
        out_shape=jax.ShapeDtypeStruct((m, d), BF16),
        grid=(m // tm,),
        in_specs=[pl.BlockSpec((tm, d), lambda i: (i, 0)),
                  pl.BlockSpec((1, d), lambda i: (0, 0))],
        out_specs=pl.BlockSpec((tm, d), lambda i: (i, 0)),
        compiler_params=_cparams(1),
        name="rmsnorm",
    )(x, w)


def _gateup_kernel(u_ref, wg_ref, wu_ref, o_ref, wg_bf, wu_bf):
    @pl.when(pl.program_id(1) == 0)
    def _():
        wg_bf[...] = wg_ref[...].astype(BF16)
        wu_bf[...] = wu_ref[...].astype(BF16)

    u = u_ref[...]
    g = jnp.dot(u, wg_bf[...], preferred_element_type=F32)
    up = jnp.dot(u, wu_bf[...], preferred_element_type=F32)
    o_ref[...] = (_silu(g) * up).astype(o_ref.dtype)


def _gateup(u, w_gate, w_up, tm=1024, tn=512):
    m, d = u.shape
    f = w_gate.shape[-1]
    w_spec = pl.BlockSpec((None, d, tn), lambda j, i: (0, 0, j))
    return pl.pallas_call(
        _gateup_kernel,
        out_shape=jax.ShapeDtypeStruct((m, f), BF16),
        grid=(pl.cdiv(f, tn), m // tm),
        in_specs=[pl.BlockSpec((tm, d), lambda j, i: (i, 0)), w_spec, w_spec],
        out_specs=pl.BlockSpec((tm, tn), lambda j, i: (i, j)),
        scratch_shapes=[pltpu.VMEM((d, tn), BF16), pltpu.VMEM((d, tn), BF16)],
        compiler_params=_cparams(2),
        name="ffn_gateup",
    )(u, w_gate, w_up)


def _proj_kernel(u_ref, w_ref, *rest, shift, tn):
    if shift:
        wn_ref, o_ref, w_bf = rest
    else:
        o_ref, w_bf = rest

    @pl.when(pl.program_id(1) == 0)
    def _():
        w = w_ref[...]
        if shift:
            w = jnp.concatenate([w, wn_ref[...]], axis=1)[:, shift:shift + tn]
        w_bf[...] = w.astype(BF16)

    o_ref[...] = jnp.dot(u_ref[...], w_bf[...], preferred_element_type=F32).astype(o_ref.dtype)


def _proj(u, w, col0, n_cols, out_dtype, tm=1024, tn=1024):
    m, d = u.shape
    shift = col0 % LANES
    blk0 = (col0 - shift) // tn
    assert blk0 * tn == col0 - shift and n_cols % tn == 0
    in_specs = [pl.BlockSpec((tm, d), lambda j, i: (i, 0)),
                pl.BlockSpec((None, d, tn), lambda j, i: (0, 0, j + blk0))]
    args = [u, w]
    if shift:
        lanes_per_tile = tn // LANES
        in_specs.append(pl.BlockSpec((None, d, LANES),
                                     lambda j, i: (0, 0, (j + blk0 + 1) * lanes_per_tile)))
        args.append(w)
    return pl.pallas_call(
        functools.partial(_proj_kernel, shift=shift, tn=tn),
        out_shape=jax.ShapeDtypeStruct((m, n_cols), out_dtype),
        grid=(n_cols // tn, m // tm),
        in_specs=in_specs,
        out_specs=pl.BlockSpec((tm, tn), lambda j, i: (i, j)),
        scratch_shapes=[pltpu.VMEM((d, tn), BF16)],
        compiler_params=_cparams(2),
        name="in_proj",
    )(*args)


def _rows_kernel(x_ref, w_ref, res_ref, post_ref, *rest, n_col_tiles, tn, res_scale, emit_next):
    if emit_next:
        next_ref, h_ref, u_ref, acc_ref = rest
    else:
        h_ref, acc_ref = rest
    j = pl.program_id(1)
    acc_ref[j] = jnp.dot(x_ref[...], w_ref[...], preferred_element_type=F32)

    @pl.when(j == n_col_tiles - 1)
    def _():
        d = n_col_tiles * tn
        ssq = None
        for jj in range(n_col_tiles):
            a = acc_ref[jj]
            s = jnp.sum(a * a, axis=-1, keepdims=True)
            ssq = s if ssq is None else ssq + s
        scale = lax.rsqrt(ssq / d + EPS) * res_scale
        hsq = None
        for jj in range(n_col_tiles):
            cols = slice(jj * tn, (jj + 1) * tn)
            h = res_ref[:, cols] + acc_ref[jj] * scale * post_ref[:, cols]
            h_ref[:, cols] = h
            if emit_next:
                s = jnp.sum(h * h, axis=-1, keepdims=True)
                hsq = s if hsq is None else hsq + s
        if emit_next:
            nscale = lax.rsqrt(hsq / d + EPS)
            for jj in range(n_col_tiles):
                cols = slice(jj * tn, (jj + 1) * tn)
                u_ref[:, cols] = (h_ref[:, cols] * nscale * next_ref[:, cols]).astype(u_ref.dtype)


def _rows(x, w, res, post_w, next_w, res_scale, tm=512, tn=512):
    m, k = x.shape
    d = w.shape[-1]
    nj = d // tn
    emit_next = next_w is not None
    row_spec = pl.BlockSpec((tm, d), lambda i, j: (i, 0))
    vec_spec = pl.BlockSpec((1, d), lambda i, j: (0, 0))
    in_specs = [pl.BlockSpec((tm, k), lambda i, j: (i, 0)),
                pl.BlockSpec((k, tn), lambda i, j: (0, j)),
                row_spec, vec_spec]
    args = [x, w, res, post_w]
    out_shape = [jax.ShapeDtypeStruct((m, d), F32)]
    out_specs = [row_spec]
    if emit_next:
        in_specs.append(vec_spec)
        args.append(next_w)
        out_shape.append(jax.ShapeDtypeStruct((m, d), BF16))
        out_specs.append(row_spec)
    outs = pl.pallas_call(
        functools.partial(_rows_kernel, n_col_tiles=nj, tn=tn, res_scale=res_scale,
                          emit_next=emit_next),
        out_shape=out_shape,
        grid=(m // tm, nj),
        in_specs=in_specs,
        out_specs=out_specs,
        scratch_shapes=[pltpu.VMEM((nj, tm, tn), F32)],
        compiler_params=_cparams(2),
        name="rows_matmul_norm",
    )(*args)
    return outs if emit_next else (outs[0], None)


def _split_bf16(x):
    hi = x.astype(BF16)
    lo = (x - hi.astype(F32)).astype(BF16)
    return hi, lo


def _dot_nt(a, b):
    return lax.dot_general(a, b, (((1,), (1,)), ((), ())), preferred_element_type=F32)


def _dot_tn(a, b):
    return lax.dot_general(a, b, (((0,), (0,)), ((), ())), preferred_element_type=F32)


def _gated_chunk(q, k, v, g, st, tri, causal):
    g_hi, g_lo = _split_bf16(g)
    cum = (jnp.dot(tri, g_hi, preferred_element_type=F32)
           + jnp.dot(tri, g_lo, preferred_element_type=F32))
    cum_last = cum[CHUNK - 1:CHUNK, :]
    e_q = jnp.exp(cum)
    q_st = (q * e_q).astype(BF16)
    k_st = (k * jnp.exp(cum_last - cum)).astype(BF16)
    out = _dot_nt(q_st, st.astype(BF16))

    rows = []
    for b in range(CHUNK // SUB):
        sl = slice(b * SUB, (b + 1) * SUB)
        if b == 0:
            qe = q[sl] * e_q[sl]
            ke = k * jnp.exp(jnp.minimum(-cum, EXP_CLAMP))
        else:
            ref = cum[b * SUB - 1:b * SUB, :]
            qe = q[sl] * jnp.exp(cum[sl] - ref)
            ke = k * jnp.exp(jnp.minimum(ref - cum, EXP_CLAMP))
        rows.append(_dot_nt(qe.astype(BF16), ke.astype(BF16)))
    scores = jnp.where(causal, jnp.concatenate(rows, axis=0), 0.0).astype(BF16)
    out = out + jnp.dot(scores, v, preferred_element_type=F32)
    st_new = st * jnp.exp(cum_last) + _dot_tn(v, k_st)
    return out, st_new


def _chunk_consts():
    r = lax.broadcasted_iota(jnp.int32, (CHUNK, CHUNK), 0)
    c = lax.broadcasted_iota(jnp.int32, (CHUNK, CHUNK), 1)
    causal = r >= c
    return jnp.where(causal, 1.0, 0.0).astype(BF16), causal


def _head_norm_gate(o, norm_w, gate):
    return (o * _rms_scale(o) * norm_w * _silu(gate.astype(F32))).astype(BF16)


def _gla_kernel(q_ref, k_ref, v_ref, go_ref, code_ref, wgk_ref, bgk_ref, gn_ref, o_ref, st_ref,
                *, n_chunks, dk, dv):
    @pl.when(pl.program_id(1) == 0)
    def _():
        st_ref[...] = jnp.zeros_like(st_ref)

    tri, causal = _chunk_consts()
    w_hi, w_lo = _split_bf16(wgk_ref[...])
    bias = bgk_ref[...]
    norm_w = gn_ref[...]
    q_scale = dk ** -0.5

    def chunk_body(c, carry):
        rows = pl.ds(pl.multiple_of(c * CHUNK, CHUNK), CHUNK)
        c_hi, c_lo = _split_bf16(code_ref[rows, :])
        z = (jnp.dot(c_hi, w_hi, preferred_element_type=F32)
             + jnp.dot(c_lo, w_hi, preferred_element_type=F32)
             + jnp.dot(c_hi, w_lo, preferred_element_type=F32)) + bias
        log_a = (jnp.minimum(z, 0.0) - jnp.log1p(jnp.exp(-jnp.abs(z)))) * (1.0 / GLA_GATE_NORMALIZER)
        for h in range(GLA_HEADS):
            kc = slice(h * dk, (h + 1) * dk)
            vc = slice(h * dv, (h + 1) * dv)
            q = q_ref[rows, kc].astype(F32) * q_scale
            k = k_ref[rows, kc].astype(F32)
            out, st_new = _gated_chunk(q, k, v_ref[rows, vc], log_a[:, kc], st_ref[h], tri, causal)
            st_ref[h] = st_new
            o_ref[rows, vc] = _head_norm_gate(out, norm_w, go_ref[rows, vc])
        return carry

    lax.fori_loop(0, n_chunks, chunk_body, 0)


def _gla(p_a, p_code, wgk_pad, b_gk, gla_norm, batch, seq, t_blk=512):
    kw = wgk_pad.shape[1]
    vw = (p_a.shape[1] - 2 * kw) // 2
    dk = kw // GLA_HEADS
    dv = vw // GLA_HEADS
    nt = seq // t_blk
    row = lambda b, t: b * nt + t
    return pl.pallas_call(
        functools.partial(_gla_kernel, n_chunks=t_blk // CHUNK, dk=dk, dv=dv),
        out_shape=jax.ShapeDtypeStruct((batch * seq, vw), BF16),
        grid=(batch, nt),
        in_specs=[pl.BlockSpec((t_blk, kw), lambda b, t: (row(b, t), 0)),
                  pl.BlockSpec((t_blk, kw), lambda b, t: (row(b, t), 1)),
                  pl.BlockSpec((t_blk, vw), lambda b, t: (row(b, t), 1)),
                  pl.BlockSpec((t_blk, vw), lambda b, t: (row(b, t), 2)),
                  pl.BlockSpec((t_blk, LANES), lambda b, t: (row(b, t), 0)),
                  pl.BlockSpec((LANES, kw), lambda b, t: (0, 0)),
                  pl.BlockSpec((1, kw), lambda b, t: (0, 0)),
                  pl.BlockSpec((1, dv), lambda b, t: (0, 0))],
        out_specs=pl.BlockSpec((t_blk, vw), lambda b, t: (row(b, t), 0)),
        scratch_shapes=[pltpu.VMEM((GLA_HEADS, dv, dk), F32)],
        compiler_params=_cparams(2),
        name="gla_mixer",
    )(p_a, p_a, p_a, p_a, p_code, wgk_pad, b_gk, gla_norm)


def _hgrn_kernel(hq_ref, hi_ref, ho_ref, hf_ref, lbl_ref, hn_ref, o_ref, st_ref,
                 *, n_chunks, n_heads, dk, layer):
    @pl.when(pl.program_id(1) == 0)
    def _():
        st_ref[...] = jnp.zeros_like(st_ref)

    tri, causal = _chunk_consts()
    logits = lbl_ref[...]
    p = jnp.exp(logits - jnp.max(logits, axis=0, keepdims=True))
    p = p / jnp.sum(p, axis=0, keepdims=True)
    lb = jnp.sum(p[:layer + 1], axis=0, keepdims=True)
    one_m_lb = jnp.sum(p[layer + 1:], axis=0, keepdims=True)
    norm_w = hn_ref[...]

    def chunk_body(c, carry):
        rows = pl.ds(pl.multiple_of(c * CHUNK, CHUNK), CHUNK)
        for h in range(n_heads):
            hc = slice(h * dk, (h + 1) * dk)
            hf = hf_ref[rows, hc]
            e = jnp.exp(-jnp.abs(hf))
            inv = 1.0 / (1.0 + e)
            pos = hf >= 0.0
            sig = jnp.where(pos, inv, e * inv)
            sig_neg = jnp.where(pos, e * inv, inv)
            log_f = jnp.log(lb[:, hc] + one_m_lb[:, hc] * sig)
            k = one_m_lb[:, hc] * sig_neg
            q = _silu(hq_ref[rows, hc].astype(F32))
            out, st_new = _gated_chunk(q, k, hi_ref[rows, hc], log_f, st_ref[h], tri, causal)
            st_ref[h] = st_new
            o_ref[rows, hc] = _head_norm_gate(out, norm_w, ho_ref[rows, hc])
        return carry

    lax.fori_loop(0, n_chunks, chunk_body, 0)


def _hgrn(p_q, p_r, p_f, lb_logits, hgrn_norm, layer, batch, seq, t_blk=512):
    w = p_f.shape[1]
    dk = HGRN_EXPAND
    n_heads = w // dk
    nt = seq // t_blk
    n_lb = lb_logits.shape[0]
    row = lambda b, t: b * nt + t
    return pl.pallas_call(
        functools.partial(_hgrn_kernel, n_chunks=t_blk // CHUNK, n_heads=n_heads, dk=dk,
                          layer=layer),
        out_shape=jax.ShapeDtypeStruct((batch * seq, w), BF16),
        grid=(batch, nt),
        in_specs=[pl.BlockSpec((t_blk, w), lambda b, t: (row(b, t), 0)),
                  pl.BlockSpec((t_blk, w), lambda b, t: (row(b, t), 0)),
                  pl.BlockSpec((t_blk, w), lambda b, t: (row(b, t), 1)),
                  pl.BlockSpec((t_blk, w), lambda b, t: (row(b, t), 0)),
                  pl.BlockSpec((n_lb, w), lambda b, t: (0, 0)),
                  pl.BlockSpec((1, dk), lambda b, t: (0, 0))],
        out_specs=pl.BlockSpec((t_blk, w), lambda b, t: (row(b, t), 0)),
        scratch_shapes=[pltpu.VMEM((n_heads, dk, dk), F32)],
        compiler_params=_cparams(2),
        name="hgrn_mixer",
    )(p_q, p_r, p_r, p_f, lb_logits, hgrn_norm)


def _merge_kernel(og_ref, oh_ref, zg_ref, zh_ref, wg_ref, wh_ref, bg_ref, bh_ref, o_ref,
                  wg_bf, wh_bf):
    @pl.when(pl.program_id(1) == 0)
    def _():
        wg_bf[...] = wg_ref[...].astype(BF16)
        wh_bf[...] = wh_ref[...].astype(BF16)

    a = jnp.dot(og_ref[...], wg_bf[...], preferred_element_type=F32)
    b = jnp.dot(oh_ref[...], wh_bf[...], preferred_element_type=F32)
    o_ref[...] = (_sigmoid(zg_ref[...].astype(F32) + bg_ref[...]) * a
                  + _sigmoid(zh_ref[...].astype(F32) + bh_ref[...]) * b).astype(o_ref.dtype)


def _merge(o_gla, o_hgrn, p_b, zg_col0, zh_col0, w_bg, w_bh, b_gates, tm=1024, tn=1024):
    m, kdim = o_gla.shape
    d = w_bg.shape[-1]
    x_spec = pl.BlockSpec((tm, kdim), lambda j, i: (i, 0))
    w_spec = pl.BlockSpec((None, kdim, tn), lambda j, i: (0, 0, j))
    return pl.pallas_call(
        _merge_kernel,
        out_shape=jax.ShapeDtypeStruct((m, d), BF16),
        grid=(d // tn, m // tm),
        in_specs=[x_spec, x_spec,
                  pl.BlockSpec((tm, tn), lambda j, i: (i, zg_col0 // tn + j)),
                  pl.BlockSpec((tm, tn), lambda j, i: (i, zh_col0 // tn + j)),
                  w_spec, w_spec,
                  pl.BlockSpec((None, 1, tn), lambda j, i: (0, 0, j)),
                  pl.BlockSpec((None, 1, tn), lambda j, i: (1, 0, j))],
        out_specs=pl.BlockSpec((tm, tn), lambda j, i: (i, j)),
        scratch_shapes=[pltpu.VMEM((kdim, tn), BF16), pltpu.VMEM((kdim, tn), BF16)],
        compiler_params=_cparams(2),
        name="branch_merge",
    )(o_gla, o_hgrn, p_b, p_b, w_bg, w_bh, b_gates, b_gates)


def kernel(x, ffn1_pre_norm, ffn1_w_gate, ffn1_w_up, ffn1_w_down, ffn1_post_norm, mix_pre_norm, w_in, gla_w_gk_up, gla_b_gk, gla_norm, hgrn_lb_logits, hgrn_norm, w_branch_gla, w_branch_hgrn, b_branch_gates, w_out, mix_post_norm, ffn2_pre_norm, ffn2_w_gate, ffn2_w_up, ffn2_w_down, ffn2_post_norm):
    batch, seq, d_model = x.shape
    depth = ffn1_w_gate.shape[0]
    m = batch * seq
    kw = gla_w_gk_up.shape[-1]
    vw = d_model // 2
    a_cols = 2 * kw + 2 * vw
    code0 = a_cols
    hq0 = code0 + GLA_GATE_RANK
    hf0, hi0 = hq0 + vw, hq0 + 2 * vw

    h = x.reshape(m, d_model)
    u = _rmsnorm(h, ffn1_pre_norm[0:1])
    for l in range(depth):
        mid = _gateup(u, ffn1_w_gate[l:l + 1], ffn1_w_up[l:l + 1])
        h, u = _rows(mid, ffn1_w_down[l].astype(BF16), h, ffn1_post_norm[l:l + 1],
                     mix_pre_norm[l:l + 1], 0.5)

        w_l = w_in[l:l + 1]
        p_a = _proj(u, w_l, 0, a_cols, BF16)
        p_code = _proj(u, w_l, code0, LANES, F32, tn=LANES)
        p_q = _proj(u, w_l, hq0, vw, BF16)
        p_f = _proj(u, w_l, hf0, vw, F32)
        p_r = _proj(u, w_l, hi0, 2 * vw + 2 * d_model, BF16)
        wgk_pad = jnp.pad(gla_w_gk_up[l], ((0, LANES - GLA_GATE_RANK), (0, 0)))
        o_gla = _gla(p_a, p_code, wgk_pad, gla_b_gk[l:l + 1], gla_norm[l:l + 1], batch, seq)
        o_hgrn = _hgrn(p_q, p_r, p_f, hgrn_lb_logits, hgrn_norm[l:l + 1], l, batch, seq)
        merged = _merge(o_gla, o_hgrn, p_r, 2 * vw, 2 * vw + d_model,
                        w_branch_gla[l:l + 1], w_branch_hgrn[l:l + 1],
                        b_branch_gates[l].reshape(2, 1, d_model))
        h, u = _rows(merged, w_out[l].astype(BF16), h, mix_post_norm[l:l + 1],
                     ffn2_pre_norm[l:l + 1], 1.0, tn=d_model)

        mid = _gateup(u, ffn2_w_gate[l:l + 1], ffn2_w_up[l:l + 1])
        next_norm = ffn1_pre_norm[l + 1:l + 2] if l + 1 < depth else None
        h, u = _rows(mid, ffn2_w_down[l].astype(BF16), h, ffn2_post_norm[l:l + 1], next_norm, 0.5)
    return h.reshape(batch, seq, d_model)
```

```python
import functools

import jax
import jax.numpy as jnp
from jax import lax
from jax.experimental import pallas as pl
from jax.experimental.pallas import tpu as pltpu

F32 = jnp.float32
BF16 = jnp.bfloat16

EPS = 1e-6
CHUNK = 64
SUB = 16
GLA_HEADS = 4
GLA_GATE_RANK = 16
GLA_GATE_NORMALIZER = 16.0
HGRN_EXPAND = 128
LOG2_E = 1.4426950408889634
EXP2_CLAMP = 115.0

LANES = 128
SUBLANES = 8
VMEM_LIMIT = 56 * 1024 * 1024


def _cparams(n_axes):
    return pltpu.CompilerParams(
        dimension_semantics=("arbitrary",) * n_axes, vmem_limit_bytes=VMEM_LIMIT)


def _sigmoid(x):
    return 1.0 / (1.0 + jnp.exp(-x))


def _silu(x):
    return x * _sigmoid(x)


def _rms_scale(x):
    return lax.rsqrt(jnp.mean(x * x, axis=-1, keepdims=True) + EPS)


def _rmsnorm_kernel(x_ref, w_ref, o_ref):
    x = x_ref[...]
    o_ref[...] = (x * _rms_scale(x) * w_ref[...]).astype(o_ref.dtype)


def _rmsnorm(x, w, tm=512):
    m, d = x.shape
    return pl.pallas_call(
        _rmsnorm_kernel,
        out_shape=jax.ShapeDtypeStruct((m, d), BF16),
        grid=(m // tm,),
        in_specs=[pl.BlockSpec((tm, d), lambda i: (i, 0)),
                  pl.BlockSpec((1, d), lambda i: (0, 0))],
        out_specs=pl.BlockSpec((tm, d), lambda i: (i, 0)),
        compiler_params=_cparams(1),
        name="rmsnorm",
    )(x, w)


def _gateup_kernel(u_ref, wg_ref, wu_ref, o_ref, wg_bf, wu_bf):
    @pl.when(pl.program_id(1) == 0)
    def _():
        wg_bf[...] = wg_ref[...].astype(BF16)
        wu_bf[...] = wu_ref[...].astype(BF16)

    u = u_ref[...]
    g = jnp.dot(u, wg_bf[...], preferred_element_type=F32)
    up = jnp.dot(u, wu_bf[...], preferred_element_type=F32)
    o_ref[...] = (_silu(g) * up).astype(o_ref.dtype)


def _gateup(u, w_gate, w_up, tm=1024, tn=512):
    m, d = u.shape
    f = w_gate.shape[-1]
    w_spec = pl.BlockSpec((None, d, tn), lambda j, i: (0, 0, j))
    return pl.pallas_call(
        _gateup_kernel,
        out_shape=jax.ShapeDtypeStruct((m, f), BF16),
        grid=(pl.cdiv(f, tn), m // tm),
        in_specs=[pl.BlockSpec((tm, d), lambda j, i: (i, 0)), w_spec, w_spec],
        out_specs=pl.BlockSpec((tm, tn), lambda j, i: (i, j)),
        scratch_shapes=[pltpu.VMEM((d, tn), BF16), pltpu.VMEM((d, tn), BF16)],
        compiler_params=_cparams(2),
        name="ffn_gateup",
    )(u, w_gate, w_up)


def _proj_kernel(u_ref, w_ref, *rest, shift, tn):
    if shift:
        wn_ref, o_ref, w_bf = rest
    else:
        o_ref, w_bf = rest

    @pl.when(pl.program_id(1) == 0)
    def _():
        w = w_ref[...]
        if shift:
            w = jnp.concatenate([w, wn_ref[...]], axis=0)[shift:shift + tn]
        w_bf[...] = w.astype(BF16)

    o_ref[...] = _dot_nt(u_ref[...], w_bf[...]).astype(o_ref.dtype)


def _proj(u, wt, col0, n_cols, out_dtype, tm=1024, tn=1024):
    m, d = u.shape
    shift = col0 % tn
    blk0 = col0 // tn
    assert n_cols % tn == 0 and shift % SUBLANES == 0
    in_specs = [pl.BlockSpec((tm, d), lambda j, i: (i, 0)),
                pl.BlockSpec((None, tn, d), lambda j, i: (0, j + blk0, 0))]
    args = [u, wt]
    if shift:
        assert tn % shift == 0
        per_tile = tn // shift
        in_specs.append(pl.BlockSpec((None, shift, d), lambda j, i: (0, (j + blk0 + 1) * per_tile, 0)))
        args.append(wt)
    return pl.pallas_call(
        functools.partial(_proj_kernel, shift=shift, tn=tn),
        out_shape=jax.ShapeDtypeStruct((m, n_cols), out_dtype),
        grid=(n_cols // tn, m // tm),
        in_specs=in_specs,
        out_specs=pl.BlockSpec((tm, tn), lambda j, i: (i, j)),
        scratch_shapes=[pltpu.VMEM((tn, d), BF16)],
        compiler_params=_cparams(2),
        name="in_proj",
    )(*args)


def _rows_kernel(x_ref, w_ref, res_ref, post_ref, *rest, n_col_tiles, tn, res_scale, emit_next):
    if emit_next:
        next_ref, h_ref, u_ref, acc_ref = rest
    else:
        h_ref, acc_ref = rest
    j = pl.program_id(1)
    acc_ref[j] = jnp.dot(x_ref[...], w_ref[...], preferred_element_type=F32)

    @pl.when(j == n_col_tiles - 1)
    def _():
        d = n_col_tiles * tn
        ssq = None
        for jj in range(n_col_tiles):
            a = acc_ref[jj]
            s = jnp.sum(a * a, axis=-1, keepdims=True)
            ssq = s if ssq is None else ssq + s
        scale = lax.rsqrt(ssq / d + EPS) * res_scale
        hsq = None
        for jj in range(n_col_tiles):
            cols = slice(jj * tn, (jj + 1) * tn)
            h = res_ref[:, cols] + acc_ref[jj] * scale * post_ref[:, cols]
            h_ref[:, cols] = h
            if emit_next:
                s = jnp.sum(h * h, axis=-1, keepdims=True)
                hsq = s if hsq is None else hsq + s
        if emit_next:
            nscale = lax.rsqrt(hsq / d + EPS)
            for jj in range(n_col_tiles):
                cols = slice(jj * tn, (jj + 1) * tn)
                u_ref[:, cols] = (h_ref[:, cols] * nscale * next_ref[:, cols]).astype(u_ref.dtype)


def _rows(x, w, res, post_w, next_w, res_scale, tm=512, tn=512):
    m, k = x.shape
    d = w.shape[-1]
    nj = d // tn
    emit_next = next_w is not None
    row_spec = pl.BlockSpec((tm, d), lambda i, j: (i, 0))
    vec_spec = pl.BlockSpec((1, d), lambda i, j: (0, 0))
    in_specs = [pl.BlockSpec((tm, k), lambda i, j: (i, 0)),
                pl.BlockSpec((k, tn), lambda i, j: (0, j)),
                row_spec, vec_spec]
    args = [x, w, res, post_w]
    out_shape = [jax.ShapeDtypeStruct((m, d), F32)]
    out_specs = [row_spec]
    if emit_next:
        in_specs.append(vec_spec)
        args.append(next_w)
        out_shape.append(jax.ShapeDtypeStruct((m, d), BF16))
        out_specs.append(row_spec)
    outs = pl.pallas_call(
        functools.partial(_rows_kernel, n_col_tiles=nj, tn=tn, res_scale=res_scale,
                          emit_next=emit_next),
        out_shape=out_shape,
        grid=(m // tm, nj),
        in_specs=in_specs,
        out_specs=out_specs,
        scratch_shapes=[pltpu.VMEM((nj, tm, tn), F32)],
        compiler_params=_cparams(2),
        name="rows_matmul_norm",
    )(*args)
    return outs if emit_next else (outs[0], None)


def _split_bf16(x):
    hi = x.astype(BF16)
    lo = (x - hi.astype(F32)).astype(BF16)
    return hi, lo


def _dot_nt(a, b):
    return lax.dot_general(a, b, (((1,), (1,)), ((), ())), preferred_element_type=F32)


def _dot_tn(a, b):
    return lax.dot_general(a, b, (((0,), (0,)), ((), ())), preferred_element_type=F32)


N_SUB = CHUNK // SUB


def _chunk_consts():
    i = lax.broadcasted_iota(jnp.int32, (CHUNK, CHUNK), 0)
    j = lax.broadcasted_iota(jnp.int32, (CHUNK, CHUNK), 1)
    lo = (i // SUB) * SUB
    hi = lo + SUB
    groups = [(j >= lo) & (j <= i), (j > i) & (j < hi), j < lo, j >= hi]
    sel = jnp.concatenate([jnp.where(g, 1.0, 0.0) for g in groups], axis=0).astype(BF16)
    return jnp.concatenate([sel, sel], axis=1), i >= j


def _decay_sums(sel2, g):
    g_hi, g_lo = _split_bf16(g * LOG2_E)
    sums = jnp.dot(sel2, jnp.concatenate([g_hi, g_lo], axis=0), preferred_element_type=F32)
    return [sums[n * CHUNK:(n + 1) * CHUNK] for n in range(4)]


def _gated_chunk(q, k, v, dec, st, causal):
    within, rest, before, after = dec
    dk = q.shape[1]
    cum = within + before
    q_st = (q * jnp.exp2(cum)).astype(BF16)
    k_st = (k * jnp.exp2(rest + after)).astype(BF16)
    q_in = (q * jnp.exp2(within)).astype(BF16)
    k_diag = (k * jnp.exp2(jnp.minimum(-within, EXP2_CLAMP))).astype(BF16)
    k_end = k * jnp.exp2(rest)
    k_end_bf = k_end.astype(BF16)

    blk = lambda x, b: x[b * SUB:(b + 1) * SUB]
    whole = {b: jnp.exp2(within[(b + 1) * SUB - 1:(b + 1) * SUB]) for b in range(1, N_SUB - 1)}
    skip = {}
    for bq in range(N_SUB):
        for bk in range(bq - 1):
            d = whole[bk + 1]
            for mid in range(bk + 2, bq):
                d = d * whole[mid]
            skip[bq, bk] = d
    zero = jnp.zeros((SUB, dk), BF16)
    k_cols, q_cols = [], []
    for bq in range(N_SUB):
        col = []
        for bk in range(N_SUB):
            if bk > bq:
                col.append(zero)
            elif bk == bq:
                col.append(blk(k_diag, bk))
            elif bk == bq - 1:
                col.append(blk(k_end_bf, bk))
            else:
                col.append((blk(k_end, bk) * skip[bq, bk]).astype(BF16))
        k_cols.append(jnp.concatenate(col, axis=0))
        q_cols.append(jnp.concatenate([blk(q_in, b) if b == bq else zero for b in range(N_SUB)],
                                      axis=0))
    scores = _dot_nt(jnp.concatenate(q_cols, axis=1), jnp.concatenate(k_cols, axis=1))
    scores = jnp.where(causal, scores, 0.0).astype(BF16)

    v_t = v.astype(F32).T.astype(BF16)
    out = _dot_nt(jnp.concatenate([q_st, scores], axis=1),
                  jnp.concatenate([st.astype(BF16), v_t], axis=1))
    st_new = st * jnp.exp2(cum[CHUNK - 1:CHUNK]) + jnp.dot(v_t, k_st, preferred_element_type=F32)
    return out, st_new


def _head_norm_gate(o, norm_w, gate):
    return (o * _rms_scale(o) * norm_w * _silu(gate.astype(F32))).astype(BF16)


def _gla_kernel(q_ref, k_ref, v_ref, go_ref, code_ref, wgk_ref, bgk_ref, gn_ref, o_ref, st_ref,
                *, n_chunks, dk, dv):
    @pl.when(pl.program_id(1) == 0)
    def _():
        st_ref[...] = jnp.zeros_like(st_ref)

    sel2, causal = _chunk_consts()
    w_hi, w_lo = _split_bf16(wgk_ref[...])
    bias = bgk_ref[...]
    norm_w = gn_ref[...]
    q_scale = dk ** -0.5

    def chunk_body(c, carry):
        rows = pl.ds(pl.multiple_of(c * CHUNK, CHUNK), CHUNK)
        c_hi, c_lo = _split_bf16(code_ref[rows, :])
        z = (jnp.dot(c_hi, w_hi, preferred_element_type=F32)
             + jnp.dot(c_lo, w_hi, preferred_element_type=F32)
             + jnp.dot(c_hi, w_lo, preferred_element_type=F32)) + bias
        log_a = (jnp.minimum(z, 0.0) - jnp.log1p(jnp.exp(-jnp.abs(z)))) * (1.0 / GLA_GATE_NORMALIZER)
        sums = _decay_sums(sel2, log_a)
        for h in range(GLA_HEADS):
            kc = slice(h * dk, (h + 1) * dk)
            vc = slice(h * dv, (h + 1) * dv)
            q = q_ref[rows, kc].astype(F32) * q_scale
            k = k_ref[rows, kc].astype(F32)
            out, st_new = _gated_chunk(q, k, v_ref[rows, vc], [s[:, kc] for s in sums],
                                       st_ref[h], causal)
            st_ref[h] = st_new
            o_ref[rows, vc] = _head_norm_gate(out, norm_w, go_ref[rows, vc])
        return carry

    lax.fori_loop(0, n_chunks, chunk_body, 0)


def _gla(p_a, p_code, wgk_pad, b_gk, gla_norm, batch, seq, t_blk=512):
    kw = wgk_pad.shape[1]
    vw = (p_a.shape[1] - 2 * kw) // 2
    dk = kw // GLA_HEADS
    dv = vw // GLA_HEADS
    nt = seq // t_blk
    row = lambda b, t: b * nt + t
    return pl.pallas_call(
        functools.partial(_gla_kernel, n_chunks=t_blk // CHUNK, dk=dk, dv=dv),
        out_shape=jax.ShapeDtypeStruct((batch * seq, vw), BF16),
        grid=(batch, nt),
        in_specs=[pl.BlockSpec((t_blk, kw), lambda b, t: (row(b, t), 0)),
                  pl.BlockSpec((t_blk, kw), lambda b, t: (row(b, t), 1)),
                  pl.BlockSpec((t_blk, vw), lambda b, t: (row(b, t), 1)),
                  pl.BlockSpec((t_blk, vw), lambda b, t: (row(b, t), 2)),
                  pl.BlockSpec((t_blk, LANES), lambda b, t: (row(b, t), 0)),
                  pl.BlockSpec((LANES, kw), lambda b, t: (0, 0)),
                  pl.BlockSpec((1, kw), lambda b, t: (0, 0)),
                  pl.BlockSpec((1, dv), lambda b, t: (0, 0))],
        out_specs=pl.BlockSpec((t_blk, vw), lambda b, t: (row(b, t), 0)),
        scratch_shapes=[pltpu.VMEM((GLA_HEADS, dv, dk), F32)],
        compiler_params=_cparams(2),
        name="gla_mixer",
    )(p_a, p_a, p_a, p_a, p_code, wgk_pad, b_gk, gla_norm)


def _hgrn_kernel(hq_ref, hi_ref, ho_ref, hf_ref, lbl_ref, hn_ref, o_ref, st_ref,
                 *, n_chunks, n_heads, dk, layer):
    @pl.when(pl.program_id(1) == 0)
    def _():
        st_ref[...] = jnp.zeros_like(st_ref)

    sel2, causal = _chunk_consts()
    logits = lbl_ref[...]
    p = jnp.exp(logits - jnp.max(logits, axis=0, keepdims=True))
    p = p / jnp.sum(p, axis=0, keepdims=True)
    lb = jnp.sum(p[:layer + 1], axis=0, keepdims=True)
    one_m_lb = jnp.sum(p[layer + 1:], axis=0, keepdims=True)
    norm_w = hn_ref[...]

    def chunk_body(c, carry):
        rows = pl.ds(pl.multiple_of(c * CHUNK, CHUNK), CHUNK)
        hf = hf_ref[rows, :]
        e = jnp.exp(-jnp.abs(hf))
        inv = 1.0 / (1.0 + e)
        pos = hf >= 0.0
        sig = jnp.where(pos, inv, e * inv)
        sig_neg = jnp.where(pos, e * inv, inv)
        log_f = jnp.log(lb + one_m_lb * sig)
        k_all = one_m_lb * sig_neg
        sums = _decay_sums(sel2, log_f)
        for h in range(n_heads):
            hc = slice(h * dk, (h + 1) * dk)
            q = _silu(hq_ref[rows, hc].astype(F32))
            out, st_new = _gated_chunk(q, k_all[:, hc], hi_ref[rows, hc], [s[:, hc] for s in sums],
                                       st_ref[h], causal)
            st_ref[h] = st_new
            o_ref[rows, hc] = _head_norm_gate(out, norm_w, ho_ref[rows, hc])
        return carry

    lax.fori_loop(0, n_chunks, chunk_body, 0)


def _hgrn(p_q, p_r, p_f, lb_logits, hgrn_norm, layer, batch, seq, t_blk=512):
    w = p_f.shape[1]
    dk = HGRN_EXPAND
    n_heads = w // dk
    nt = seq // t_blk
    n_lb = lb_logits.shape[0]
    row = lambda b, t: b * nt + t
    return pl.pallas_call(
        functools.partial(_hgrn_kernel, n_chunks=t_blk // CHUNK, n_heads=n_heads, dk=dk,
                          layer=layer),
        out_shape=jax.ShapeDtypeStruct((batch * seq, w), BF16),
        grid=(batch, nt),
        in_specs=[pl.BlockSpec((t_blk, w), lambda b, t: (row(b, t), 0)),
                  pl.BlockSpec((t_blk, w), lambda b, t: (row(b, t), 0)),
                  pl.BlockSpec((t_blk, w), lambda b, t: (row(b, t), 1)),
                  pl.BlockSpec((t_blk, w), lambda b, t: (row(b, t), 0)),
                  pl.BlockSpec((n_lb, w), lambda b, t: (0, 0)),
                  pl.BlockSpec((1, dk), lambda b, t: (0, 0))],
        out_specs=pl.BlockSpec((t_blk, w), lambda b, t: (row(b, t), 0)),
        scratch_shapes=[pltpu.VMEM((n_heads, dk, dk), F32)],
        compiler_params=_cparams(2),
        name="hgrn_mixer",
    )(p_q, p_r, p_r, p_f, lb_logits, hgrn_norm)


def _merge_kernel(og_ref, oh_ref, zg_ref, zh_ref, wg_ref, wh_ref, bg_ref, bh_ref, o_ref,
                  wg_bf, wh_bf):
    @pl.when(pl.program_id(1) == 0)
    def _():
        wg_bf[...] = wg_ref[...].astype(BF16)
        wh_bf[...] = wh_ref[...].astype(BF16)

    a = jnp.dot(og_ref[...], wg_bf[...], preferred_element_type=F32)
    b = jnp.dot(oh_ref[...], wh_bf[...], preferred_element_type=F32)
    o_ref[...] = (_sigmoid(zg_ref[...].astype(F32) + bg_ref[...]) * a
                  + _sigmoid(zh_ref[...].astype(F32) + bh_ref[...]) * b).astype(o_ref.dtype)


def _merge(o_gla, o_hgrn, p_b, zg_col0, zh_col0, w_bg, w_bh, b_gates, tm=1024, tn=1024):
    m, kdim = o_gla.shape
    d = w_bg.shape[-1]
    x_spec = pl.BlockSpec((tm, kdim), lambda j, i: (i, 0))
    w_spec = pl.BlockSpec((None, kdim, tn), lambda j, i: (0, 0, j))
    return pl.pallas_call(
        _merge_kernel,
        out_shape=jax.ShapeDtypeStruct((m, d), BF16),
        grid=(d // tn, m // tm),
        in_specs=[x_spec, x_spec,
                  pl.BlockSpec((tm, tn), lambda j, i: (i, zg_col0 // tn + j)),
                  pl.BlockSpec((tm, tn), lambda j, i: (i, zh_col0 // tn + j)),
                  w_spec, w_spec,
                  pl.BlockSpec((None, 1, tn), lambda j, i: (0, 0, j)),
                  pl.BlockSpec((None, 1, tn), lambda j, i: (1, 0, j))],
        out_specs=pl.BlockSpec((tm, tn), lambda j, i: (i, j)),
        scratch_shapes=[pltpu.VMEM((kdim, tn), BF16), pltpu.VMEM((kdim, tn), BF16)],
        compiler_params=_cparams(2),
        name="branch_merge",
    )(o_gla, o_hgrn, p_b, p_b, w_bg, w_bh, b_gates, b_gates)


def kernel(x, ffn1_pre_norm, ffn1_w_gate, ffn1_w_up, ffn1_w_down, ffn1_post_norm, mix_pre_norm, w_in, gla_w_gk_up, gla_b_gk, gla_norm, hgrn_lb_logits, hgrn_norm, w_branch_gla, w_branch_hgrn, b_branch_gates, w_out, mix_post_norm, ffn2_pre_norm, ffn2_w_gate, ffn2_w_up, ffn2_w_down, ffn2_post_norm):
    batch, seq, d_model = x.shape
    depth = ffn1_w_gate.shape[0]
    m = batch * seq
    kw = gla_w_gk_up.shape[-1]
    vw = d_model // 2
    a_cols = 2 * kw + 2 * vw
    code0 = a_cols
    hq0 = code0 + GLA_GATE_RANK
    hf0, hi0 = hq0 + vw, hq0 + 2 * vw

    h = x.reshape(m, d_model)
    u = _rmsnorm(h, ffn1_pre_norm[0:1])
    for l in range(depth):
        mid = _gateup(u, ffn1_w_gate[l:l + 1], ffn1_w_up[l:l + 1])
        h, u = _rows(mid, ffn1_w_down[l].astype(BF16), h, ffn1_post_norm[l:l + 1],
                     mix_pre_norm[l:l + 1], 0.5)

        w_l = jnp.swapaxes(w_in[l:l + 1], 1, 2)
        p_a = _proj(u, w_l, 0, a_cols, BF16)
        p_code = _proj(u, w_l, code0, LANES, F32, tn=LANES)
        p_q = _proj(u, w_l, hq0, vw, BF16)
        p_f = _proj(u, w_l, hf0, vw, F32)
        p_r = _proj(u, w_l, hi0, 2 * vw + 2 * d_model, BF16)
        wgk_pad = jnp.pad(gla_w_gk_up[l], ((0, LANES - GLA_GATE_RANK), (0, 0)))
        o_gla = _gla(p_a, p_code, wgk_pad, gla_b_gk[l:l + 1], gla_norm[l:l + 1], batch, seq)
        o_hgrn = _hgrn(p_q, p_r, p_f, hgrn_lb_logits, hgrn_norm[l:l + 1], l, batch, seq)
        merged = _merge(o_gla, o_hgrn, p_r, 2 * vw, 2 * vw + d_model,
                        w_branch_gla[l:l + 1], w_branch_hgrn[l:l + 1],
                        b_branch_gates[l].reshape(2, 1, d_model))
        h, u = _rows(merged, w_out[l].astype(BF16), h, mix_post_norm[l:l + 1],
                     ffn2_pre_norm[l:l + 1], 1.0, tn=d_model)

        mid = _gateup(u, ffn2_w_gate[l:l + 1], ffn2_w_up[l:l + 1])
        next_norm = ffn1_pre_norm[l + 1:l + 2] if l + 1 < depth else None
        h, u = _rows(mid, ffn2_w_down[l].astype(BF16), h, ffn2_post_norm[l:l + 1], next_norm, 0.5)
    return h.reshape(batch, seq, d_model)
```

```python
import functools

import jax
import jax.numpy as jnp
from jax import lax
from jax.experimental import pallas as pl
from jax.experimental.pallas import tpu as pltpu

F32 = jnp.float32
BF16 = jnp.bfloat16

EPS = 1e-6
CHUNK = 64
SUB = 16
GLA_HEADS = 4
GLA_GATE_RANK = 16
GLA_GATE_NORMALIZER = 16.0
HGRN_EXPAND = 128
LOG2_E = 1.4426950408889634
EXP2_CLAMP = 115.0

LANES = 128
SUBLANES = 8
VMEM_LIMIT = 56 * 1024 * 1024


def _cparams(n_axes):
    return pltpu.CompilerParams(
        dimension_semantics=("arbitrary",) * n_axes, vmem_limit_bytes=VMEM_LIMIT)


def _sigmoid(x):
    return 1.0 / (1.0 + jnp.exp(-x))


def _silu(x):
    return x * _sigmoid(x)


def _rms_scale(x):
    return lax.rsqrt(jnp.mean(x * x, axis=-1, keepdims=True) + EPS)


def _rmsnorm_kernel(x_ref, w_ref, o_ref):
    x = x_ref[...]
    o_ref[...] = (x * _rms_scale(x) * w_ref[...]).astype(o_ref.dtype)


def _rmsnorm(x, w, tm=512):
    m, d = x.shape
    return pl.pallas_call(
        _rmsnorm_kernel,
        out_shape=jax.ShapeDtypeStruct((m, d), BF16),
        grid=(m // tm,),
        in_specs=[pl.BlockSpec((tm, d), lambda i: (i, 0)),
                  pl.BlockSpec((1, d), lambda i: (0, 0))],
        out_specs=pl.BlockSpec((tm, d), lambda i: (i, 0)),
        compiler_params=_cparams(1),
        name="rmsnorm",
    )(x, w)


def _gateup_kernel(u_ref, wg_ref, wu_ref, o_ref, wg_bf, wu_bf):
    @pl.when(pl.program_id(1) == 0)
    def _():
        wg_bf[...] = wg_ref[...].astype(BF16)
        wu_bf[...] = wu_ref[...].astype(BF16)

    u = u_ref[...]
    g = jnp.dot(u, wg_bf[...], preferred_element_type=F32)
    up = jnp.dot(u, wu_bf[...], preferred_element_type=F32)
    o_ref[...] = (_silu(g) * up).astype(o_ref.dtype)


def _gateup(u, w_gate, w_up, tm=1024, tn=512):
    m, d = u.shape
    f = w_gate.shape[-1]
    w_spec = pl.BlockSpec((None, d, tn), lambda j, i: (0, 0, j))
    return pl.pallas_call(
        _gateup_kernel,
        out_shape=jax.ShapeDtypeStruct((m, f), BF16),
        grid=(pl.cdiv(f, tn), m // tm),
        in_specs=[pl.BlockSpec((tm, d), lambda j, i: (i, 0)), w_spec, w_spec],
        out_specs=pl.BlockSpec((tm, tn), lambda j, i: (i, j)),
        scratch_shapes=[pltpu.VMEM((d, tn), BF16), pltpu.VMEM((d, tn), BF16)],
        compiler_params=_cparams(2),
        name="ffn_gateup",
    )(u, w_gate, w_up)


def _proj_kernel(u_ref, w_ref, *rest, shift, tn):
    if shift:
        wn_ref, o_ref, w_bf = rest
    else:
        o_ref, w_bf = rest

    @pl.when(pl.program_id(1) == 0)
    def _():
        w = w_ref[...]
        if shift:
            w = jnp.concatenate([w, wn_ref[...]], axis=0)[shift:shift + tn]
        w_bf[...] = w.astype(BF16)

    o_ref[...] = _dot_nt(u_ref[...], w_bf[...]).astype(o_ref.dtype)


def _proj(u, wt, col0, n_cols, out_dtype, tm=1024, tn=1024):
    m, d = u.shape
    shift = col0 % tn
    blk0 = col0 // tn
    assert n_cols % tn == 0 and shift % SUBLANES == 0
    in_specs = [pl.BlockSpec((tm, d), lambda j, i: (i, 0)),
                pl.BlockSpec((None, tn, d), lambda j, i: (0, j + blk0, 0))]
    args = [u, wt]
    if shift:
        assert tn % shift == 0
        per_tile = tn // shift
        in_specs.append(pl.BlockSpec((None, shift, d), lambda j, i: (0, (j + blk0 + 1) * per_tile, 0)))
        args.append(wt)
    return pl.pallas_call(
        functools.partial(_proj_kernel, shift=shift, tn=tn),
        out_shape=jax.ShapeDtypeStruct((m, n_cols), out_dtype),
        grid=(n_cols // tn, m // tm),
        in_specs=in_specs,
        out_specs=pl.BlockSpec((tm, tn), lambda j, i: (i, j)),
        scratch_shapes=[pltpu.VMEM((tn, d), BF16)],
        compiler_params=_cparams(2),
        name="in_proj",
    )(*args)


def _rows_kernel(x_ref, w_ref, res_ref, post_ref, *rest, n_col_tiles, tn, res_scale, emit_next):
    if emit_next:
        next_ref, h_ref, u_ref, acc_ref = rest
    else:
        h_ref, acc_ref = rest
    j = pl.program_id(1)
    acc_ref[j] = jnp.dot(x_ref[...], w_ref[...], preferred_element_type=F32)

    @pl.when(j == n_col_tiles - 1)
    def _():
        d = n_col_tiles * tn
        ssq = None
        for jj in range(n_col_tiles):
            a = acc_ref[jj]
            s = jnp.sum(a * a, axis=-1, keepdims=True)
            ssq = s if ssq is None else ssq + s
        scale = lax.rsqrt(ssq / d + EPS) * res_scale
        hsq = None
        for jj in range(n_col_tiles):
            cols = slice(jj * tn, (jj + 1) * tn)
            h = res_ref[:, cols] + acc_ref[jj] * scale * post_ref[:, cols]
            h_ref[:, cols] = h
            if emit_next:
                s = jnp.sum(h * h, axis=-1, keepdims=True)
                hsq = s if hsq is None else hsq + s
        if emit_next:
            nscale = lax.rsqrt(hsq / d + EPS)
            for jj in range(n_col_tiles):
                cols = slice(jj * tn, (jj + 1) * tn)
                u_ref[:, cols] = (h_ref[:, cols] * nscale * next_ref[:, cols]).astype(u_ref.dtype)


def _rows(x, w, res, post_w, next_w, res_scale, tm=512, tn=512):
    m, k = x.shape
    d = w.shape[-1]
    nj = d // tn
    emit_next = next_w is not None
    row_spec = pl.BlockSpec((tm, d), lambda i, j: (i, 0))
    vec_spec = pl.BlockSpec((1, d), lambda i, j: (0, 0))
    in_specs = [pl.BlockSpec((tm, k), lambda i, j: (i, 0)),
                pl.BlockSpec((k, tn), lambda i, j: (0, j)),
                row_spec, vec_spec]
    args = [x, w, res, post_w]
    out_shape = [jax.ShapeDtypeStruct((m, d), F32)]
    out_specs = [row_spec]
    if emit_next:
        in_specs.append(vec_spec)
        args.append(next_w)
        out_shape.append(jax.ShapeDtypeStruct((m, d), BF16))
        out_specs.append(row_spec)
    outs = pl.pallas_call(
        functools.partial(_rows_kernel, n_col_tiles=nj, tn=tn, res_scale=res_scale,
                          emit_next=emit_next),
        out_shape=out_shape,
        grid=(m // tm, nj),
        in_specs=in_specs,
        out_specs=out_specs,
        scratch_shapes=[pltpu.VMEM((nj, tm, tn), F32)],
        compiler_params=_cparams(2),
        name="rows_matmul_norm",
    )(*args)
    return outs if emit_next else (outs[0], None)


def _split_bf16(x):
    hi = x.astype(BF16)
    lo = (x - hi.astype(F32)).astype(BF16)
    return hi, lo


def _dot_nt(a, b):
    return lax.dot_general(a, b, (((1,), (1,)), ((), ())), preferred_element_type=F32)


def _dot_tn(a, b):
    return lax.dot_general(a, b, (((0,), (0,)), ((), ())), preferred_element_type=F32)


N_SUB = CHUNK // SUB


def _chunk_consts():
    i = lax.broadcasted_iota(jnp.int32, (CHUNK, CHUNK), 0)
    j = lax.broadcasted_iota(jnp.int32, (CHUNK, CHUNK), 1)
    lo = (i // SUB) * SUB
    hi = lo + SUB
    groups = [(j >= lo) & (j <= i), (j > i) & (j < hi), j < lo, j >= hi]
    sel = jnp.concatenate([jnp.where(g, 1.0, 0.0) for g in groups], axis=0).astype(BF16)
    return jnp.concatenate([sel, sel], axis=1), i >= j


def _decay_sums(sel2, g):
    g_hi, g_lo = _split_bf16(g * LOG2_E)
    return jnp.dot(sel2, jnp.concatenate([g_hi, g_lo], axis=0), preferred_element_type=F32)


def _gated_chunks(heads, causal):
    prep = [_chunk_operands(*h) for h in heads]
    scores = [_dot_nt(p[0], p[1]) for p in prep]
    kv = [jnp.dot(p[4], p[3], preferred_element_type=F32) for p in prep]
    outs, states = [], []
    for (q_t, k_t, q_st, k_st, v_t, st_bf, st_decay), s, upd, h in zip(prep, scores, kv, heads):
        s = jnp.where(causal, s, 0.0).astype(BF16)
        outs.append(_dot_nt(jnp.concatenate([q_st, s], axis=1),
                            jnp.concatenate([st_bf, v_t], axis=1)))
        states.append(h[4] * st_decay + upd)
    return outs, states


def _chunk_operands(q, k, v, dec, st):
    within, rest, before, after = dec
    dk = q.shape[1]
    cum = within + before
    q_st = (q * jnp.exp2(cum)).astype(BF16)
    k_st = (k * jnp.exp2(rest + after)).astype(BF16)
    q_in = (q * jnp.exp2(within)).astype(BF16)
    k_diag = (k * jnp.exp2(jnp.minimum(-within, EXP2_CLAMP))).astype(BF16)
    k_end = k * jnp.exp2(rest)
    k_end_bf = k_end.astype(BF16)

    blk = lambda x, b: x[b * SUB:(b + 1) * SUB]
    whole = {b: jnp.exp2(within[(b + 1) * SUB - 1:(b + 1) * SUB]) for b in range(1, N_SUB - 1)}
    skip = {}
    for bq in range(N_SUB):
        for bk in range(bq - 1):
            d = whole[bk + 1]
            for mid in range(bk + 2, bq):
                d = d * whole[mid]
            skip[bq, bk] = d
    zero = jnp.zeros((SUB, dk), BF16)
    k_cols, q_cols = [], []
    for bq in range(N_SUB):
        col = []
        for bk in range(N_SUB):
            if bk > bq:
                col.append(zero)
            elif bk == bq:
                col.append(blk(k_diag, bk))
            elif bk == bq - 1:
                col.append(blk(k_end_bf, bk))
            else:
                col.append((blk(k_end, bk) * skip[bq, bk]).astype(BF16))
        k_cols.append(jnp.concatenate(col, axis=0))
        q_cols.append(jnp.concatenate([blk(q_in, b) if b == bq else zero for b in range(N_SUB)],
                                      axis=0))
    q_tilde = jnp.concatenate(q_cols, axis=1)
    k_tilde = jnp.concatenate(k_cols, axis=1)
    v_t = v.T
    return q_tilde, k_tilde, q_st, k_st, v_t, st.astype(BF16), jnp.exp2(cum[CHUNK - 1:CHUNK])


def _head_norm_gate(o, norm_w, gate):
    return (o * _rms_scale(o) * norm_w * _silu(gate.astype(F32))).astype(BF16)


def _gla_kernel(q_ref, k_ref, v_ref, go_ref, code_ref, wgk_ref, bgk_ref, gn_ref, o_ref, st_ref,
                sums_ref, *, n_chunks, dk, dv):
    @pl.when(pl.program_id(1) == 0)
    def _():
        st_ref[...] = jnp.zeros_like(st_ref)

    sel2, causal = _chunk_consts()
    norm_w = gn_ref[...]
    q_scale = dk ** -0.5

    w_hi, w_lo = _split_bf16(wgk_ref[...])
    c_hi, c_lo = _split_bf16(code_ref[...])
    z = (jnp.dot(c_hi, w_hi, preferred_element_type=F32)
         + jnp.dot(c_lo, w_hi, preferred_element_type=F32)
         + jnp.dot(c_hi, w_lo, preferred_element_type=F32)) + bgk_ref[...]
    log_a = (jnp.minimum(z, 0.0) - jnp.log1p(jnp.exp(-jnp.abs(z)))) * (1.0 / GLA_GATE_NORMALIZER)
    for c in range(n_chunks):
        sums_ref[c] = _decay_sums(sel2, log_a[c * CHUNK:(c + 1) * CHUNK])

    def chunk_body(c, carry):
        rows = pl.ds(pl.multiple_of(c * CHUNK, CHUNK), CHUNK)
        heads = []
        for h in range(GLA_HEADS):
            kc = slice(h * dk, (h + 1) * dk)
            q = q_ref[rows, kc].astype(F32) * q_scale
            k = k_ref[rows, kc].astype(F32)
            dec = [sums_ref[c, n * CHUNK:(n + 1) * CHUNK, kc] for n in range(4)]
            heads.append((q, k, v_ref[rows, h * dv:(h + 1) * dv], dec, st_ref[h]))
        outs, states = _gated_chunks(heads, causal)
        for h in range(GLA_HEADS):
            vc = slice(h * dv, (h + 1) * dv)
            st_ref[h] = states[h]
            o_ref[rows, vc] = _head_norm_gate(outs[h], norm_w, go_ref[rows, vc])
        return carry

    lax.fori_loop(0, n_chunks, chunk_body, 0)


def _gla(p_a, p_code, wgk_pad, b_gk, gla_norm, batch, seq, t_blk=512):
    kw = wgk_pad.shape[1]
    vw = (p_a.shape[1] - 2 * kw) // 2
    dk = kw // GLA_HEADS
    dv = vw // GLA_HEADS
    nt = seq // t_blk
    row = lambda b, t: b * nt + t
    return pl.pallas_call(
        functools.partial(_gla_kernel, n_chunks=t_blk // CHUNK, dk=dk, dv=dv),
        out_shape=jax.ShapeDtypeStruct((batch * seq, vw), BF16),
        grid=(batch, nt),
        in_specs=[pl.BlockSpec((t_blk, kw), lambda b, t: (row(b, t), 0)),
                  pl.BlockSpec((t_blk, kw), lambda b, t: (row(b, t), 1)),
                  pl.BlockSpec((t_blk, vw), lambda b, t: (row(b, t), 1)),
                  pl.BlockSpec((t_blk, vw), lambda b, t: (row(b, t), 2)),
                  pl.BlockSpec((t_blk, LANES), lambda b, t: (row(b, t), 0)),
                  pl.BlockSpec((LANES, kw), lambda b, t: (0, 0)),
                  pl.BlockSpec((1, kw), lambda b, t: (0, 0)),
                  pl.BlockSpec((1, dv), lambda b, t: (0, 0))],
        out_specs=pl.BlockSpec((t_blk, vw), lambda b, t: (row(b, t), 0)),
        scratch_shapes=[pltpu.VMEM((GLA_HEADS, dv, dk), F32),
                        pltpu.VMEM((t_blk // CHUNK, 4 * CHUNK, kw), F32)],
        compiler_params=_cparams(2),
        name="gla_mixer",
    )(p_a, p_a, p_a, p_a, p_code, wgk_pad, b_gk, gla_norm)


def _hgrn_kernel(hq_ref, hi_ref, ho_ref, hf_ref, lbl_ref, hn_ref, o_ref, st_ref,
                 *, n_chunks, n_heads, dk, layer):
    @pl.when(pl.program_id(1) == 0)
    def _():
        st_ref[...] = jnp.zeros_like(st_ref)

    sel2, causal = _chunk_consts()
    logits = lbl_ref[...]
    p = jnp.exp(logits - jnp.max(logits, axis=0, keepdims=True))
    p = p / jnp.sum(p, axis=0, keepdims=True)
    lb = jnp.sum(p[:layer + 1], axis=0, keepdims=True)
    one_m_lb = jnp.sum(p[layer + 1:], axis=0, keepdims=True)
    norm_w = hn_ref[...]

    def chunk_body(c, carry):
        rows = pl.ds(pl.multiple_of(c * CHUNK, CHUNK), CHUNK)
        hf = hf_ref[rows, :]
        e = jnp.exp(-jnp.abs(hf))
        inv = 1.0 / (1.0 + e)
        pos = hf >= 0.0
        sig = jnp.where(pos, inv, e * inv)
        sig_neg = jnp.where(pos, e * inv, inv)
        log_f = jnp.log(lb + one_m_lb * sig)
        k_all = one_m_lb * sig_neg
        sums = _decay_sums(sel2, log_f)
        heads = []
        for h in range(n_heads):
            hc = slice(h * dk, (h + 1) * dk)
            q = _silu(hq_ref[rows, hc].astype(F32))
            dec = [sums[n * CHUNK:(n + 1) * CHUNK, hc] for n in range(4)]
            heads.append((q, k_all[:, hc], hi_ref[rows, hc], dec, st_ref[h]))
        outs, states = _gated_chunks(heads, causal)
        for h in range(n_heads):
            hc = slice(h * dk, (h + 1) * dk)
            st_ref[h] = states[h]
            o_ref[rows, hc] = _head_norm_gate(outs[h], norm_w, ho_ref[rows, hc])
        return carry

    lax.fori_loop(0, n_chunks, chunk_body, 0)


def _hgrn(p_q, p_r, p_f, lb_logits, hgrn_norm, layer, batch, seq, t_blk=512):
    w = p_f.shape[1]
    dk = HGRN_EXPAND
    n_heads = w // dk
    nt = seq // t_blk
    n_lb = lb_logits.shape[0]
    row = lambda b, t: b * nt + t
    return pl.pallas_call(
        functools.partial(_hgrn_kernel, n_chunks=t_blk // CHUNK, n_heads=n_heads, dk=dk,
                          layer=layer),
        out_shape=jax.ShapeDtypeStruct((batch * seq, w), BF16),
        grid=(batch, nt),
        in_specs=[pl.BlockSpec((t_blk, w), lambda b, t: (row(b, t), 0)),
                  pl.BlockSpec((t_blk, w), lambda b, t: (row(b, t), 0)),
                  pl.BlockSpec((t_blk, w), lambda b, t: (row(b, t), 1)),
                  pl.BlockSpec((t_blk, w), lambda b, t: (row(b, t), 0)),
                  pl.BlockSpec((n_lb, w), lambda b, t: (0, 0)),
                  pl.BlockSpec((1, dk), lambda b, t: (0, 0))],
        out_specs=pl.BlockSpec((t_blk, w), lambda b, t: (row(b, t), 0)),
        scratch_shapes=[pltpu.VMEM((n_heads, dk, dk), F32)],
        compiler_params=_cparams(2),
        name="hgrn_mixer",
    )(p_q, p_r, p_r, p_f, lb_logits, hgrn_norm)


def _merge_kernel(og_ref, oh_ref, zg_ref, zh_ref, wg_ref, wh_ref, bg_ref, bh_ref, o_ref,
                  wg_bf, wh_bf):
    @pl.when(pl.program_id(1) == 0)
    def _():
        wg_bf[...] = wg_ref[...].astype(BF16)
        wh_bf[...] = wh_ref[...].astype(BF16)

    a = jnp.dot(og_ref[...], wg_bf[...], preferred_element_type=F32)
    b = jnp.dot(oh_ref[...], wh_bf[...], preferred_element_type=F32)
    o_ref[...] = (_sigmoid(zg_ref[...].astype(F32) + bg_ref[...]) * a
                  + _sigmoid(zh_ref[...].astype(F32) + bh_ref[...]) * b).astype(o_ref.dtype)


def _merge(o_gla, o_hgrn, p_b, zg_col0, zh_col0, w_bg, w_bh, b_gates, tm=1024, tn=1024):
    m, kdim = o_gla.shape
    d = w_bg.shape[-1]
    x_spec = pl.BlockSpec((tm, kdim), lambda j, i: (i, 0))
    w_spec = pl.BlockSpec((None, kdim, tn), lambda j, i: (0, 0, j))
    return pl.pallas_call(
        _merge_kernel,
        out_shape=jax.ShapeDtypeStruct((m, d), BF16),
        grid=(d // tn, m // tm),
        in_specs=[x_spec, x_spec,
                  pl.BlockSpec((tm, tn), lambda j, i: (i, zg_col0 // tn + j)),
                  pl.BlockSpec((tm, tn), lambda j, i: (i, zh_col0 // tn + j)),
                  w_spec, w_spec,
                  pl.BlockSpec((None, 1, tn), lambda j, i: (0, 0, j)),
                  pl.BlockSpec((None, 1, tn), lambda j, i: (1, 0, j))],
        out_specs=pl.BlockSpec((tm, tn), lambda j, i: (i, j)),
        scratch_shapes=[pltpu.VMEM((kdim, tn), BF16), pltpu.VMEM((kdim, tn), BF16)],
        compiler_params=_cparams(2),
        name="branch_merge",
    )(o_gla, o_hgrn, p_b, p_b, w_bg, w_bh, b_gates, b_gates)


def kernel(x, ffn1_pre_norm, ffn1_w_gate, ffn1_w_up, ffn1_w_down, ffn1_post_norm, mix_pre_norm, w_in, gla_w_gk_up, gla_b_gk, gla_norm, hgrn_lb_logits, hgrn_norm, w_branch_gla, w_branch_hgrn, b_branch_gates, w_out, mix_post_norm, ffn2_pre_norm, ffn2_w_gate, ffn2_w_up, ffn2_w_down, ffn2_post_norm):
    batch, seq, d_model = x.shape
    depth = ffn1_w_gate.shape[0]
    m = batch * seq
    kw = gla_w_gk_up.shape[-1]
    vw = d_model // 2
    a_cols = 2 * kw + 2 * vw
    code0 = a_cols
    hq0 = code0 + GLA_GATE_RANK
    hf0, hi0 = hq0 + vw, hq0 + 2 * vw

    h = x.reshape(m, d_model)
    u = _rmsnorm(h, ffn1_pre_norm[0:1])
    for l in range(depth):
        mid = _gateup(u, ffn1_w_gate[l:l + 1], ffn1_w_up[l:l + 1])
        h, u = _rows(mid, ffn1_w_down[l].astype(BF16), h, ffn1_post_norm[l:l + 1],
                     mix_pre_norm[l:l + 1], 0.5)

        w_l = jnp.swapaxes(w_in[l:l + 1], 1, 2)
        p_a = _proj(u, w_l, 0, a_cols, BF16)
        p_code = _proj(u, w_l, code0, LANES, F32, tn=LANES)
        p_q = _proj(u, w_l, hq0, vw, BF16)
        p_f = _proj(u, w_l, hf0, vw, F32)
        p_r = _proj(u, w_l, hi0, 2 * vw + 2 * d_model, BF16)
        wgk_pad = jnp.pad(gla_w_gk_up[l], ((0, LANES - GLA_GATE_RANK), (0, 0)))
        o_gla = _gla(p_a, p_code, wgk_pad, gla_b_gk[l:l + 1], gla_norm[l:l + 1], batch, seq)
        o_hgrn = _hgrn(p_q, p_r, p_f, hgrn_lb_logits, hgrn_norm[l:l + 1], l, batch, seq)
        merged = _merge(o_gla, o_hgrn, p_r, 2 * vw, 2 * vw + d_model,
                        w_branch_gla[l:l + 1], w_branch_hgrn[l:l + 1],
                        b_branch_gates[l].reshape(2, 1, d_model))
        h, u = _rows(merged, w_out[l].astype(BF16), h, mix_post_norm[l:l + 1],
                     ffn2_pre_norm[l:l + 1], 1.0, tn=d_model)

        mid = _gateup(u, ffn2_w_gate[l:l + 1], ffn2_w_up[l:l + 1])
        next_norm = ffn1_pre_norm[l + 1:l + 2] if l + 1 < depth else None
        h, u = _rows(mid, ffn2_w_down[l].astype(BF16), h, ffn2_post_norm[l:l + 1], next_norm, 0.5)
    return h.reshape(batch, seq, d_model)
```

```python
import functools

import jax
import jax.numpy as jnp
from jax import lax
from jax.experimental import pallas as pl
from jax.experimental.pallas import tpu as pltpu

F32 = jnp.float32
BF16 = jnp.bfloat16

EPS = 1e-6
CHUNK = 64
SUB = 16
GLA_HEADS = 4
GLA_GATE_RANK = 16
GLA_GATE_NORMALIZER = 16.0
HGRN_EXPAND = 128
LOG2_E = 1.4426950408889634
EXP2_CLAMP = 115.0

LANES = 128
SUBLANES = 8
VMEM_LIMIT = 56 * 1024 * 1024


def _cparams(n_axes):
    return pltpu.CompilerParams(
        dimension_semantics=("arbitrary",) * n_axes, vmem_limit_bytes=VMEM_LIMIT)


def _sigmoid(x):
    return 1.0 / (1.0 + jnp.exp(-x))


def _silu(x):
    return x * _sigmoid(x)


def _rms_scale(x):
    return lax.rsqrt(jnp.mean(x * x, axis=-1, keepdims=True) + EPS)


def _rmsnorm_kernel(x_ref, w_ref, o_ref):
    x = x_ref[...]
    o_ref[...] = (x * _rms_scale(x) * w_ref[...]).astype(o_ref.dtype)


def _rmsnorm(x, w, tm=512):
    m, d = x.shape
    return pl.pallas_call(
        _rmsnorm_kernel,
        out_shape=jax.ShapeDtypeStruct((m, d), BF16),
        grid=(m // tm,),
        in_specs=[pl.BlockSpec((tm, d), lambda i: (i, 0)),
                  pl.BlockSpec((1, d), lambda i: (0, 0))],
        out_specs=pl.BlockSpec((tm, d), lambda i: (i, 0)),
        compiler_params=_cparams(1),
        name="rmsnorm",
    )(x, w)


def _gateup_kernel(u_ref, wg_ref, wu_ref, o_ref, wg_bf, wu_bf):
    @pl.when(pl.program_id(1) == 0)
    def _():
        wg_bf[...] = wg_ref[...].astype(BF16)
        wu_bf[...] = wu_ref[...].astype(BF16)

    u = u_ref[...]
    g = jnp.dot(u, wg_bf[...], preferred_element_type=F32)
    up = jnp.dot(u, wu_bf[...], preferred_element_type=F32)
    o_ref[...] = (_silu(g) * up).astype(o_ref.dtype)


def _gateup(u, w_gate, w_up, tm=2048, tn=512):
    m, d = u.shape
    f = w_gate.shape[-1]
    w_spec = pl.BlockSpec((None, d, tn), lambda j, i: (0, 0, j))
    return pl.pallas_call(
        _gateup_kernel,
        out_shape=jax.ShapeDtypeStruct((m, f), BF16),
        grid=(pl.cdiv(f, tn), m // tm),
        in_specs=[pl.BlockSpec((tm, d), lambda j, i: (i, 0)), w_spec, w_spec],
        out_specs=pl.BlockSpec((tm, tn), lambda j, i: (i, j)),
        scratch_shapes=[pltpu.VMEM((d, tn), BF16), pltpu.VMEM((d, tn), BF16)],
        compiler_params=_cparams(2),
        name="ffn_gateup",
    )(u, w_gate, w_up)


def _proj_kernel(u_ref, w_ref, *rest, shift, tn):
    if shift:
        wn_ref, o_ref, w_bf = rest
    else:
        o_ref, w_bf = rest

    @pl.when(pl.program_id(1) == 0)
    def _():
        w = w_ref[...]
        if shift:
            w = jnp.concatenate([w, wn_ref[...]], axis=0)[shift:shift + tn]
        w_bf[...] = w.astype(BF16)

    o_ref[...] = _dot_nt(u_ref[...], w_bf[...]).astype(o_ref.dtype)


def _proj(u, wt, col0, n_cols, out_dtype, tm=2048, tn=1024):
    m, d = u.shape
    shift = col0 % tn
    blk0 = col0 // tn
    assert n_cols % tn == 0 and shift % SUBLANES == 0
    in_specs = [pl.BlockSpec((tm, d), lambda j, i: (i, 0)),
                pl.BlockSpec((None, tn, d), lambda j, i: (0, j + blk0, 0))]
    args = [u, wt]
    if shift:
        assert tn % shift == 0
        per_tile = tn // shift
        in_specs.append(pl.BlockSpec((None, shift, d), lambda j, i: (0, (j + blk0 + 1) * per_tile, 0)))
        args.append(wt)
    return pl.pallas_call(
        functools.partial(_proj_kernel, shift=shift, tn=tn),
        out_shape=jax.ShapeDtypeStruct((m, n_cols), out_dtype),
        grid=(n_cols // tn, m // tm),
        in_specs=in_specs,
        out_specs=pl.BlockSpec((tm, tn), lambda j, i: (i, j)),
        scratch_shapes=[pltpu.VMEM((tn, d), BF16)],
        compiler_params=_cparams(2),
        name="in_proj",
    )(*args)


def _rows_kernel(x_ref, w_ref, res_ref, post_ref, *rest, n_col_tiles, tn, res_scale, emit_next):
    if emit_next:
        next_ref, h_ref, u_ref, acc_ref, ssq_ref = rest
    else:
        h_ref, acc_ref, ssq_ref = rest
    j = pl.program_id(1)
    r = jnp.dot(x_ref[...], w_ref[...], preferred_element_type=F32)
    acc_ref[j] = r
    s = jnp.sum(r * r, axis=-1, keepdims=True)
    ssq_ref[...] = jnp.where(j == 0, s, ssq_ref[...] + s)

    @pl.when(j == n_col_tiles - 1)
    def _():
        d = n_col_tiles * tn
        scale = lax.rsqrt(ssq_ref[...] / d + EPS) * res_scale
        hsq = None
        for jj in range(n_col_tiles):
            cols = slice(jj * tn, (jj + 1) * tn)
            h = res_ref[:, cols] + acc_ref[jj] * scale * post_ref[:, cols]
            h_ref[:, cols] = h
            if emit_next:
                s = jnp.sum(h * h, axis=-1, keepdims=True)
                hsq = s if hsq is None else hsq + s
        if emit_next:
            nscale = lax.rsqrt(hsq / d + EPS)
            for jj in range(n_col_tiles):
                cols = slice(jj * tn, (jj + 1) * tn)
                u_ref[:, cols] = (h_ref[:, cols] * nscale * next_ref[:, cols]).astype(u_ref.dtype)


def _rows(x, w, res, post_w, next_w, res_scale, tm=512, tn=512):
    m, k = x.shape
    d = w.shape[-1]
    nj = d // tn
    emit_next = next_w is not None
    row_spec = pl.BlockSpec((tm, d), lambda i, j: (i, 0))
    vec_spec = pl.BlockSpec((1, d), lambda i, j: (0, 0))
    in_specs = [pl.BlockSpec((tm, k), lambda i, j: (i, 0)),
                pl.BlockSpec((k, tn), lambda i, j: (0, j)),
                row_spec, vec_spec]
    args = [x, w, res, post_w]
    out_shape = [jax.ShapeDtypeStruct((m, d), F32)]
    out_specs = [row_spec]
    if emit_next:
        in_specs.append(vec_spec)
        args.append(next_w)
        out_shape.append(jax.ShapeDtypeStruct((m, d), BF16))
        out_specs.append(row_spec)
    outs = pl.pallas_call(
        functools.partial(_rows_kernel, n_col_tiles=nj, tn=tn, res_scale=res_scale,
                          emit_next=emit_next),
        out_shape=out_shape,
        grid=(m // tm, nj),
        in_specs=in_specs,
        out_specs=out_specs,
        scratch_shapes=[pltpu.VMEM((nj, tm, tn), F32), pltpu.VMEM((tm, 1), F32)],
        compiler_params=_cparams(2),
        name="rows_matmul_norm",
    )(*args)
    return outs if emit_next else (outs[0], None)


def _split_bf16(x):
    hi = x.astype(BF16)
    lo = (x - hi.astype(F32)).astype(BF16)
    return hi, lo


def _dot_nt(a, b):
    return lax.dot_general(a, b, (((1,), (1,)), ((), ())), preferred_element_type=F32)


def _dot_tn(a, b):
    return lax.dot_general(a, b, (((0,), (0,)), ((), ())), preferred_element_type=F32)


N_SUB = CHUNK // SUB


def _chunk_consts():
    i = lax.broadcasted_iota(jnp.int32, (CHUNK, CHUNK), 0)
    j = lax.broadcasted_iota(jnp.int32, (CHUNK, CHUNK), 1)
    lo = (i // SUB) * SUB
    hi = lo + SUB
    groups = [(j >= lo) & (j <= i), (j > i) & (j < hi), j < lo, j >= hi]
    sel = jnp.concatenate([jnp.where(g, 1.0, 0.0) for g in groups], axis=0).astype(BF16)
    return jnp.concatenate([sel, sel], axis=1), i >= j


def _decay_sums(sel2, g):
    g_hi, g_lo = _split_bf16(g * LOG2_E)
    return jnp.dot(sel2, jnp.concatenate([g_hi, g_lo], axis=0), preferred_element_type=F32)


def _gated_chunks(heads, causal):
    prep = [_chunk_operands(*h) for h in heads]
    scores = [_dot_nt(p[0], p[1]) for p in prep]
    kv = [jnp.dot(p[4], p[3], preferred_element_type=F32) for p in prep]
    outs, states = [], []
    for (q_t, k_t, q_st, k_st, v_t, st_bf, st_decay), s, upd, h in zip(prep, scores, kv, heads):
        s = jnp.where(causal, s, 0.0).astype(BF16)
        outs.append(_dot_nt(jnp.concatenate([q_st, s], axis=1),
                            jnp.concatenate([st_bf, v_t], axis=1)))
        states.append(h[4] * st_decay + upd)
    return outs, states


def _chunk_operands(q, k, v, dec, st):
    within, rest, before, after = dec
    dk = q.shape[1]
    cum = within + before
    q_st = (q * jnp.exp2(cum)).astype(BF16)
    k_st = (k * jnp.exp2(rest + after)).astype(BF16)
    q_in = (q * jnp.exp2(within)).astype(BF16)
    k_diag = (k * jnp.exp2(jnp.minimum(-within, EXP2_CLAMP))).astype(BF16)
    k_end = k * jnp.exp2(rest)
    k_end_bf = k_end.astype(BF16)

    blk = lambda x, b: x[b * SUB:(b + 1) * SUB]
    whole = {b: jnp.exp2(within[(b + 1) * SUB - 1:(b + 1) * SUB]) for b in range(1, N_SUB - 1)}
    skip = {}
    for bq in range(N_SUB):
        for bk in range(bq - 1):
            d = whole[bk + 1]
            for mid in range(bk + 2, bq):
                d = d * whole[mid]
            skip[bq, bk] = d
    zero = jnp.zeros((SUB, dk), BF16)
    k_cols, q_cols = [], []
    for bq in range(N_SUB):
        col = []
        for bk in range(N_SUB):
            if bk > bq:
                col.append(zero)
            elif bk == bq:
                col.append(blk(k_diag, bk))
            elif bk == bq - 1:
                col.append(blk(k_end_bf, bk))
            else:
                col.append((blk(k_end, bk) * skip[bq, bk]).astype(BF16))
        k_cols.append(jnp.concatenate(col, axis=0))
        q_cols.append(jnp.concatenate([blk(q_in, b) if b == bq else zero for b in range(N_SUB)],
                                      axis=0))
    q_tilde = jnp.concatenate(q_cols, axis=1)
    k_tilde = jnp.concatenate(k_cols, axis=1)
    v_t = v.T
    return q_tilde, k_tilde, q_st, k_st, v_t, st.astype(BF16), jnp.exp2(cum[CHUNK - 1:CHUNK])


def _head_norm_gate(o, norm_w, gate):
    return (o * _rms_scale(o) * norm_w * _silu(gate.astype(F32))).astype(BF16)


def _gla_kernel(q_ref, k_ref, v_ref, go_ref, code_ref, wgk_ref, bgk_ref, gn_ref, o_ref, st_ref,
                sums_ref, *, n_chunks, dk, dv):
    @pl.when(pl.program_id(1) == 0)
    def _():
        st_ref[...] = jnp.zeros_like(st_ref)

    sel2, causal = _chunk_consts()
    norm_w = gn_ref[...]
    q_scale = dk ** -0.5

    w_hi, w_lo = _split_bf16(wgk_ref[...])
    c_hi, c_lo = _split_bf16(code_ref[...])
    z = (jnp.dot(c_hi, w_hi, preferred_element_type=F32)
         + jnp.dot(c_lo, w_hi, preferred_element_type=F32)
         + jnp.dot(c_hi, w_lo, preferred_element_type=F32)) + bgk_ref[...]
    log_a = (jnp.minimum(z, 0.0) - jnp.log1p(jnp.exp(-jnp.abs(z)))) * (1.0 / GLA_GATE_NORMALIZER)
    for c in range(n_chunks):
        sums_ref[c] = _decay_sums(sel2, log_a[c * CHUNK:(c + 1) * CHUNK])

    def chunk_body(c, carry):
        rows = pl.ds(pl.multiple_of(c * CHUNK, CHUNK), CHUNK)
        heads = []
        for h in range(GLA_HEADS):
            kc = slice(h * dk, (h + 1) * dk)
            q = q_ref[rows, kc].astype(F32) * q_scale
            k = k_ref[rows, kc].astype(F32)
            dec = [sums_ref[c, n * CHUNK:(n + 1) * CHUNK, kc] for n in range(4)]
            heads.append((q, k, v_ref[rows, h * dv:(h + 1) * dv], dec, st_ref[h]))
        outs, states = _gated_chunks(heads, causal)
        for h in range(GLA_HEADS):
            vc = slice(h * dv, (h + 1) * dv)
            st_ref[h] = states[h]
            o_ref[rows, vc] = _head_norm_gate(outs[h], norm_w, go_ref[rows, vc])
        return carry

    lax.fori_loop(0, n_chunks, chunk_body, 0)


def _gla(p_a, p_code, wgk_pad, b_gk, gla_norm, batch, seq, t_blk=512):
    kw = wgk_pad.shape[1]
    vw = (p_a.shape[1] - 2 * kw) // 2
    dk = kw // GLA_HEADS
    dv = vw // GLA_HEADS
    nt = seq // t_blk
    row = lambda b, t: b * nt + t
    return pl.pallas_call(
        functools.partial(_gla_kernel, n_chunks=t_blk // CHUNK, dk=dk, dv=dv),
        out_shape=jax.ShapeDtypeStruct((batch * seq, vw), BF16),
        grid=(batch, nt),
        in_specs=[pl.BlockSpec((t_blk, kw), lambda b, t: (row(b, t), 0)),
                  pl.BlockSpec((t_blk, kw), lambda b, t: (row(b, t), 1)),
                  pl.BlockSpec((t_blk, vw), lambda b, t: (row(b, t), 1)),
                  pl.BlockSpec((t_blk, vw), lambda b, t: (row(b, t), 2)),
                  pl.BlockSpec((t_blk, LANES), lambda b, t: (row(b, t), 0)),
                  pl.BlockSpec((LANES, kw), lambda b, t: (0, 0)),
                  pl.BlockSpec((1, kw), lambda b, t: (0, 0)),
                  pl.BlockSpec((1, dv), lambda b, t: (0, 0))],
        out_specs=pl.BlockSpec((t_blk, vw), lambda b, t: (row(b, t), 0)),
        scratch_shapes=[pltpu.VMEM((GLA_HEADS, dv, dk), F32),
                        pltpu.VMEM((t_blk // CHUNK, 4 * CHUNK, kw), F32)],
        compiler_params=_cparams(2),
        name="gla_mixer",
    )(p_a, p_a, p_a, p_a, p_code, wgk_pad, b_gk, gla_norm)


def _hgrn_kernel(hq_ref, hi_ref, ho_ref, hf_ref, lbl_ref, hn_ref, o_ref, st_ref,
                 *, n_chunks, n_heads, dk, layer):
    @pl.when(pl.program_id(1) == 0)
    def _():
        st_ref[...] = jnp.zeros_like(st_ref)

    sel2, causal = _chunk_consts()
    logits = lbl_ref[...]
    p = jnp.exp(logits - jnp.max(logits, axis=0, keepdims=True))
    p = p / jnp.sum(p, axis=0, keepdims=True)
    lb = jnp.sum(p[:layer + 1], axis=0, keepdims=True)
    one_m_lb = jnp.sum(p[layer + 1:], axis=0, keepdims=True)
    norm_w = hn_ref[...]

    def chunk_body(c, carry):
        rows = pl.ds(pl.multiple_of(c * CHUNK, CHUNK), CHUNK)
        hf = hf_ref[rows, :]
        e = jnp.exp(-jnp.abs(hf))
        inv = 1.0 / (1.0 + e)
        pos = hf >= 0.0
        sig = jnp.where(pos, inv, e * inv)
        sig_neg = jnp.where(pos, e * inv, inv)
        log_f = jnp.log(lb + one_m_lb * sig)
        k_all = one_m_lb * sig_neg
        sums = _decay_sums(sel2, log_f)
        heads = []
        for h in range(n_heads):
            hc = slice(h * dk, (h + 1) * dk)
            q = _silu(hq_ref[rows, hc].astype(F32))
            dec = [sums[n * CHUNK:(n + 1) * CHUNK, hc] for n in range(4)]
            heads.append((q, k_all[:, hc], hi_ref[rows, hc], dec, st_ref[h]))
        outs, states = _gated_chunks(heads, causal)
        for h in range(n_heads):
            hc = slice(h * dk, (h + 1) * dk)
            st_ref[h] = states[h]
            o_ref[rows, hc] = _head_norm_gate(outs[h], norm_w, ho_ref[rows, hc])
        return carry

    lax.fori_loop(0, n_chunks, chunk_body, 0)


def _hgrn(p_q, p_r, p_f, lb_logits, hgrn_norm, layer, batch, seq, t_blk=512):
    w = p_f.shape[1]
    dk = HGRN_EXPAND
    n_heads = w // dk
    nt = seq // t_blk
    n_lb = lb_logits.shape[0]
    row = lambda b, t: b * nt + t
    return pl.pallas_call(
        functools.partial(_hgrn_kernel, n_chunks=t_blk // CHUNK, n_heads=n_heads, dk=dk,
                          layer=layer),
        out_shape=jax.ShapeDtypeStruct((batch * seq, w), BF16),
        grid=(batch, nt),
        in_specs=[pl.BlockSpec((t_blk, w), lambda b, t: (row(b, t), 0)),
                  pl.BlockSpec((t_blk, w), lambda b, t: (row(b, t), 0)),
                  pl.BlockSpec((t_blk, w), lambda b, t: (row(b, t), 1)),
                  pl.BlockSpec((t_blk, w), lambda b, t: (row(b, t), 0)),
                  pl.BlockSpec((n_lb, w), lambda b, t: (0, 0)),
                  pl.BlockSpec((1, dk), lambda b, t: (0, 0))],
        out_specs=pl.BlockSpec((t_blk, w), lambda b, t: (row(b, t), 0)),
        scratch_shapes=[pltpu.VMEM((n_heads, dk, dk), F32)],
        compiler_params=_cparams(2),
        name="hgrn_mixer",
    )(p_q, p_r, p_r, p_f, lb_logits, hgrn_norm)


def _merge_kernel(og_ref, oh_ref, zg_ref, zh_ref, wg_ref, wh_ref, bg_ref, bh_ref, o_ref,
                  wg_bf, wh_bf):
    @pl.when(pl.program_id(1) == 0)
    def _():
        wg_bf[...] = wg_ref[...].astype(BF16)
        wh_bf[...] = wh_ref[...].astype(BF16)

    a = jnp.dot(og_ref[...], wg_bf[...], preferred_element_type=F32)
    b = jnp.dot(oh_ref[...], wh_bf[...], preferred_element_type=F32)
    o_ref[...] = (_sigmoid(zg_ref[...].astype(F32) + bg_ref[...]) * a
                  + _sigmoid(zh_ref[...].astype(F32) + bh_ref[...]) * b).astype(o_ref.dtype)


def _merge(o_gla, o_hgrn, p_b, zg_col0, zh_col0, w_bg, w_bh, b_gates, tm=1024, tn=1024):
    m, kdim = o_gla.shape
    d = w_bg.shape[-1]
    x_spec = pl.BlockSpec((tm, kdim), lambda j, i: (i, 0))
    w_spec = pl.BlockSpec((None, kdim, tn), lambda j, i: (0, 0, j))
    return pl.pallas_call(
        _merge_kernel,
        out_shape=jax.ShapeDtypeStruct((m, d), BF16),
        grid=(d // tn, m // tm),
        in_specs=[x_spec, x_spec,
                  pl.BlockSpec((tm, tn), lambda j, i: (i, zg_col0 // tn + j)),
                  pl.BlockSpec((tm, tn), lambda j, i: (i, zh_col0 // tn + j)),
                  w_spec, w_spec,
                  pl.BlockSpec((None, 1, tn), lambda j, i: (0, 0, j)),
                  pl.BlockSpec((None, 1, tn), lambda j, i: (1, 0, j))],
        out_specs=pl.BlockSpec((tm, tn), lambda j, i: (i, j)),
        scratch_shapes=[pltpu.VMEM((kdim, tn), BF16), pltpu.VMEM((kdim, tn), BF16)],
        compiler_params=_cparams(2),
        name="branch_merge",
    )(o_gla, o_hgrn, p_b, p_b, w_bg, w_bh, b_gates, b_gates)


def kernel(x, ffn1_pre_norm, ffn1_w_gate, ffn1_w_up, ffn1_w_down, ffn1_post_norm, mix_pre_norm, w_in, gla_w_gk_up, gla_b_gk, gla_norm, hgrn_lb_logits, hgrn_norm, w_branch_gla, w_branch_hgrn, b_branch_gates, w_out, mix_post_norm, ffn2_pre_norm, ffn2_w_gate, ffn2_w_up, ffn2_w_down, ffn2_post_norm):
    batch, seq, d_model = x.shape
    depth = ffn1_w_gate.shape[0]
    m = batch * seq
    kw = gla_w_gk_up.shape[-1]
    vw = d_model // 2
    a_cols = 2 * kw + 2 * vw
    code0 = a_cols
    hq0 = code0 + GLA_GATE_RANK
    hf0, hi0 = hq0 + vw, hq0 + 2 * vw

    h = x.reshape(m, d_model)
    u = _rmsnorm(h, ffn1_pre_norm[0:1])
    for l in range(depth):
        mid = _gateup(u, ffn1_w_gate[l:l + 1], ffn1_w_up[l:l + 1])
        h, u = _rows(mid, ffn1_w_down[l].astype(BF16), h, ffn1_post_norm[l:l + 1],
                     mix_pre_norm[l:l + 1], 0.5)

        w_l = jnp.swapaxes(w_in[l:l + 1], 1, 2)
        p_a = _proj(u, w_l, 0, a_cols, BF16)
        p_code = _proj(u, w_l, code0, LANES, F32, tn=LANES)
        p_q = _proj(u, w_l, hq0, vw, BF16)
        p_f = _proj(u, w_l, hf0, vw, F32)
        p_r = _proj(u, w_l, hi0, 2 * vw + 2 * d_model, BF16)
        wgk_pad = jnp.pad(gla_w_gk_up[l], ((0, LANES - GLA_GATE_RANK), (0, 0)))
        o_gla = _gla(p_a, p_code, wgk_pad, gla_b_gk[l:l + 1], gla_norm[l:l + 1], batch, seq)
        o_hgrn = _hgrn(p_q, p_r, p_f, hgrn_lb_logits, hgrn_norm[l:l + 1], l, batch, seq)
        merged = _merge(o_gla, o_hgrn, p_r, 2 * vw, 2 * vw + d_model,
                        w_branch_gla[l:l + 1], w_branch_hgrn[l:l + 1],
                        b_branch_gates[l].reshape(2, 1, d_model))
        h, u = _rows(merged, w_out[l].astype(BF16), h, mix_post_norm[l:l + 1],
                     ffn2_pre_norm[l:l + 1], 1.0, tn=d_model)

        mid = _gateup(u, ffn2_w_gate[l:l + 1], ffn2_w_up[l:l + 1])
        next_norm = ffn1_pre_norm[l + 1:l + 2] if l + 1 < depth else None
        h, u = _rows(mid, ffn2_w_down[l].astype(BF16), h, ffn2_post_norm[l:l + 1], next_norm, 0.5)
    return h.reshape(batch, seq, d_model)
```

```python
import functools

import jax
import jax.numpy as jnp
from jax import lax
from jax.experimental import pallas as pl
from jax.experimental.pallas import tpu as pltpu

F32 = jnp.float32
BF16 = jnp.bfloat16

EPS = 1e-6
CHUNK = 64
SUB = 16
GLA_HEADS = 4
GLA_GATE_RANK = 16
GLA_GATE_NORMALIZER = 16.0
HGRN_EXPAND = 128
LOG2_E = 1.4426950408889634
EXP2_CLAMP = 115.0

LANES = 128
SUBLANES = 8
VMEM_LIMIT = 56 * 1024 * 1024


def _cparams(n_axes):
    return pltpu.CompilerParams(
        dimension_semantics=("arbitrary",) * n_axes, vmem_limit_bytes=VMEM_LIMIT)


def _sigmoid(x):
    return 1.0 / (1.0 + jnp.exp(-x))


def _silu(x):
    return x * _sigmoid(x)


def _rms_scale(x):
    return lax.rsqrt(jnp.mean(x * x, axis=-1, keepdims=True) + EPS)


def _rmsnorm_kernel(x_ref, w_ref, o_ref):
    x = x_ref[...]
    o_ref[...] = (x * _rms_scale(x) * w_ref[...]).astype(o_ref.dtype)


def _rmsnorm(x, w, tm=512):
    m, d = x.shape
    return pl.pallas_call(
        _rmsnorm_kernel,
        out_shape=jax.ShapeDtypeStruct((m, d), BF16),
        grid=(m // tm,),
        in_specs=[pl.BlockSpec((tm, d), lambda i: (i, 0)),
                  pl.BlockSpec((1, d), lambda i: (0, 0))],
        out_specs=pl.BlockSpec((tm, d), lambda i: (i, 0)),
        compiler_params=_cparams(1),
        name="rmsnorm",
    )(x, w)


def _gateup_kernel(u_ref, wg_ref, wu_ref, o_ref, wg_bf, wu_bf):
    @pl.when(pl.program_id(1) == 0)
    def _():
        wg_bf[...] = wg_ref[...].astype(BF16)
        wu_bf[...] = wu_ref[...].astype(BF16)

    u = u_ref[...]
    g = jnp.dot(u, wg_bf[...], preferred_element_type=F32)
    up = jnp.dot(u, wu_bf[...], preferred_element_type=F32)
    o_ref[...] = (_silu(g) * up).astype(o_ref.dtype)


def _gateup(u, w_gate, w_up, tm=1024, tn=512):
    m, d = u.shape
    f = w_gate.shape[-1]
    w_spec = pl.BlockSpec((None, d, tn), lambda j, i: (0, 0, j))
    return pl.pallas_call(
        _gateup_kernel,
        out_shape=jax.ShapeDtypeStruct((m, f), BF16),
        grid=(pl.cdiv(f, tn), m // tm),
        in_specs=[pl.BlockSpec((tm, d), lambda j, i: (i, 0)), w_spec, w_spec],
        out_specs=pl.BlockSpec((tm, tn), lambda j, i: (i, j)),
        scratch_shapes=[pltpu.VMEM((d, tn), BF16), pltpu.VMEM((d, tn), BF16)],
        compiler_params=_cparams(2),
        name="ffn_gateup",
    )(u, w_gate, w_up)


def _proj_kernel(u_ref, w_ref, *rest, shift, tn):
    if shift:
        wn_ref, o_ref, w_bf = rest
    else:
        o_ref, w_bf = rest

    @pl.when(pl.program_id(1) == 0)
    def _():
        w = w_ref[...]
        if shift:
            w = jnp.concatenate([w, wn_ref[...]], axis=0)[shift:shift + tn]
        w_bf[...] = w.astype(BF16)

    o_ref[...] = _dot_nt(u_ref[...], w_bf[...]).astype(o_ref.dtype)


def _proj(u, wt, col0, n_cols, out_dtype, tn=1024):
    m, d = u.shape
    tm = 2048 if n_cols >= 3 * tn else 1024
    shift = col0 % tn
    blk0 = col0 // tn
    assert n_cols % tn == 0 and shift % SUBLANES == 0
    in_specs = [pl.BlockSpec((tm, d), lambda j, i: (i, 0)),
                pl.BlockSpec((None, tn, d), lambda j, i: (0, j + blk0, 0))]
    args = [u, wt]
    if shift:
        assert tn % shift == 0
        per_tile = tn // shift
        in_specs.append(pl.BlockSpec((None, shift, d), lambda j, i: (0, (j + blk0 + 1) * per_tile, 0)))
        args.append(wt)
    return pl.pallas_call(
        functools.partial(_proj_kernel, shift=shift, tn=tn),
        out_shape=jax.ShapeDtypeStruct((m, n_cols), out_dtype),
        grid=(n_cols // tn, m // tm),
        in_specs=in_specs,
        out_specs=pl.BlockSpec((tm, tn), lambda j, i: (i, j)),
        scratch_shapes=[pltpu.VMEM((tn, d), BF16)],
        compiler_params=_cparams(2),
        name="in_proj",
    )(*args)


def _rows_kernel(x_ref, w_ref, res_ref, post_ref, *rest, n_col_tiles, tn, res_scale, emit_next):
    if emit_next:
        next_ref, h_ref, u_ref, acc_ref = rest
    else:
        h_ref, acc_ref = rest
    j = pl.program_id(1)
    sub = h_ref.shape[0]

    @pl.when(j < n_col_tiles)
    def _():
        acc_ref[j] = jnp.dot(x_ref[...], w_ref[...], preferred_element_type=F32)

    @pl.when(j >= n_col_tiles)
    def _():
        rows = pl.ds(pl.multiple_of((j - n_col_tiles) * sub, sub), sub)
        d = n_col_tiles * tn
        ssq = None
        for jj in range(n_col_tiles):
            a = acc_ref[jj, rows, :]
            s = jnp.sum(a * a, axis=-1, keepdims=True)
            ssq = s if ssq is None else ssq + s
        scale = lax.rsqrt(ssq / d + EPS) * res_scale
        hsq = None
        for jj in range(n_col_tiles):
            cols = slice(jj * tn, (jj + 1) * tn)
            h = res_ref[:, cols] + acc_ref[jj, rows, :] * scale * post_ref[:, cols]
            h_ref[:, cols] = h
            if emit_next:
                s = jnp.sum(h * h, axis=-1, keepdims=True)
                hsq = s if hsq is None else hsq + s
        if emit_next:
            nscale = lax.rsqrt(hsq / d + EPS)
            for jj in range(n_col_tiles):
                cols = slice(jj * tn, (jj + 1) * tn)
                u_ref[:, cols] = (h_ref[:, cols] * nscale * next_ref[:, cols]).astype(u_ref.dtype)


def _rows(x, w, res, post_w, next_w, res_scale, tm=1024, tn=512, sub=256):
    m, k = x.shape
    d = w.shape[-1]
    nj = d // tn
    n_sub = tm // sub
    emit_next = next_w is not None
    row_spec = pl.BlockSpec((sub, d), lambda i, j: (i * n_sub + jnp.maximum(j - nj, 0), 0))
    vec_spec = pl.BlockSpec((1, d), lambda i, j: (0, 0))
    in_specs = [pl.BlockSpec((tm, k), lambda i, j: (i, 0)),
                pl.BlockSpec((k, tn), lambda i, j: (0, jnp.minimum(j, nj - 1))),
                row_spec, vec_spec]
    args = [x, w, res, post_w]
    out_shape = [jax.ShapeDtypeStruct((m, d), F32)]
    out_specs = [row_spec]
    if emit_next:
        in_specs.append(vec_spec)
        args.append(next_w)
        out_shape.append(jax.ShapeDtypeStruct((m, d), BF16))
        out_specs.append(row_spec)
    outs = pl.pallas_call(
        functools.partial(_rows_kernel, n_col_tiles=nj, tn=tn, res_scale=res_scale,
                          emit_next=emit_next),
        out_shape=out_shape,
        grid=(m // tm, nj + n_sub),
        in_specs=in_specs,
        out_specs=out_specs,
        scratch_shapes=[pltpu.VMEM((nj, tm, tn), F32)],
        compiler_params=_cparams(2),
        name="rows_matmul_norm",
    )(*args)
    return outs if emit_next else (outs[0], None)


def _split_bf16(x):
    hi = x.astype(BF16)
    lo = (x - hi.astype(F32)).astype(BF16)
    return hi, lo


def _dot_nt(a, b):
    return lax.dot_general(a, b, (((1,), (1,)), ((), ())), preferred_element_type=F32)


def _dot_tn(a, b):
    return lax.dot_general(a, b, (((0,), (0,)), ((), ())), preferred_element_type=F32)


N_SUB = CHUNK // SUB


def _chunk_consts():
    i = lax.broadcasted_iota(jnp.int32, (CHUNK, CHUNK), 0)
    j = lax.broadcasted_iota(jnp.int32, (CHUNK, CHUNK), 1)
    lo = (i // SUB) * SUB
    hi = lo + SUB
    groups = [(j >= lo) & (j <= i), (j > i) & (j < hi), j < lo, j >= hi]
    sel = jnp.concatenate([jnp.where(g, 1.0, 0.0) for g in groups], axis=0).astype(BF16)
    return jnp.concatenate([sel, sel], axis=1), i >= j


def _decay_sums(sel2, g):
    g_hi, g_lo = _split_bf16(g * LOG2_E)
    return jnp.dot(sel2, jnp.concatenate([g_hi, g_lo], axis=0), preferred_element_type=F32)


def _gated_chunks(heads, causal):
    prep = [_chunk_operands(*h) for h in heads]
    scores = [_dot_nt(p[0], p[1]) for p in prep]
    kv = [jnp.dot(p[4], p[3], preferred_element_type=F32) for p in prep]
    outs, states = [], []
    for (q_t, k_t, q_st, k_st, v_t, st_bf, st_decay), s, upd, h in zip(prep, scores, kv, heads):
        s = jnp.where(causal, s, 0.0).astype(BF16)
        outs.append(_dot_nt(jnp.concatenate([q_st, s], axis=1),
                            jnp.concatenate([st_bf, v_t], axis=1)))
        states.append(h[4] * st_decay + upd)
    return outs, states


def _chunk_operands(q, k, v, dec, st):
    within, rest, before, after = dec
    dk = q.shape[1]
    cum = within + before
    q_st = (q * jnp.exp2(cum)).astype(BF16)
    k_st = (k * jnp.exp2(rest + after)).astype(BF16)
    q_in = (q * jnp.exp2(within)).astype(BF16)
    k_diag = (k * jnp.exp2(jnp.minimum(-within, EXP2_CLAMP))).astype(BF16)
    k_end = k * jnp.exp2(rest)
    k_end_bf = k_end.astype(BF16)

    blk = lambda x, b: x[b * SUB:(b + 1) * SUB]
    whole = {b: jnp.exp2(within[(b + 1) * SUB - 1:(b + 1) * SUB]) for b in range(1, N_SUB - 1)}
    skip = {}
    for bq in range(N_SUB):
        for bk in range(bq - 1):
            d = whole[bk + 1]
            for mid in range(bk + 2, bq):
                d = d * whole[mid]
            skip[bq, bk] = d
    zero = jnp.zeros((SUB, dk), BF16)
    k_cols, q_cols = [], []
    for bq in range(N_SUB):
        col = []
        for bk in range(N_SUB):
            if bk > bq:
                col.append(zero)
            elif bk == bq:
                col.append(blk(k_diag, bk))
            elif bk == bq - 1:
                col.append(blk(k_end_bf, bk))
            else:
                col.append((blk(k_end, bk) * skip[bq, bk]).astype(BF16))
        k_cols.append(jnp.concatenate(col, axis=0))
        q_cols.append(jnp.concatenate([blk(q_in, b) if b == bq else zero for b in range(N_SUB)],
                                      axis=0))
    q_tilde = jnp.concatenate(q_cols, axis=1)
    k_tilde = jnp.concatenate(k_cols, axis=1)
    v_t = v.T
    return q_tilde, k_tilde, q_st, k_st, v_t, st.astype(BF16), jnp.exp2(cum[CHUNK - 1:CHUNK])


def _head_norm_gate(o, norm_w, gate):
    return (o * _rms_scale(o) * norm_w * _silu(gate.astype(F32))).astype(BF16)


def _gla_kernel(q_ref, k_ref, v_ref, go_ref, code_ref, wgk_ref, bgk_ref, gn_ref, o_ref, st_ref,
                sums_ref, *, n_chunks, dk, dv):
    @pl.when(pl.program_id(1) == 0)
    def _():
        st_ref[...] = jnp.zeros_like(st_ref)

    sel2, causal = _chunk_consts()
    norm_w = gn_ref[...]
    q_scale = dk ** -0.5

    w_hi, w_lo = _split_bf16(wgk_ref[...])
    c_hi, c_lo = _split_bf16(code_ref[...])
    z = (jnp.dot(c_hi, w_hi, preferred_element_type=F32)
         + jnp.dot(c_lo, w_hi, preferred_element_type=F32)
         + jnp.dot(c_hi, w_lo, preferred_element_type=F32)) + bgk_ref[...]
    log_a = (jnp.minimum(z, 0.0) - jnp.log1p(jnp.exp(-jnp.abs(z)))) * (1.0 / GLA_GATE_NORMALIZER)
    for c in range(n_chunks):
        sums_ref[c] = _decay_sums(sel2, log_a[c * CHUNK:(c + 1) * CHUNK])

    def chunk_body(c, carry):
        rows = pl.ds(pl.multiple_of(c * CHUNK, CHUNK), CHUNK)
        heads = []
        for h in range(GLA_HEADS):
            kc = slice(h * dk, (h + 1) * dk)
            q = q_ref[rows, kc].astype(F32) * q_scale
            k = k_ref[rows, kc].astype(F32)
            dec = [sums_ref[c, n * CHUNK:(n + 1) * CHUNK, kc] for n in range(4)]
            heads.append((q, k, v_ref[rows, h * dv:(h + 1) * dv], dec, st_ref[h]))
        outs, states = _gated_chunks(heads, causal)
        for h in range(GLA_HEADS):
            vc = slice(h * dv, (h + 1) * dv)
            st_ref[h] = states[h]
            o_ref[rows, vc] = _head_norm_gate(outs[h], norm_w, go_ref[rows, vc])
        return carry

    lax.fori_loop(0, n_chunks, chunk_body, 0)


def _gla(p_a, p_code, wgk_pad, b_gk, gla_norm, batch, seq, t_blk=512):
    kw = wgk_pad.shape[1]
    vw = (p_a.shape[1] - 2 * kw) // 2
    dk = kw // GLA_HEADS
    dv = vw // GLA_HEADS
    nt = seq // t_blk
    row = lambda b, t: b * nt + t
    return pl.pallas_call(
        functools.partial(_gla_kernel, n_chunks=t_blk // CHUNK, dk=dk, dv=dv),
        out_shape=jax.ShapeDtypeStruct((batch * seq, vw), BF16),
        grid=(batch, nt),
        in_specs=[pl.BlockSpec((t_blk, kw), lambda b, t: (row(b, t), 0)),
                  pl.BlockSpec((t_blk, kw), lambda b, t: (row(b, t), 1)),
                  pl.BlockSpec((t_blk, vw), lambda b, t: (row(b, t), 1)),
                  pl.BlockSpec((t_blk, vw), lambda b, t: (row(b, t), 2)),
                  pl.BlockSpec((t_blk, LANES), lambda b, t: (row(b, t), 0)),
                  pl.BlockSpec((LANES, kw), lambda b, t: (0, 0)),
                  pl.BlockSpec((1, kw), lambda b, t: (0, 0)),
                  pl.BlockSpec((1, dv), lambda b, t: (0, 0))],
        out_specs=pl.BlockSpec((t_blk, vw), lambda b, t: (row(b, t), 0)),
        scratch_shapes=[pltpu.VMEM((GLA_HEADS, dv, dk), F32),
                        pltpu.VMEM((t_blk // CHUNK, 4 * CHUNK, kw), F32)],
        compiler_params=_cparams(2),
        name="gla_mixer",
    )(p_a, p_a, p_a, p_a, p_code, wgk_pad, b_gk, gla_norm)


def _hgrn_kernel(hq_ref, hi_ref, ho_ref, hf_ref, lbl_ref, hn_ref, o_ref, st_ref,
                 *, n_chunks, n_heads, dk, layer):
    @pl.when(pl.program_id(1) == 0)
    def _():
        st_ref[...] = jnp.zeros_like(st_ref)

    sel2, causal = _chunk_consts()
    logits = lbl_ref[...]
    p = jnp.exp(logits - jnp.max(logits, axis=0, keepdims=True))
    p = p / jnp.sum(p, axis=0, keepdims=True)
    lb = jnp.sum(p[:layer + 1], axis=0, keepdims=True)
    one_m_lb = jnp.sum(p[layer + 1:], axis=0, keepdims=True)
    norm_w = hn_ref[...]

    def chunk_body(c, carry):
        rows = pl.ds(pl.multiple_of(c * CHUNK, CHUNK), CHUNK)
        hf = hf_ref[rows, :]
        e = jnp.exp(-jnp.abs(hf))
        inv = 1.0 / (1.0 + e)
        pos = hf >= 0.0
        sig = jnp.where(pos, inv, e * inv)
        sig_neg = jnp.where(pos, e * inv, inv)
        log_f = jnp.log(lb + one_m_lb * sig)
        k_all = one_m_lb * sig_neg
        sums = _decay_sums(sel2, log_f)
        heads = []
        for h in range(n_heads):
            hc = slice(h * dk, (h + 1) * dk)
            q = _silu(hq_ref[rows, hc].astype(F32))
            dec = [sums[n * CHUNK:(n + 1) * CHUNK, hc] for n in range(4)]
            heads.append((q, k_all[:, hc], hi_ref[rows, hc], dec, st_ref[h]))
        outs, states = _gated_chunks(heads, causal)
        for h in range(n_heads):
            hc = slice(h * dk, (h + 1) * dk)
            st_ref[h] = states[h]
            o_ref[rows, hc] = _head_norm_gate(outs[h], norm_w, ho_ref[rows, hc])
        return carry

    lax.fori_loop(0, n_chunks, chunk_body, 0)


def _hgrn(p_q, p_r, p_f, lb_logits, hgrn_norm, layer, batch, seq, t_blk=512):
    w = p_f.shape[1]
    dk = HGRN_EXPAND
    n_heads = w // dk
    nt = seq // t_blk
    n_lb = lb_logits.shape[0]
    row = lambda b, t: b * nt + t
    return pl.pallas_call(
        functools.partial(_hgrn_kernel, n_chunks=t_blk // CHUNK, n_heads=n_heads, dk=dk,
                          layer=layer),
        out_shape=jax.ShapeDtypeStruct((batch * seq, w), BF16),
        grid=(batch, nt),
        in_specs=[pl.BlockSpec((t_blk, w), lambda b, t: (row(b, t), 0)),
                  pl.BlockSpec((t_blk, w), lambda b, t: (row(b, t), 0)),
                  pl.BlockSpec((t_blk, w), lambda b, t: (row(b, t), 1)),
                  pl.BlockSpec((t_blk, w), lambda b, t: (row(b, t), 0)),
                  pl.BlockSpec((n_lb, w), lambda b, t: (0, 0)),
                  pl.BlockSpec((1, dk), lambda b, t: (0, 0))],
        out_specs=pl.BlockSpec((t_blk, w), lambda b, t: (row(b, t), 0)),
        scratch_shapes=[pltpu.VMEM((n_heads, dk, dk), F32)],
        compiler_params=_cparams(2),
        name="hgrn_mixer",
    )(p_q, p_r, p_r, p_f, lb_logits, hgrn_norm)


def _merge_kernel(og_ref, oh_ref, zg_ref, zh_ref, wg_ref, wh_ref, bg_ref, bh_ref, o_ref,
                  wg_bf, wh_bf):
    @pl.when(pl.program_id(1) == 0)
    def _():
        wg_bf[...] = wg_ref[...].astype(BF16)
        wh_bf[...] = wh_ref[...].astype(BF16)

    a = jnp.dot(og_ref[...], wg_bf[...], preferred_element_type=F32)
    b = jnp.dot(oh_ref[...], wh_bf[...], preferred_element_type=F32)
    o_ref[...] = (_sigmoid(zg_ref[...].astype(F32) + bg_ref[...]) * a
                  + _sigmoid(zh_ref[...].astype(F32) + bh_ref[...]) * b).astype(o_ref.dtype)


def _merge(o_gla, o_hgrn, p_b, zg_col0, zh_col0, w_bg, w_bh, b_gates, tm=1024, tn=1024):
    m, kdim = o_gla.shape
    d = w_bg.shape[-1]
    x_spec = pl.BlockSpec((tm, kdim), lambda j, i: (i, 0))
    w_spec = pl.BlockSpec((None, kdim, tn), lambda j, i: (0, 0, j))
    return pl.pallas_call(
        _merge_kernel,
        out_shape=jax.ShapeDtypeStruct((m, d), BF16),
        grid=(d // tn, m // tm),
        in_specs=[x_spec, x_spec,
                  pl.BlockSpec((tm, tn), lambda j, i: (i, zg_col0 // tn + j)),
                  pl.BlockSpec((tm, tn), lambda j, i: (i, zh_col0 // tn + j)),
                  w_spec, w_spec,
                  pl.BlockSpec((None, 1, tn), lambda j, i: (0, 0, j)),
                  pl.BlockSpec((None, 1, tn), lambda j, i: (1, 0, j))],
        out_specs=pl.BlockSpec((tm, tn), lambda j, i: (i, j)),
        scratch_shapes=[pltpu.VMEM((kdim, tn), BF16), pltpu.VMEM((kdim, tn), BF16)],
        compiler_params=_cparams(2),
        name="branch_merge",
    )(o_gla, o_hgrn, p_b, p_b, w_bg, w_bh, b_gates, b_gates)


def kernel(x, ffn1_pre_norm, ffn1_w_gate, ffn1_w_up, ffn1_w_down, ffn1_post_norm, mix_pre_norm, w_in, gla_w_gk_up, gla_b_gk, gla_norm, hgrn_lb_logits, hgrn_norm, w_branch_gla, w_branch_hgrn, b_branch_gates, w_out, mix_post_norm, ffn2_pre_norm, ffn2_w_gate, ffn2_w_up, ffn2_w_down, ffn2_post_norm):
    batch, seq, d_model = x.shape
    depth = ffn1_w_gate.shape[0]
    m = batch * seq
    kw = gla_w_gk_up.shape[-1]
    vw = d_model // 2
    a_cols = 2 * kw + 2 * vw
    code0 = a_cols
    hq0 = code0 + GLA_GATE_RANK
    hf0, hi0 = hq0 + vw, hq0 + 2 * vw

    h = x.reshape(m, d_model)
    u = _rmsnorm(h, ffn1_pre_norm[0:1])
    for l in range(depth):
        mid = _gateup(u, ffn1_w_gate[l:l + 1], ffn1_w_up[l:l + 1])
        h, u = _rows(mid, ffn1_w_down[l].astype(BF16), h, ffn1_post_norm[l:l + 1],
                     mix_pre_norm[l:l + 1], 0.5)

        w_l = jnp.swapaxes(w_in[l:l + 1], 1, 2)
        p_a = _proj(u, w_l, 0, a_cols, BF16)
        p_code = _proj(u, w_l, code0, LANES, F32, tn=LANES)
        p_q = _proj(u, w_l, hq0, vw, BF16)
        p_f = _proj(u, w_l, hf0, vw, F32)
        p_r = _proj(u, w_l, hi0, 2 * vw + 2 * d_model, BF16)
        wgk_pad = jnp.pad(gla_w_gk_up[l], ((0, LANES - GLA_GATE_RANK), (0, 0)))
        o_gla = _gla(p_a, p_code, wgk_pad, gla_b_gk[l:l + 1], gla_norm[l:l + 1], batch, seq)
        o_hgrn = _hgrn(p_q, p_r, p_f, hgrn_lb_logits, hgrn_norm[l:l + 1], l, batch, seq)
        merged = _merge(o_gla, o_hgrn, p_r, 2 * vw, 2 * vw + d_model,
                        w_branch_gla[l:l + 1], w_branch_hgrn[l:l + 1],
                        b_branch_gates[l].reshape(2, 1, d_model))
        h, u = _rows(merged, w_out[l].astype(BF16), h, mix_post_norm[l:l + 1],
                     ffn2_pre_norm[l:l + 1], 1.0, tn=d_model)

        mid = _gateup(u, ffn2_w_gate[l:l + 1], ffn2_w_up[l:l + 1])
        next_norm = ffn1_pre_norm[l + 1:l + 2] if l + 1 < depth else None
        h, u = _rows(mid, ffn2_w_down[l].astype(BF16), h, ffn2_post_norm[l:l + 1], next_norm, 0.5)
    return h.reshape(batch, seq, d_model)
```

```python
import functools

import jax
import jax.numpy as jnp
from jax import lax
from jax.experimental import pallas as pl
from jax.experimental.pallas import tpu as pltpu

F32 = jnp.float32
BF16 = jnp.bfloat16

EPS = 1e-6
CHUNK = 64
SUB = 16
GLA_HEADS = 4
GLA_GATE_RANK = 16
GLA_GATE_NORMALIZER = 16.0
HGRN_EXPAND = 128
LOG2_E = 1.4426950408889634
EXP2_CLAMP = 115.0

LANES = 128
SUBLANES = 8
VMEM_LIMIT = 56 * 1024 * 1024


def _cparams(n_axes):
    return pltpu.CompilerParams(
        dimension_semantics=("arbitrary",) * n_axes, vmem_limit_bytes=VMEM_LIMIT)


def _sigmoid(x):
    return 1.0 / (1.0 + jnp.exp(-x))


def _silu(x):
    return x * _sigmoid(x)


def _rms_scale(x):
    return lax.rsqrt(jnp.mean(x * x, axis=-1, keepdims=True) + EPS)


def _rmsnorm_kernel(x_ref, w_ref, o_ref):
    x = x_ref[...]
    o_ref[...] = (x * _rms_scale(x) * w_ref[...]).astype(o_ref.dtype)


def _rmsnorm(x, w, tm=512):
    m, d = x.shape
    return pl.pallas_call(
        _rmsnorm_kernel,
        out_shape=jax.ShapeDtypeStruct((m, d), BF16),
        grid=(m // tm,),
        in_specs=[pl.BlockSpec((tm, d), lambda i: (i, 0)),
                  pl.BlockSpec((1, d), lambda i: (0, 0))],
        out_specs=pl.BlockSpec((tm, d), lambda i: (i, 0)),
        compiler_params=_cparams(1),
        name="rmsnorm",
    )(x, w)


def _gateup_kernel(u_ref, wg_ref, wu_ref, o_ref, wg_bf, wu_bf):
    @pl.when(pl.program_id(1) == 0)
    def _():
        wg_bf[...] = wg_ref[...].astype(BF16)
        wu_bf[...] = wu_ref[...].astype(BF16)

    u = u_ref[...]
    g = jnp.dot(u, wg_bf[...], preferred_element_type=F32)
    up = jnp.dot(u, wu_bf[...], preferred_element_type=F32)
    o_ref[...] = (_silu(g) * up).astype(o_ref.dtype)


def _gateup(u, w_gate, w_up, tm=1024, tn=512):
    m, d = u.shape
    f = w_gate.shape[-1]
    w_spec = pl.BlockSpec((None, d, tn), lambda j, i: (0, 0, j))
    return pl.pallas_call(
        _gateup_kernel,
        out_shape=jax.ShapeDtypeStruct((m, f), BF16),
        grid=(pl.cdiv(f, tn), m // tm),
        in_specs=[pl.BlockSpec((tm, d), lambda j, i: (i, 0)), w_spec, w_spec],
        out_specs=pl.BlockSpec((tm, tn), lambda j, i: (i, j)),
        scratch_shapes=[pltpu.VMEM((d, tn), BF16), pltpu.VMEM((d, tn), BF16)],
        compiler_params=_cparams(2),
        name="ffn_gateup",
    )(u, w_gate, w_up)


def _proj_kernel(u_ref, w_ref, *rest, shift, tn):
    if shift:
        wn_ref, o_ref, w_bf = rest
    else:
        o_ref, w_bf = rest

    @pl.when(pl.program_id(1) == 0)
    def _():
        w = w_ref[...]
        if shift:
            w = jnp.concatenate([w, wn_ref[...]], axis=0)[shift:shift + tn]
        w_bf[...] = w.astype(BF16)

    o_ref[...] = _dot_nt(u_ref[...], w_bf[...]).astype(o_ref.dtype)


def _proj(u, wt, col0, n_cols, out_dtype, tn=1024):
    m, d = u.shape
    tm = 2048 if n_cols >= 3 * tn else 1024
    shift = col0 % tn
    blk0 = col0 // tn
    assert n_cols % tn == 0 and shift % SUBLANES == 0
    in_specs = [pl.BlockSpec((tm, d), lambda j, i: (i, 0)),
                pl.BlockSpec((None, tn, d), lambda j, i: (0, j + blk0, 0))]
    args = [u, wt]
    if shift:
        assert tn % shift == 0
        per_tile = tn // shift
        in_specs.append(pl.BlockSpec((None, shift, d), lambda j, i: (0, (j + blk0 + 1) * per_tile, 0)))
        args.append(wt)
    return pl.pallas_call(
        functools.partial(_proj_kernel, shift=shift, tn=tn),
        out_shape=jax.ShapeDtypeStruct((m, n_cols), out_dtype),
        grid=(n_cols // tn, m // tm),
        in_specs=in_specs,
        out_specs=pl.BlockSpec((tm, tn), lambda j, i: (i, j)),
        scratch_shapes=[pltpu.VMEM((tn, d), BF16)],
        compiler_params=_cparams(2),
        name="in_proj",
    )(*args)


def _rows_kernel(x_ref, w_ref, res_ref, post_ref, *rest, tn, res_scale, emit_next):
    if emit_next:
        next_ref, h_ref, u_ref, acc_a, acc_b = rest
    else:
        h_ref, acc_a, acc_b = rest
    i = pl.program_id(0)
    d = h_ref.shape[1]
    col_tiles = [slice(c, c + tn) for c in range(0, d, tn)]

    @pl.when(i == 0)
    def _():
        acc_b[...] = jnp.zeros_like(acc_b)

    def step(acc_cur, acc_prev):
        x = x_ref[...]
        for cols in col_tiles:
            acc_cur[:, cols] = jnp.dot(x, w_ref[:, cols], preferred_element_type=F32)

        ssq = None
        for cols in col_tiles:
            a = acc_prev[:, cols]
            s = jnp.sum(a * a, axis=-1, keepdims=True)
            ssq = s if ssq is None else ssq + s
        scale = lax.rsqrt(ssq / d + EPS) * res_scale
        hsq = None
        for cols in col_tiles:
            h = res_ref[:, cols] + acc_prev[:, cols] * scale * post_ref[:, cols]
            h_ref[:, cols] = h
            if emit_next:
                s = jnp.sum(h * h, axis=-1, keepdims=True)
                hsq = s if hsq is None else hsq + s
        if emit_next:
            nscale = lax.rsqrt(hsq / d + EPS)
            for cols in col_tiles:
                u_ref[:, cols] = (h_ref[:, cols] * nscale * next_ref[:, cols]).astype(u_ref.dtype)

    @pl.when(i % 2 == 0)
    def _():
        step(acc_a, acc_b)

    @pl.when(i % 2 == 1)
    def _():
        step(acc_b, acc_a)


def _rows(x, w, res, post_w, next_w, res_scale, tm=256, tn=512):
    m, k = x.shape
    d = w.shape[-1]
    n_tiles = m // tm
    emit_next = next_w is not None
    lag_spec = pl.BlockSpec((tm, d), lambda i: (jnp.maximum(i - 1, 0), 0))
    vec_spec = pl.BlockSpec((1, d), lambda i: (0, 0))
    in_specs = [pl.BlockSpec((tm, k), lambda i: (jnp.minimum(i, n_tiles - 1), 0)),
                pl.BlockSpec((k, d), lambda i: (0, 0), pipeline_mode=pl.Buffered(1)),
                lag_spec, vec_spec]
    args = [x, w, res, post_w]
    out_shape = [jax.ShapeDtypeStruct((m, d), F32)]
    out_specs = [lag_spec]
    if emit_next:
        in_specs.append(vec_spec)
        args.append(next_w)
        out_shape.append(jax.ShapeDtypeStruct((m, d), BF16))
        out_specs.append(lag_spec)
    outs = pl.pallas_call(
        functools.partial(_rows_kernel, tn=tn, res_scale=res_scale, emit_next=emit_next),
        out_shape=out_shape,
        grid=(n_tiles + 1,),
        in_specs=in_specs,
        out_specs=out_specs,
        scratch_shapes=[pltpu.VMEM((tm, d), F32), pltpu.VMEM((tm, d), F32)],
        compiler_params=_cparams(1),
        name="rows_matmul_norm",
    )(*args)
    return outs if emit_next else (outs[0], None)


def _split_bf16(x):
    hi = x.astype(BF16)
    lo = (x - hi.astype(F32)).astype(BF16)
    return hi, lo


def _dot_nt(a, b):
    return lax.dot_general(a, b, (((1,), (1,)), ((), ())), preferred_element_type=F32)


def _dot_tn(a, b):
    return lax.dot_general(a, b, (((0,), (0,)), ((), ())), preferred_element_type=F32)


N_SUB = CHUNK // SUB


def _chunk_consts():
    i = lax.broadcasted_iota(jnp.int32, (CHUNK, CHUNK), 0)
    j = lax.broadcasted_iota(jnp.int32, (CHUNK, CHUNK), 1)
    lo = (i // SUB) * SUB
    hi = lo + SUB
    groups = [(j >= lo) & (j <= i), (j > i) & (j < hi), j < lo, j >= hi]
    sel = jnp.concatenate([jnp.where(g, 1.0, 0.0) for g in groups], axis=0).astype(BF16)
    return jnp.concatenate([sel, sel], axis=1), i >= j


def _decay_sums(sel2, g):
    g_hi, g_lo = _split_bf16(g * LOG2_E)
    return jnp.dot(sel2, jnp.concatenate([g_hi, g_lo], axis=0), preferred_element_type=F32)


def _gated_chunks(heads, causal):
    prep = [_chunk_operands(*h) for h in heads]
    scores = [_dot_nt(p[0], p[1]) for p in prep]
    kv = [jnp.dot(p[4], p[3], preferred_element_type=F32) for p in prep]
    outs, states = [], []
    for (q_t, k_t, q_st, k_st, v_t, st_bf, st_decay), s, upd, h in zip(prep, scores, kv, heads):
        s = jnp.where(causal, s, 0.0).astype(BF16)
        outs.append(_dot_nt(jnp.concatenate([q_st, s], axis=1),
                            jnp.concatenate([st_bf, v_t], axis=1)))
        states.append(h[4] * st_decay + upd)
    return outs, states


def _chunk_operands(q, k, v, dec, st):
    within, rest, before, after = dec
    dk = q.shape[1]
    cum = within + before
    q_st = (q * jnp.exp2(cum)).astype(BF16)
    k_st = (k * jnp.exp2(rest + after)).astype(BF16)
    q_in = (q * jnp.exp2(within)).astype(BF16)
    k_diag = (k * jnp.exp2(jnp.minimum(-within, EXP2_CLAMP))).astype(BF16)
    k_end = k * jnp.exp2(rest)
    k_end_bf = k_end.astype(BF16)

    blk = lambda x, b: x[b * SUB:(b + 1) * SUB]
    whole = {b: jnp.exp2(within[(b + 1) * SUB - 1:(b + 1) * SUB]) for b in range(1, N_SUB - 1)}
    skip = {}
    for bq in range(N_SUB):
        for bk in range(bq - 1):
            d = whole[bk + 1]
            for mid in range(bk + 2, bq):
                d = d * whole[mid]
            skip[bq, bk] = d
    zero = jnp.zeros((SUB, dk), BF16)
    k_cols, q_cols = [], []
    for bq in range(N_SUB):
        col = []
        for bk in range(N_SUB):
            if bk > bq:
                col.append(zero)
            elif bk == bq:
                col.append(blk(k_diag, bk))
            elif bk == bq - 1:
                col.append(blk(k_end_bf, bk))
            else:
                col.append((blk(k_end, bk) * skip[bq, bk]).astype(BF16))
        k_cols.append(jnp.concatenate(col, axis=0))
        q_cols.append(jnp.concatenate([blk(q_in, b) if b == bq else zero for b in range(N_SUB)],
                                      axis=0))
    q_tilde = jnp.concatenate(q_cols, axis=1)
    k_tilde = jnp.concatenate(k_cols, axis=1)
    v_t = v.T
    return q_tilde, k_tilde, q_st, k_st, v_t, st.astype(BF16), jnp.exp2(cum[CHUNK - 1:CHUNK])


def _head_norm_gate(o, norm_w, gate):
    return (o * _rms_scale(o) * norm_w * _silu(gate.astype(F32))).astype(BF16)


def _gla_kernel(q_ref, k_ref, v_ref, go_ref, code_ref, wgk_ref, bgk_ref, gn_ref, o_ref, st_ref,
                sums_ref, *, n_chunks, dk, dv):
    @pl.when(pl.program_id(1) == 0)
    def _():
        st_ref[...] = jnp.zeros_like(st_ref)

    sel2, causal = _chunk_consts()
    norm_w = gn_ref[...]
    q_scale = dk ** -0.5

    w_hi, w_lo = _split_bf16(wgk_ref[...])
    c_hi, c_lo = _split_bf16(code_ref[...])
    z = (jnp.dot(c_hi, w_hi, preferred_element_type=F32)
         + jnp.dot(c_lo, w_hi, preferred_element_type=F32)
         + jnp.dot(c_hi, w_lo, preferred_element_type=F32)) + bgk_ref[...]
    log_a = (jnp.minimum(z, 0.0) - jnp.log1p(jnp.exp(-jnp.abs(z)))) * (1.0 / GLA_GATE_NORMALIZER)
    for c in range(n_chunks):
        sums_ref[c] = _decay_sums(sel2, log_a[c * CHUNK:(c + 1) * CHUNK])

    def chunk_body(c, carry):
        rows = pl.ds(pl.multiple_of(c * CHUNK, CHUNK), CHUNK)
        heads = []
        for h in range(GLA_HEADS):
            kc = slice(h * dk, (h + 1) * dk)
            q = q_ref[rows, kc].astype(F32) * q_scale
            k = k_ref[rows, kc].astype(F32)
            dec = [sums_ref[c, n * CHUNK:(n + 1) * CHUNK, kc] for n in range(4)]
            heads.append((q, k, v_ref[rows, h * dv:(h + 1) * dv], dec, st_ref[h]))
        outs, states = _gated_chunks(heads, causal)
        for h in range(GLA_HEADS):
            vc = slice(h * dv, (h + 1) * dv)
            st_ref[h] = states[h]
            o_ref[rows, vc] = _head_norm_gate(outs[h], norm_w, go_ref[rows, vc])
        return carry

    lax.fori_loop(0, n_chunks, chunk_body, 0)


def _gla(p_a, p_code, wgk_pad, b_gk, gla_norm, batch, seq, t_blk=512):
    kw = wgk_pad.shape[1]
    vw = (p_a.shape[1] - 2 * kw) // 2
    dk = kw // GLA_HEADS
    dv = vw // GLA_HEADS
    nt = seq // t_blk
    row = lambda b, t: b * nt + t
    return pl.pallas_call(
        functools.partial(_gla_kernel, n_chunks=t_blk // CHUNK, dk=dk, dv=dv),
        out_shape=jax.ShapeDtypeStruct((batch * seq, vw), BF16),
        grid=(batch, nt),
        in_specs=[pl.BlockSpec((t_blk, kw), lambda b, t: (row(b, t), 0)),
                  pl.BlockSpec((t_blk, kw), lambda b, t: (row(b, t), 1)),
                  pl.BlockSpec((t_blk, vw), lambda b, t: (row(b, t), 1)),
                  pl.BlockSpec((t_blk, vw), lambda b, t: (row(b, t), 2)),
                  pl.BlockSpec((t_blk, LANES), lambda b, t: (row(b, t), 0)),
                  pl.BlockSpec((LANES, kw), lambda b, t: (0, 0)),
                  pl.BlockSpec((1, kw), lambda b, t: (0, 0)),
                  pl.BlockSpec((1, dv), lambda b, t: (0, 0))],
        out_specs=pl.BlockSpec((t_blk, vw), lambda b, t: (row(b, t), 0)),
        scratch_shapes=[pltpu.VMEM((GLA_HEADS, dv, dk), F32),
                        pltpu.VMEM((t_blk // CHUNK, 4 * CHUNK, kw), F32)],
        compiler_params=_cparams(2),
        name="gla_mixer",
    )(p_a, p_a, p_a, p_a, p_code, wgk_pad, b_gk, gla_norm)


def _hgrn_kernel(hq_ref, hi_ref, ho_ref, hf_ref, lbl_ref, hn_ref, o_ref, st_ref,
                 *, n_chunks, n_heads, dk, layer):
    @pl.when(pl.program_id(1) == 0)
    def _():
        st_ref[...] = jnp.zeros_like(st_ref)

    sel2, causal = _chunk_consts()
    logits = lbl_ref[...]
    p = jnp.exp(logits - jnp.max(logits, axis=0, keepdims=True))
    p = p / jnp.sum(p, axis=0, keepdims=True)
    lb = jnp.sum(p[:layer + 1], axis=0, keepdims=True)
    one_m_lb = jnp.sum(p[layer + 1:], axis=0, keepdims=True)
    norm_w = hn_ref[...]

    def chunk_body(c, carry):
        rows = pl.ds(pl.multiple_of(c * CHUNK, CHUNK), CHUNK)
        hf = hf_ref[rows, :]
        e = jnp.exp(-jnp.abs(hf))
        inv = 1.0 / (1.0 + e)
        pos = hf >= 0.0
        sig = jnp.where(pos, inv, e * inv)
        sig_neg = jnp.where(pos, e * inv, inv)
        log_f = jnp.log(lb + one_m_lb * sig)
        k_all = one_m_lb * sig_neg
        sums = _decay_sums(sel2, log_f)
        heads = []
        for h in range(n_heads):
            hc = slice(h * dk, (h + 1) * dk)
            q = _silu(hq_ref[rows, hc].astype(F32))
            dec = [sums[n * CHUNK:(n + 1) * CHUNK, hc] for n in range(4)]
            heads.append((q, k_all[:, hc], hi_ref[rows, hc], dec, st_ref[h]))
        outs, states = _gated_chunks(heads, causal)
        for h in range(n_heads):
            hc = slice(h * dk, (h + 1) * dk)
            st_ref[h] = states[h]
            o_ref[rows, hc] = _head_norm_gate(outs[h], norm_w, ho_ref[rows, hc])
        return carry

    lax.fori_loop(0, n_chunks, chunk_body, 0)


def _hgrn(p_q, p_r, p_f, lb_logits, hgrn_norm, layer, batch, seq, t_blk=512):
    w = p_f.shape[1]
    dk = HGRN_EXPAND
    n_heads = w // dk
    nt = seq // t_blk
    n_lb = lb_logits.shape[0]
    row = lambda b, t: b * nt + t
    return pl.pallas_call(
        functools.partial(_hgrn_kernel, n_chunks=t_blk // CHUNK, n_heads=n_heads, dk=dk,
                          layer=layer),
        out_shape=jax.ShapeDtypeStruct((batch * seq, w), BF16),
        grid=(batch, nt),
        in_specs=[pl.BlockSpec((t_blk, w), lambda b, t: (row(b, t), 0)),
                  pl.BlockSpec((t_blk, w), lambda b, t: (row(b, t), 0)),
                  pl.BlockSpec((t_blk, w), lambda b, t: (row(b, t), 1)),
                  pl.BlockSpec((t_blk, w), lambda b, t: (row(b, t), 0)),
                  pl.BlockSpec((n_lb, w), lambda b, t: (0, 0)),
                  pl.BlockSpec((1, dk), lambda b, t: (0, 0))],
        out_specs=pl.BlockSpec((t_blk, w), lambda b, t: (row(b, t), 0)),
        scratch_shapes=[pltpu.VMEM((n_heads, dk, dk), F32)],
        compiler_params=_cparams(2),
        name="hgrn_mixer",
    )(p_q, p_r, p_r, p_f, lb_logits, hgrn_norm)


def _merge_kernel(og_ref, oh_ref, zg_ref, zh_ref, wg_ref, wh_ref, bg_ref, bh_ref, o_ref,
                  wg_bf, wh_bf):
    @pl.when(pl.program_id(1) == 0)
    def _():
        wg_bf[...] = wg_ref[...].astype(BF16)
        wh_bf[...] = wh_ref[...].astype(BF16)

    a = jnp.dot(og_ref[...], wg_bf[...], preferred_element_type=F32)
    b = jnp.dot(oh_ref[...], wh_bf[...], preferred_element_type=F32)
    o_ref[...] = (_sigmoid(zg_ref[...].astype(F32) + bg_ref[...]) * a
                  + _sigmoid(zh_ref[...].astype(F32) + bh_ref[...]) * b).astype(o_ref.dtype)


def _merge(o_gla, o_hgrn, p_b, zg_col0, zh_col0, w_bg, w_bh, b_gates, tm=1024, tn=1024):
    m, kdim = o_gla.shape
    d = w_bg.shape[-1]
    x_spec = pl.BlockSpec((tm, kdim), lambda j, i: (i, 0))
    w_spec = pl.BlockSpec((None, kdim, tn), lambda j, i: (0, 0, j))
    return pl.pallas_call(
        _merge_kernel,
        out_shape=jax.ShapeDtypeStruct((m, d), BF16),
        grid=(d // tn, m // tm),
        in_specs=[x_spec, x_spec,
                  pl.BlockSpec((tm, tn), lambda j, i: (i, zg_col0 // tn + j)),
                  pl.BlockSpec((tm, tn), lambda j, i: (i, zh_col0 // tn + j)),
                  w_spec, w_spec,
                  pl.BlockSpec((None, 1, tn), lambda j, i: (0, 0, j)),
                  pl.BlockSpec((None, 1, tn), lambda j, i: (1, 0, j))],
        out_specs=pl.BlockSpec((tm, tn), lambda j, i: (i, j)),
        scratch_shapes=[pltpu.VMEM((kdim, tn), BF16), pltpu.VMEM((kdim, tn), BF16)],
        compiler_params=_cparams(2),
        name="branch_merge",
    )(o_gla, o_hgrn, p_b, p_b, w_bg, w_bh, b_gates, b_gates)


def kernel(x, ffn1_pre_norm, ffn1_w_gate, ffn1_w_up, ffn1_w_down, ffn1_post_norm, mix_pre_norm, w_in, gla_w_gk_up, gla_b_gk, gla_norm, hgrn_lb_logits, hgrn_norm, w_branch_gla, w_branch_hgrn, b_branch_gates, w_out, mix_post_norm, ffn2_pre_norm, ffn2_w_gate, ffn2_w_up, ffn2_w_down, ffn2_post_norm):
    batch, seq, d_model = x.shape
    depth = ffn1_w_gate.shape[0]
    m = batch * seq
    kw = gla_w_gk_up.shape[-1]
    vw = d_model // 2
    a_cols = 2 * kw + 2 * vw
    code0 = a_cols
    hq0 = code0 + GLA_GATE_RANK
    hf0, hi0 = hq0 + vw, hq0 + 2 * vw

    h = x.reshape(m, d_model)
    u = _rmsnorm(h, ffn1_pre_norm[0:1])
    for l in range(depth):
        mid = _gateup(u, ffn1_w_gate[l:l + 1], ffn1_w_up[l:l + 1])
        h, u = _rows(mid, ffn1_w_down[l].astype(BF16), h, ffn1_post_norm[l:l + 1],
                     mix_pre_norm[l:l + 1], 0.5)

        w_l = jnp.swapaxes(w_in[l:l + 1], 1, 2)
        p_a = _proj(u, w_l, 0, a_cols, BF16)
        p_code = _proj(u, w_l, code0, LANES, F32, tn=LANES)
        p_q = _proj(u, w_l, hq0, vw, BF16)
        p_f = _proj(u, w_l, hf0, vw, F32)
        p_r = _proj(u, w_l, hi0, 2 * vw + 2 * d_model, BF16)
        wgk_pad = jnp.pad(gla_w_gk_up[l], ((0, LANES - GLA_GATE_RANK), (0, 0)))
        o_gla = _gla(p_a, p_code, wgk_pad, gla_b_gk[l:l + 1], gla_norm[l:l + 1], batch, seq)
        o_hgrn = _hgrn(p_q, p_r, p_f, hgrn_lb_logits, hgrn_norm[l:l + 1], l, batch, seq)
        merged = _merge(o_gla, o_hgrn, p_r, 2 * vw, 2 * vw + d_model,
                        w_branch_gla[l:l + 1], w_branch_hgrn[l:l + 1],
                        b_branch_gates[l].reshape(2, 1, d_model))
        h, u = _rows(merged, w_out[l].astype(BF16), h, mix_post_norm[l:l + 1],
                     ffn2_pre_norm[l:l + 1], 1.0, tm=512)

        mid = _gateup(u, ffn2_w_gate[l:l + 1], ffn2_w_up[l:l + 1])
        next_norm = ffn1_pre_norm[l + 1:l + 2] if l + 1 < depth else None
        h, u = _rows(mid, ffn2_w_down[l].astype(BF16), h, ffn2_post_norm[l:l + 1], next_norm, 0.5)
    return h.reshape(batch, seq, d_model)
```

```python
import functools

import jax
import jax.numpy as jnp
from jax import lax
from jax.experimental import pallas as pl
from jax.experimental.pallas import tpu as pltpu

F32 = jnp.float32
BF16 = jnp.bfloat16

EPS = 1e-6
CHUNK = 64
SUB = 16
GLA_HEADS = 4
GLA_GATE_RANK = 16
GLA_GATE_NORMALIZER = 16.0
HGRN_EXPAND = 128
LOG2_E = 1.4426950408889634
EXP2_CLAMP = 115.0

LANES = 128
SUBLANES = 8
VMEM_LIMIT = 56 * 1024 * 1024


def _cparams(n_axes):
    return pltpu.CompilerParams(
        dimension_semantics=("arbitrary",) * n_axes, vmem_limit_bytes=VMEM_LIMIT)


def _sigmoid(x):
    return 1.0 / (1.0 + jnp.exp(-x))


def _silu(x):
    return x * _sigmoid(x)


def _rms_scale(x):
    return lax.rsqrt(jnp.mean(x * x, axis=-1, keepdims=True) + EPS)


def _rmsnorm_kernel(x_ref, w_ref, o_ref):
    x = x_ref[...]
    o_ref[...] = (x * _rms_scale(x) * w_ref[...]).astype(o_ref.dtype)


def _rmsnorm(x, w, tm=512):
    m, d = x.shape
    return pl.pallas_call(
        _rmsnorm_kernel,
        out_shape=jax.ShapeDtypeStruct((m, d), BF16),
        grid=(m // tm,),
        in_specs=[pl.BlockSpec((tm, d), lambda i: (i, 0)),
                  pl.BlockSpec((1, d), lambda i: (0, 0))],
        out_specs=pl.BlockSpec((tm, d), lambda i: (i, 0)),
        compiler_params=_cparams(1),
        name="rmsnorm",
    )(x, w)


def _gateup_kernel(u_ref, wg_ref, wu_ref, wd_ref, o_ref, wd_bf_ref, wg_bf, wu_bf):
    @pl.when(pl.program_id(1) == 0)
    def _():
        wg_bf[...] = wg_ref[...].astype(BF16)
        wu_bf[...] = wu_ref[...].astype(BF16)
        wd_bf_ref[...] = wd_ref[...].astype(BF16)

    u = u_ref[...]
    g = jnp.dot(u, wg_bf[...], preferred_element_type=F32)
    up = jnp.dot(u, wu_bf[...], preferred_element_type=F32)
    o_ref[...] = (_silu(g) * up).astype(o_ref.dtype)


def _gateup(u, w_gate, w_up, w_down, tm=1024, tn=512):
    m, d = u.shape
    f = w_gate.shape[-1]
    w_spec = pl.BlockSpec((None, d, tn), lambda j, i: (0, 0, j))
    return pl.pallas_call(
        _gateup_kernel,
        out_shape=[jax.ShapeDtypeStruct((m, f), BF16), jax.ShapeDtypeStruct((f, d), BF16)],
        grid=(pl.cdiv(f, tn), m // tm),
        in_specs=[pl.BlockSpec((tm, d), lambda j, i: (i, 0)), w_spec, w_spec,
                  pl.BlockSpec((None, tn, d), lambda j, i: (0, j, 0))],
        out_specs=[pl.BlockSpec((tm, tn), lambda j, i: (i, j)),
                   pl.BlockSpec((tn, d), lambda j, i: (j, 0))],
        scratch_shapes=[pltpu.VMEM((d, tn), BF16), pltpu.VMEM((d, tn), BF16)],
        compiler_params=_cparams(2),
        name="ffn_gateup",
    )(u, w_gate, w_up, w_down)


def _proj_kernel(u_ref, w_ref, *rest, shift, tn):
    if shift:
        wn_ref, o_ref, w_bf = rest
    else:
        o_ref, w_bf = rest

    @pl.when(pl.program_id(1) == 0)
    def _():
        w = w_ref[...]
        if shift:
            w = jnp.concatenate([w, wn_ref[...]], axis=0)[shift:shift + tn]
        w_bf[...] = w.astype(BF16)

    o_ref[...] = _dot_nt(u_ref[...], w_bf[...]).astype(o_ref.dtype)


def _proj(u, wt, col0, n_cols, out_dtype, tn=1024):
    m, d = u.shape
    tm = 2048 if n_cols >= 3 * tn else 1024
    shift = col0 % tn
    blk0 = col0 // tn
    assert n_cols % tn == 0 and shift % SUBLANES == 0
    in_specs = [pl.BlockSpec((tm, d), lambda j, i: (i, 0)),
                pl.BlockSpec((None, tn, d), lambda j, i: (0, j + blk0, 0))]
    args = [u, wt]
    if shift:
        assert tn % shift == 0
        per_tile = tn // shift
        in_specs.append(pl.BlockSpec((None, shift, d), lambda j, i: (0, (j + blk0 + 1) * per_tile, 0)))
        args.append(wt)
    return pl.pallas_call(
        functools.partial(_proj_kernel, shift=shift, tn=tn),
        out_shape=jax.ShapeDtypeStruct((m, n_cols), out_dtype),
        grid=(n_cols // tn, m // tm),
        in_specs=in_specs,
        out_specs=pl.BlockSpec((tm, tn), lambda j, i: (i, j)),
        scratch_shapes=[pltpu.VMEM((tn, d), BF16)],
        compiler_params=_cparams(2),
        name="in_proj",
    )(*args)


def _rows_kernel(x_ref, w_ref, res_ref, post_ref, *rest, tn, res_scale, emit_next):
    if emit_next:
        next_ref, h_ref, u_ref, acc_a, acc_b = rest
    else:
        h_ref, acc_a, acc_b = rest
    i = pl.program_id(0)
    d = h_ref.shape[1]
    col_tiles = [slice(c, c + tn) for c in range(0, d, tn)]

    @pl.when(i == 0)
    def _():
        acc_b[...] = jnp.zeros_like(acc_b)

    def step(acc_cur, acc_prev):
        x = x_ref[...]
        for cols in col_tiles:
            acc_cur[:, cols] = jnp.dot(x, w_ref[:, cols], preferred_element_type=F32)

        ssq = None
        for cols in col_tiles:
            a = acc_prev[:, cols]
            s = jnp.sum(a * a, axis=-1, keepdims=True)
            ssq = s if ssq is None else ssq + s
        scale = lax.rsqrt(ssq / d + EPS) * res_scale
        hsq = None
        for cols in col_tiles:
            h = res_ref[:, cols] + acc_prev[:, cols] * scale * post_ref[:, cols]
            h_ref[:, cols] = h
            if emit_next:
                s = jnp.sum(h * h, axis=-1, keepdims=True)
                hsq = s if hsq is None else hsq + s
        if emit_next:
            nscale = lax.rsqrt(hsq / d + EPS)
            for cols in col_tiles:
                u_ref[:, cols] = (h_ref[:, cols] * nscale * next_ref[:, cols]).astype(u_ref.dtype)

    @pl.when(i % 2 == 0)
    def _():
        step(acc_a, acc_b)

    @pl.when(i % 2 == 1)
    def _():
        step(acc_b, acc_a)


def _rows(x, w, res, post_w, next_w, res_scale, tm=256, tn=512):
    m, k = x.shape
    d = w.shape[-1]
    n_tiles = m // tm
    emit_next = next_w is not None
    lag_spec = pl.BlockSpec((tm, d), lambda i: (jnp.maximum(i - 1, 0), 0))
    vec_spec = pl.BlockSpec((1, d), lambda i: (0, 0))
    in_specs = [pl.BlockSpec((tm, k), lambda i: (jnp.minimum(i, n_tiles - 1), 0)),
                pl.BlockSpec((k, d), lambda i: (0, 0), pipeline_mode=pl.Buffered(1)),
                lag_spec, vec_spec]
    args = [x, w, res, post_w]
    out_shape = [jax.ShapeDtypeStruct((m, d), F32)]
    out_specs = [lag_spec]
    if emit_next:
        in_specs.append(vec_spec)
        args.append(next_w)
        out_shape.append(jax.ShapeDtypeStruct((m, d), BF16))
        out_specs.append(lag_spec)
    outs = pl.pallas_call(
        functools.partial(_rows_kernel, tn=tn, res_scale=res_scale, emit_next=emit_next),
        out_shape=out_shape,
        grid=(n_tiles + 1,),
        in_specs=in_specs,
        out_specs=out_specs,
        scratch_shapes=[pltpu.VMEM((tm, d), F32), pltpu.VMEM((tm, d), F32)],
        compiler_params=_cparams(1),
        name="rows_matmul_norm",
    )(*args)
    return outs if emit_next else (outs[0], None)


def _split_bf16(x):
    hi = x.astype(BF16)
    lo = (x - hi.astype(F32)).astype(BF16)
    return hi, lo


def _dot_nt(a, b):
    return lax.dot_general(a, b, (((1,), (1,)), ((), ())), preferred_element_type=F32)


def _dot_tn(a, b):
    return lax.dot_general(a, b, (((0,), (0,)), ((), ())), preferred_element_type=F32)


N_SUB = CHUNK // SUB


def _chunk_consts():
    i = lax.broadcasted_iota(jnp.int32, (CHUNK, CHUNK), 0)
    j = lax.broadcasted_iota(jnp.int32, (CHUNK, CHUNK), 1)
    lo = (i // SUB) * SUB
    hi = lo + SUB
    groups = [(j >= lo) & (j <= i), (j > i) & (j < hi), j < lo, j >= hi]
    sel = jnp.concatenate([jnp.where(g, 1.0, 0.0) for g in groups], axis=0).astype(BF16)
    return jnp.concatenate([sel, sel], axis=1), i >= j


def _decay_sums(sel2, g):
    g_hi, g_lo = _split_bf16(g * LOG2_E)
    return jnp.dot(sel2, jnp.concatenate([g_hi, g_lo], axis=0), preferred_element_type=F32)


def _gated_chunks(heads, causal):
    prep = [_chunk_operands(*h) for h in heads]
    scores = [_dot_nt(p[0], p[1]) for p in prep]
    kv = [jnp.dot(p[4], p[3], preferred_element_type=F32) for p in prep]
    outs, states = [], []
    for (q_t, k_t, q_st, k_st, v_t, st_bf, st_decay), s, upd, h in zip(prep, scores, kv, heads):
        s = jnp.where(causal, s, 0.0).astype(BF16)
        outs.append(_dot_nt(jnp.concatenate([q_st, s], axis=1),
                            jnp.concatenate([st_bf, v_t], axis=1)))
        states.append(h[4] * st_decay + upd)
    return outs, states


def _chunk_operands(q, k, v, dec, st):
    within, rest, before, after = dec
    dk = q.shape[1]
    cum = within + before
    q_st = (q * jnp.exp2(cum)).astype(BF16)
    k_st = (k * jnp.exp2(rest + after)).astype(BF16)
    q_in = (q * jnp.exp2(within)).astype(BF16)
    k_diag = (k * jnp.exp2(jnp.minimum(-within, EXP2_CLAMP))).astype(BF16)
    k_end = k * jnp.exp2(rest)
    k_end_bf = k_end.astype(BF16)

    blk = lambda x, b: x[b * SUB:(b + 1) * SUB]
    whole = {b: jnp.exp2(within[(b + 1) * SUB - 1:(b + 1) * SUB]) for b in range(1, N_SUB - 1)}
    skip = {}
    for bq in range(N_SUB):
        for bk in range(bq - 1):
            d = whole[bk + 1]
            for mid in range(bk + 2, bq):
                d = d * whole[mid]
            skip[bq, bk] = d
    zero = jnp.zeros((SUB, dk), BF16)
    k_cols, q_cols = [], []
    for bq in range(N_SUB):
        col = []
        for bk in range(N_SUB):
            if bk > bq:
                col.append(zero)
            elif bk == bq:
                col.append(blk(k_diag, bk))
            elif bk == bq - 1:
                col.append(blk(k_end_bf, bk))
            else:
                col.append((blk(k_end, bk) * skip[bq, bk]).astype(BF16))
        k_cols.append(jnp.concatenate(col, axis=0))
        q_cols.append(jnp.concatenate([blk(q_in, b) if b == bq else zero for b in range(N_SUB)],
                                      axis=0))
    q_tilde = jnp.concatenate(q_cols, axis=1)
    k_tilde = jnp.concatenate(k_cols, axis=1)
    v_t = v.T
    return q_tilde, k_tilde, q_st, k_st, v_t, st.astype(BF16), jnp.exp2(cum[CHUNK - 1:CHUNK])


def _head_norm_gate(o, norm_w, gate):
    return (o * _rms_scale(o) * norm_w * _silu(gate.astype(F32))).astype(BF16)


def _gla_kernel(q_ref, k_ref, v_ref, go_ref, code_ref, wgk_ref, bgk_ref, gn_ref, o_ref, st_ref,
                sums_ref, *, n_chunks, dk, dv):
    @pl.when(pl.program_id(1) == 0)
    def _():
        st_ref[...] = jnp.zeros_like(st_ref)

    sel2, causal = _chunk_consts()
    norm_w = gn_ref[...]
    q_scale = dk ** -0.5

    w_hi, w_lo = _split_bf16(wgk_ref[...])
    c_hi, c_lo = _split_bf16(code_ref[...])
    z = (jnp.dot(c_hi, w_hi, preferred_element_type=F32)
         + jnp.dot(c_lo, w_hi, preferred_element_type=F32)
         + jnp.dot(c_hi, w_lo, preferred_element_type=F32)) + bgk_ref[...]
    log_a = (jnp.minimum(z, 0.0) - jnp.log1p(jnp.exp(-jnp.abs(z)))) * (1.0 / GLA_GATE_NORMALIZER)
    for c in range(n_chunks):
        sums_ref[c] = _decay_sums(sel2, log_a[c * CHUNK:(c + 1) * CHUNK])

    def chunk_body(c, carry):
        rows = pl.ds(pl.multiple_of(c * CHUNK, CHUNK), CHUNK)
        heads = []
        for h in range(GLA_HEADS):
            kc = slice(h * dk, (h + 1) * dk)
            q = q_ref[rows, kc].astype(F32) * q_scale
            k = k_ref[rows, kc].astype(F32)
            dec = [sums_ref[c, n * CHUNK:(n + 1) * CHUNK, kc] for n in range(4)]
            heads.append((q, k, v_ref[rows, h * dv:(h + 1) * dv], dec, st_ref[h]))
        outs, states = _gated_chunks(heads, causal)
        for h in range(GLA_HEADS):
            vc = slice(h * dv, (h + 1) * dv)
            st_ref[h] = states[h]
            o_ref[rows, vc] = _head_norm_gate(outs[h], norm_w, go_ref[rows, vc])
        return carry

    lax.fori_loop(0, n_chunks, chunk_body, 0)


def _gla(p_a, p_code, wgk_pad, b_gk, gla_norm, batch, seq, t_blk=512):
    kw = wgk_pad.shape[1]
    vw = (p_a.shape[1] - 2 * kw) // 2
    dk = kw // GLA_HEADS
    dv = vw // GLA_HEADS
    nt = seq // t_blk
    row = lambda b, t: b * nt + t
    return pl.pallas_call(
        functools.partial(_gla_kernel, n_chunks=t_blk // CHUNK, dk=dk, dv=dv),
        out_shape=jax.ShapeDtypeStruct((batch * seq, vw), BF16),
        grid=(batch, nt),
        in_specs=[pl.BlockSpec((t_blk, kw), lambda b, t: (row(b, t), 0)),
                  pl.BlockSpec((t_blk, kw), lambda b, t: (row(b, t), 1)),
                  pl.BlockSpec((t_blk, vw), lambda b, t: (row(b, t), 1)),
                  pl.BlockSpec((t_blk, vw), lambda b, t: (row(b, t), 2)),
                  pl.BlockSpec((t_blk, LANES), lambda b, t: (row(b, t), 0)),
                  pl.BlockSpec((LANES, kw), lambda b, t: (0, 0)),
                  pl.BlockSpec((1, kw), lambda b, t: (0, 0)),
                  pl.BlockSpec((1, dv), lambda b, t: (0, 0))],
        out_specs=pl.BlockSpec((t_blk, vw), lambda b, t: (row(b, t), 0)),
        scratch_shapes=[pltpu.VMEM((GLA_HEADS, dv, dk), F32),
                        pltpu.VMEM((t_blk // CHUNK, 4 * CHUNK, kw), F32)],
        compiler_params=_cparams(2),
        name="gla_mixer",
    )(p_a, p_a, p_a, p_a, p_code, wgk_pad, b_gk, gla_norm)


def _hgrn_kernel(hq_ref, hi_ref, ho_ref, hf_ref, lbl_ref, hn_ref, o_ref, st_ref,
                 *, n_chunks, n_heads, dk, layer):
    @pl.when(pl.program_id(1) == 0)
    def _():
        st_ref[...] = jnp.zeros_like(st_ref)

    sel2, causal = _chunk_consts()
    logits = lbl_ref[...]
    p = jnp.exp(logits - jnp.max(logits, axis=0, keepdims=True))
    p = p / jnp.sum(p, axis=0, keepdims=True)
    lb = jnp.sum(p[:layer + 1], axis=0, keepdims=True)
    one_m_lb = jnp.sum(p[layer + 1:], axis=0, keepdims=True)
    norm_w = hn_ref[...]

    def chunk_body(c, carry):
        rows = pl.ds(pl.multiple_of(c * CHUNK, CHUNK), CHUNK)
        hf = hf_ref[rows, :]
        e = jnp.exp(-jnp.abs(hf))
        inv = 1.0 / (1.0 + e)
        pos = hf >= 0.0
        sig = jnp.where(pos, inv, e * inv)
        sig_neg = jnp.where(pos, e * inv, inv)
        log_f = jnp.log(lb + one_m_lb * sig)
        k_all = one_m_lb * sig_neg
        sums = _decay_sums(sel2, log_f)
        heads = []
        for h in range(n_heads):
            hc = slice(h * dk, (h + 1) * dk)
            q = _silu(hq_ref[rows, hc].astype(F32))
            dec = [sums[n * CHUNK:(n + 1) * CHUNK, hc] for n in range(4)]
            heads.append((q, k_all[:, hc], hi_ref[rows, hc], dec, st_ref[h]))
        outs, states = _gated_chunks(heads, causal)
        for h in range(n_heads):
            hc = slice(h * dk, (h + 1) * dk)
            st_ref[h] = states[h]
            o_ref[rows, hc] = _head_norm_gate(outs[h], norm_w, ho_ref[rows, hc])
        return carry

    lax.fori_loop(0, n_chunks, chunk_body, 0)


def _hgrn(p_q, p_r, p_f, lb_logits, hgrn_norm, layer, batch, seq, t_blk=512):
    w = p_f.shape[1]
    dk = HGRN_EXPAND
    n_heads = w // dk
    nt = seq // t_blk
    n_lb = lb_logits.shape[0]
    row = lambda b, t: b * nt + t
    return pl.pallas_call(
        functools.partial(_hgrn_kernel, n_chunks=t_blk // CHUNK, n_heads=n_heads, dk=dk,
                          layer=layer),
        out_shape=jax.ShapeDtypeStruct((batch * seq, w), BF16),
        grid=(batch, nt),
        in_specs=[pl.BlockSpec((t_blk, w), lambda b, t: (row(b, t), 0)),
                  pl.BlockSpec((t_blk, w), lambda b, t: (row(b, t), 0)),
                  pl.BlockSpec((t_blk, w), lambda b, t: (row(b, t), 1)),
                  pl.BlockSpec((t_blk, w), lambda b, t: (row(b, t), 0)),
                  pl.BlockSpec((n_lb, w), lambda b, t: (0, 0)),
                  pl.BlockSpec((1, dk), lambda b, t: (0, 0))],
        out_specs=pl.BlockSpec((t_blk, w), lambda b, t: (row(b, t), 0)),
        scratch_shapes=[pltpu.VMEM((n_heads, dk, dk), F32)],
        compiler_params=_cparams(2),
        name="hgrn_mixer",
    )(p_q, p_r, p_r, p_f, lb_logits, hgrn_norm)


def _merge_kernel(og_ref, oh_ref, zg_ref, zh_ref, wg_ref, wh_ref, bg_ref, bh_ref, wo_ref,
                  o_ref, wo_bf_ref, wg_bf, wh_bf):
    @pl.when(pl.program_id(1) == 0)
    def _():
        wg_bf[...] = wg_ref[...].astype(BF16)
        wh_bf[...] = wh_ref[...].astype(BF16)

    wo_bf_ref[...] = wo_ref[...].astype(BF16)
    a = jnp.dot(og_ref[...], wg_bf[...], preferred_element_type=F32)
    b = jnp.dot(oh_ref[...], wh_bf[...], preferred_element_type=F32)
    o_ref[...] = (_sigmoid(zg_ref[...].astype(F32) + bg_ref[...]) * a
                  + _sigmoid(zh_ref[...].astype(F32) + bh_ref[...]) * b).astype(o_ref.dtype)


def _merge(o_gla, o_hgrn, p_b, zg_col0, zh_col0, w_bg, w_bh, b_gates, w_out, tm=1024, tn=1024):
    m, kdim = o_gla.shape
    d = w_bg.shape[-1]
    ni = m // tm
    k_out, d_out = w_out.shape[1:]
    rb = k_out // ((d // tn) * ni)
    assert rb * (d // tn) * ni == k_out and rb % (2 * SUBLANES) == 0
    x_spec = pl.BlockSpec((tm, kdim), lambda j, i: (i, 0))
    w_spec = pl.BlockSpec((None, kdim, tn), lambda j, i: (0, 0, j))
    return pl.pallas_call(
        _merge_kernel,
        out_shape=[jax.ShapeDtypeStruct((m, d), BF16), jax.ShapeDtypeStruct((k_out, d_out), BF16)],
        grid=(d // tn, ni),
        in_specs=[x_spec, x_spec,
                  pl.BlockSpec((tm, tn), lambda j, i: (i, zg_col0 // tn + j)),
                  pl.BlockSpec((tm, tn), lambda j, i: (i, zh_col0 // tn + j)),
                  w_spec, w_spec,
                  pl.BlockSpec((None, 1, tn), lambda j, i: (0, 0, j)),
                  pl.BlockSpec((None, 1, tn), lambda j, i: (1, 0, j)),
                  pl.BlockSpec((None, rb, d_out), lambda j, i: (0, j * ni + i, 0))],
        out_specs=[pl.BlockSpec((tm, tn), lambda j, i: (i, j)),
                   pl.BlockSpec((rb, d_out), lambda j, i: (j * ni + i, 0))],
        scratch_shapes=[pltpu.VMEM((kdim, tn), BF16), pltpu.VMEM((kdim, tn), BF16)],
        compiler_params=_cparams(2),
        name="branch_merge",
    )(o_gla, o_hgrn, p_b, p_b, w_bg, w_bh, b_gates, b_gates, w_out)


def kernel(x, ffn1_pre_norm, ffn1_w_gate, ffn1_w_up, ffn1_w_down, ffn1_post_norm, mix_pre_norm, w_in, gla_w_gk_up, gla_b_gk, gla_norm, hgrn_lb_logits, hgrn_norm, w_branch_gla, w_branch_hgrn, b_branch_gates, w_out, mix_post_norm, ffn2_pre_norm, ffn2_w_gate, ffn2_w_up, ffn2_w_down, ffn2_post_norm):
    batch, seq, d_model = x.shape
    depth = ffn1_w_gate.shape[0]
    m = batch * seq
    kw = gla_w_gk_up.shape[-1]
    vw = d_model // 2
    a_cols = 2 * kw + 2 * vw
    code0 = a_cols
    hq0 = code0 + GLA_GATE_RANK
    hf0, hi0 = hq0 + vw, hq0 + 2 * vw

    h = x.reshape(m, d_model)
    u = _rmsnorm(h, ffn1_pre_norm[0:1])
    for l in range(depth):
        mid, w_down = _gateup(u, ffn1_w_gate[l:l + 1], ffn1_w_up[l:l + 1], ffn1_w_down[l:l + 1])
        h, u = _rows(mid, w_down, h, ffn1_post_norm[l:l + 1],
                     mix_pre_norm[l:l + 1], 0.5)

        w_l = jnp.swapaxes(w_in[l:l + 1], 1, 2)
        p_a = _proj(u, w_l, 0, a_cols, BF16)
        p_code = _proj(u, w_l, code0, LANES, F32, tn=LANES)
        p_q = _proj(u, w_l, hq0, vw, BF16)
        p_f = _proj(u, w_l, hf0, vw, F32)
        p_r = _proj(u, w_l, hi0, 2 * vw + 2 * d_model, BF16)
        wgk_pad = jnp.pad(gla_w_gk_up[l], ((0, LANES - GLA_GATE_RANK), (0, 0)))
        o_gla = _gla(p_a, p_code, wgk_pad, gla_b_gk[l:l + 1], gla_norm[l:l + 1], batch, seq)
        o_hgrn = _hgrn(p_q, p_r, p_f, hgrn_lb_logits, hgrn_norm[l:l + 1], l, batch, seq)
        merged, w_out_bf = _merge(o_gla, o_hgrn, p_r, 2 * vw, 2 * vw + d_model,
                                  w_branch_gla[l:l + 1], w_branch_hgrn[l:l + 1],
                                  b_branch_gates[l].reshape(2, 1, d_model), w_out[l:l + 1])
        h, u = _rows(merged, w_out_bf, h, mix_post_norm[l:l + 1],
                     ffn2_pre_norm[l:l + 1], 1.0, tm=512)

        mid, w_down = _gateup(u, ffn2_w_gate[l:l + 1], ffn2_w_up[l:l + 1], ffn2_w_down[l:l + 1])
        next_norm = ffn1_pre_norm[l + 1:l + 2] if l + 1 < depth else None
        h, u = _rows(mid, w_down, h, ffn2_post_norm[l:l + 1], next_norm, 0.5)
    return h.reshape(batch, seq, d_model)
```

```python
import functools

import jax
import jax.numpy as jnp
from jax import lax
from jax.experimental import pallas as pl
from jax.experimental.pallas import tpu as pltpu

F32 = jnp.float32
BF16 = jnp.bfloat16

EPS = 1e-6
CHUNK = 64
SUB = 16
GLA_HEADS = 4
GLA_GATE_RANK = 16
GLA_GATE_NORMALIZER = 16.0
HGRN_EXPAND = 128
LOG2_E = 1.4426950408889634
EXP2_CLAMP = 115.0

LANES = 128
SUBLANES = 8
VMEM_LIMIT = 62 * 1024 * 1024


def _cparams(n_axes):
    return pltpu.CompilerParams(
        dimension_semantics=("arbitrary",) * n_axes, vmem_limit_bytes=VMEM_LIMIT)


def _sigmoid(x):
    return 1.0 / (1.0 + jnp.exp(-x))


def _silu(x):
    return x * _sigmoid(x)


def _rms_scale(x):
    return lax.rsqrt(jnp.mean(x * x, axis=-1, keepdims=True) + EPS)


def _rmsnorm_kernel(x_ref, w_ref, o_ref):
    x = x_ref[...]
    o_ref[...] = (x * _rms_scale(x) * w_ref[...]).astype(o_ref.dtype)


def _rmsnorm(x, w, tm=512):
    m, d = x.shape
    return pl.pallas_call(
        _rmsnorm_kernel,
        out_shape=jax.ShapeDtypeStruct((m, d), BF16),
        grid=(m // tm,),
        in_specs=[pl.BlockSpec((tm, d), lambda i: (i, 0)),
                  pl.BlockSpec((1, d), lambda i: (0, 0))],
        out_specs=pl.BlockSpec((tm, d), lambda i: (i, 0)),
        compiler_params=_cparams(1),
        name="rmsnorm",
    )(x, w)


def _gateup_kernel(u_ref, wg_ref, wu_ref, wd_ref, o_ref, wd_bf_ref, wg_bf, wu_bf):
    @pl.when(pl.program_id(1) == 0)
    def _():
        wg_bf[...] = wg_ref[...].astype(BF16)
        wu_bf[...] = wu_ref[...].astype(BF16)
        wd_bf_ref[...] = wd_ref[...].astype(BF16)

    for r in range(0, u_ref.shape[0], GATEUP_SUB_ROWS):
        rows = slice(r, r + GATEUP_SUB_ROWS)
        u = u_ref[rows, :]
        g = jnp.dot(u, wg_bf[...], preferred_element_type=F32)
        up = jnp.dot(u, wu_bf[...], preferred_element_type=F32)
        o_ref[rows, :] = (_silu(g) * up).astype(o_ref.dtype)


GATEUP_SUB_ROWS = 1024


def _gateup(u, w_gate, w_up, w_down, tm=2048, tn=512):
    m, d = u.shape
    f = w_gate.shape[-1]
    w_spec = pl.BlockSpec((None, d, tn), lambda j, i: (0, 0, j))
    return pl.pallas_call(
        _gateup_kernel,
        out_shape=[jax.ShapeDtypeStruct((m, f), BF16), jax.ShapeDtypeStruct((f, d), BF16)],
        grid=(pl.cdiv(f, tn), m // tm),
        in_specs=[pl.BlockSpec((tm, d), lambda j, i: (i, 0)), w_spec, w_spec,
                  pl.BlockSpec((None, tn, d), lambda j, i: (0, j, 0))],
        out_specs=[pl.BlockSpec((tm, tn), lambda j, i: (i, j)),
                   pl.BlockSpec((tn, d), lambda j, i: (j, 0))],
        scratch_shapes=[pltpu.VMEM((d, tn), BF16), pltpu.VMEM((d, tn), BF16)],
        compiler_params=_cparams(2),
        name="ffn_gateup",
    )(u, w_gate, w_up, w_down)


def _proj_kernel(u_ref, w_ref, *rest, shift, tn):
    if shift:
        wn_ref, o_ref, w_bf = rest
    else:
        o_ref, w_bf = rest

    @pl.when(pl.program_id(1) == 0)
    def _():
        w = w_ref[...]
        if shift:
            w = jnp.concatenate([w, wn_ref[...]], axis=0)[shift:shift + tn]
        w_bf[...] = w.astype(BF16)

    o_ref[...] = _dot_nt(u_ref[...], w_bf[...]).astype(o_ref.dtype)


def _proj(u, wt, col0, n_cols, out_dtype, tn=1024):
    m, d = u.shape
    tm = 2048 if n_cols >= 3 * tn else 1024
    shift = col0 % tn
    blk0 = col0 // tn
    assert n_cols % tn == 0 and shift % SUBLANES == 0
    in_specs = [pl.BlockSpec((tm, d), lambda j, i: (i, 0)),
                pl.BlockSpec((None, tn, d), lambda j, i: (0, j + blk0, 0))]
    args = [u, wt]
    if shift:
        assert tn % shift == 0
        per_tile = tn // shift
        in_specs.append(pl.BlockSpec((None, shift, d), lambda j, i: (0, (j + blk0 + 1) * per_tile, 0)))
        args.append(wt)
    return pl.pallas_call(
        functools.partial(_proj_kernel, shift=shift, tn=tn),
        out_shape=jax.ShapeDtypeStruct((m, n_cols), out_dtype),
        grid=(n_cols // tn, m // tm),
        in_specs=in_specs,
        out_specs=pl.BlockSpec((tm, tn), lambda j, i: (i, j)),
        scratch_shapes=[pltpu.VMEM((tn, d), BF16)],
        compiler_params=_cparams(2),
        name="in_proj",
    )(*args)


def _rows_kernel(x_ref, w_ref, res_ref, post_ref, *rest, tn, res_scale, emit_next):
    if emit_next:
        next_ref, h_ref, u_ref, acc_a, acc_b = rest
    else:
        h_ref, acc_a, acc_b = rest
    i = pl.program_id(0)
    d = h_ref.shape[1]
    col_tiles = [slice(c, c + tn) for c in range(0, d, tn)]

    @pl.when(i == 0)
    def _():
        acc_b[...] = jnp.zeros_like(acc_b)

    def step(acc_cur, acc_prev):
        x = x_ref[...]
        for cols in col_tiles:
            acc_cur[:, cols] = jnp.dot(x, w_ref[:, cols], preferred_element_type=F32)

        ssq = None
        for cols in col_tiles:
            a = acc_prev[:, cols]
            s = jnp.sum(a * a, axis=-1, keepdims=True)
            ssq = s if ssq is None else ssq + s
        scale = lax.rsqrt(ssq / d + EPS) * res_scale
        hsq = None
        for cols in col_tiles:
            h = res_ref[:, cols] + acc_prev[:, cols] * scale * post_ref[:, cols]
            h_ref[:, cols] = h
            if emit_next:
                s = jnp.sum(h * h, axis=-1, keepdims=True)
                hsq = s if hsq is None else hsq + s
        if emit_next:
            nscale = lax.rsqrt(hsq / d + EPS)
            for cols in col_tiles:
                u_ref[:, cols] = (h_ref[:, cols] * nscale * next_ref[:, cols]).astype(u_ref.dtype)

    @pl.when(i % 2 == 0)
    def _():
        step(acc_a, acc_b)

    @pl.when(i % 2 == 1)
    def _():
        step(acc_b, acc_a)


def _rows(x, w, res, post_w, next_w, res_scale, tm=256, tn=512):
    m, k = x.shape
    d = w.shape[-1]
    n_tiles = m // tm
    emit_next = next_w is not None
    lag_spec = pl.BlockSpec((tm, d), lambda i: (jnp.maximum(i - 1, 0), 0))
    vec_spec = pl.BlockSpec((1, d), lambda i: (0, 0))
    in_specs = [pl.BlockSpec((tm, k), lambda i: (jnp.minimum(i, n_tiles - 1), 0)),
                pl.BlockSpec((k, d), lambda i: (0, 0), pipeline_mode=pl.Buffered(1)),
                lag_spec, vec_spec]
    args = [x, w, res, post_w]
    out_shape = [jax.ShapeDtypeStruct((m, d), F32)]
    out_specs = [lag_spec]
    if emit_next:
        in_specs.append(vec_spec)
        args.append(next_w)
        out_shape.append(jax.ShapeDtypeStruct((m, d), BF16))
        out_specs.append(lag_spec)
    outs = pl.pallas_call(
        functools.partial(_rows_kernel, tn=tn, res_scale=res_scale, emit_next=emit_next),
        out_shape=out_shape,
        grid=(n_tiles + 1,),
        in_specs=in_specs,
        out_specs=out_specs,
        scratch_shapes=[pltpu.VMEM((tm, d), F32), pltpu.VMEM((tm, d), F32)],
        compiler_params=_cparams(1),
        name="rows_matmul_norm",
    )(*args)
    return outs if emit_next else (outs[0], None)


def _split_bf16(x):
    hi = x.astype(BF16)
    lo = (x - hi.astype(F32)).astype(BF16)
    return hi, lo


def _dot_nt(a, b):
    return lax.dot_general(a, b, (((1,), (1,)), ((), ())), preferred_element_type=F32)


def _dot_tn(a, b):
    return lax.dot_general(a, b, (((0,), (0,)), ((), ())), preferred_element_type=F32)


N_SUB = CHUNK // SUB


def _chunk_consts():
    i = lax.broadcasted_iota(jnp.int32, (CHUNK, CHUNK), 0)
    j = lax.broadcasted_iota(jnp.int32, (CHUNK, CHUNK), 1)
    lo = (i // SUB) * SUB
    hi = lo + SUB
    groups = [(j >= lo) & (j <= i), (j > i) & (j < hi), j < lo, j >= hi]
    sel = jnp.concatenate([jnp.where(g, 1.0, 0.0) for g in groups], axis=0).astype(BF16)
    return jnp.concatenate([sel, sel], axis=1), i >= j


def _decay_sums(sel2, g):
    g_hi, g_lo = _split_bf16(g * LOG2_E)
    return jnp.dot(sel2, jnp.concatenate([g_hi, g_lo], axis=0), preferred_element_type=F32)


def _gated_chunks(heads, causal):
    prep = [_chunk_operands(*h) for h in heads]
    scores = [_dot_nt(p[0], p[1]) for p in prep]
    kv = [jnp.dot(p[4], p[3], preferred_element_type=F32) for p in prep]
    outs, states = [], []
    for (q_t, k_t, q_st, k_st, v_t, st_bf, st_decay), s, upd, h in zip(prep, scores, kv, heads):
        s = jnp.where(causal, s, 0.0).astype(BF16)
        outs.append(_dot_nt(jnp.concatenate([q_st, s], axis=1),
                            jnp.concatenate([st_bf, v_t], axis=1)))
        states.append(h[4] * st_decay + upd)
    return outs, states


def _chunk_operands(q, k, v, dec, st):
    within, rest, before, after = dec
    dk = q.shape[1]
    cum = within + before
    q_st = (q * jnp.exp2(cum)).astype(BF16)
    k_st = (k * jnp.exp2(rest + after)).astype(BF16)
    q_in = (q * jnp.exp2(within)).astype(BF16)
    k_diag = (k * jnp.exp2(jnp.minimum(-within, EXP2_CLAMP))).astype(BF16)
    k_end = k * jnp.exp2(rest)
    k_end_bf = k_end.astype(BF16)

    blk = lambda x, b: x[b * SUB:(b + 1) * SUB]
    whole = {b: jnp.exp2(within[(b + 1) * SUB - 1:(b + 1) * SUB]) for b in range(1, N_SUB - 1)}
    skip = {}
    for bq in range(N_SUB):
        for bk in range(bq - 1):
            d = whole[bk + 1]
            for mid in range(bk + 2, bq):
                d = d * whole[mid]
            skip[bq, bk] = d
    zero = jnp.zeros((SUB, dk), BF16)
    k_cols, q_cols = [], []
    for bq in range(N_SUB):
        col = []
        for bk in range(N_SUB):
            if bk > bq:
                col.append(zero)
            elif bk == bq:
                col.append(blk(k_diag, bk))
            elif bk == bq - 1:
                col.append(blk(k_end_bf, bk))
            else:
                col.append((blk(k_end, bk) * skip[bq, bk]).astype(BF16))
        k_cols.append(jnp.concatenate(col, axis=0))
        q_cols.append(jnp.concatenate([blk(q_in, b) if b == bq else zero for b in range(N_SUB)],
                                      axis=0))
    q_tilde = jnp.concatenate(q_cols, axis=1)
    k_tilde = jnp.concatenate(k_cols, axis=1)
    v_t = v.T
    return q_tilde, k_tilde, q_st, k_st, v_t, st.astype(BF16), jnp.exp2(cum[CHUNK - 1:CHUNK])


def _head_norm_gate(o, norm_w, gate):
    return (o * _rms_scale(o) * norm_w * _silu(gate.astype(F32))).astype(BF16)


def _gla_kernel(q_ref, k_ref, v_ref, go_ref, code_ref, wgk_ref, bgk_ref, gn_ref, o_ref, st_ref,
                sums_ref, *, n_chunks, dk, dv):
    @pl.when(pl.program_id(1) == 0)
    def _():
        st_ref[...] = jnp.zeros_like(st_ref)

    sel2, causal = _chunk_consts()
    norm_w = gn_ref[...]
    q_scale = dk ** -0.5

    w_hi, w_lo = _split_bf16(wgk_ref[...])
    c_hi, c_lo = _split_bf16(code_ref[...])
    z = (jnp.dot(c_hi, w_hi, preferred_element_type=F32)
         + jnp.dot(c_lo, w_hi, preferred_element_type=F32)
         + jnp.dot(c_hi, w_lo, preferred_element_type=F32)) + bgk_ref[...]
    log_a = (jnp.minimum(z, 0.0) - jnp.log1p(jnp.exp(-jnp.abs(z)))) * (1.0 / GLA_GATE_NORMALIZER)
    for c in range(n_chunks):
        sums_ref[c] = _decay_sums(sel2, log_a[c * CHUNK:(c + 1) * CHUNK])

    def chunk_body(c, carry):
        rows = pl.ds(pl.multiple_of(c * CHUNK, CHUNK), CHUNK)
        heads = []
        for h in range(GLA_HEADS):
            kc = slice(h * dk, (h + 1) * dk)
            q = q_ref[rows, kc].astype(F32) * q_scale
            k = k_ref[rows, kc].astype(F32)
            dec = [sums_ref[c, n * CHUNK:(n + 1) * CHUNK, kc] for n in range(4)]
            heads.append((q, k, v_ref[rows, h * dv:(h + 1) * dv], dec, st_ref[h]))
        outs, states = _gated_chunks(heads, causal)
        for h in range(GLA_HEADS):
            vc = slice(h * dv, (h + 1) * dv)
            st_ref[h] = states[h]
            o_ref[rows, vc] = _head_norm_gate(outs[h], norm_w, go_ref[rows, vc])
        return carry

    lax.fori_loop(0, n_chunks, chunk_body, 0)


def _gla(p_a, p_code, wgk_pad, b_gk, gla_norm, batch, seq, t_blk=512):
    kw = wgk_pad.shape[1]
    vw = (p_a.shape[1] - 2 * kw) // 2
    dk = kw // GLA_HEADS
    dv = vw // GLA_HEADS
    nt = seq // t_blk
    row = lambda b, t: b * nt + t
    return pl.pallas_call(
        functools.partial(_gla_kernel, n_chunks=t_blk // CHUNK, dk=dk, dv=dv),
        out_shape=jax.ShapeDtypeStruct((batch * seq, vw), BF16),
        grid=(batch, nt),
        in_specs=[pl.BlockSpec((t_blk, kw), lambda b, t: (row(b, t), 0)),
                  pl.BlockSpec((t_blk, kw), lambda b, t: (row(b, t), 1)),
                  pl.BlockSpec((t_blk, vw), lambda b, t: (row(b, t), 1)),
                  pl.BlockSpec((t_blk, vw), lambda b, t: (row(b, t), 2)),
                  pl.BlockSpec((t_blk, LANES), lambda b, t: (row(b, t), 0)),
                  pl.BlockSpec((LANES, kw), lambda b, t: (0, 0)),
                  pl.BlockSpec((1, kw), lambda b, t: (0, 0)),
                  pl.BlockSpec((1, dv), lambda b, t: (0, 0))],
        out_specs=pl.BlockSpec((t_blk, vw), lambda b, t: (row(b, t), 0)),
        scratch_shapes=[pltpu.VMEM((GLA_HEADS, dv, dk), F32),
                        pltpu.VMEM((t_blk // CHUNK, 4 * CHUNK, kw), F32)],
        compiler_params=_cparams(2),
        name="gla_mixer",
    )(p_a, p_a, p_a, p_a, p_code, wgk_pad, b_gk, gla_norm)


def _hgrn_kernel(hq_ref, hi_ref, ho_ref, hf_ref, lbl_ref, hn_ref, o_ref, st_ref,
                 *, n_chunks, n_heads, dk, layer):
    @pl.when(pl.program_id(1) == 0)
    def _():
        st_ref[...] = jnp.zeros_like(st_ref)

    sel2, causal = _chunk_consts()
    logits = lbl_ref[...]
    p = jnp.exp(logits - jnp.max(logits, axis=0, keepdims=True))
    p = p / jnp.sum(p, axis=0, keepdims=True)
    lb = jnp.sum(p[:layer + 1], axis=0, keepdims=True)
    one_m_lb = jnp.sum(p[layer + 1:], axis=0, keepdims=True)
    norm_w = hn_ref[...]

    def chunk_body(c, carry):
        rows = pl.ds(pl.multiple_of(c * CHUNK, CHUNK), CHUNK)
        hf = hf_ref[rows, :]
        e = jnp.exp(-jnp.abs(hf))
        inv = 1.0 / (1.0 + e)
        pos = hf >= 0.0
        sig = jnp.where(pos, inv, e * inv)
        sig_neg = jnp.where(pos, e * inv, inv)
        log_f = jnp.log(lb + one_m_lb * sig)
        k_all = one_m_lb * sig_neg
        sums = _decay_sums(sel2, log_f)
        heads = []
        for h in range(n_heads):
            hc = slice(h * dk, (h + 1) * dk)
            q = _silu(hq_ref[rows, hc].astype(F32))
            dec = [sums[n * CHUNK:(n + 1) * CHUNK, hc] for n in range(4)]
            heads.append((q, k_all[:, hc], hi_ref[rows, hc], dec, st_ref[h]))
        outs, states = _gated_chunks(heads, causal)
        for h in range(n_heads):
            hc = slice(h * dk, (h + 1) * dk)
            st_ref[h] = states[h]
            o_ref[rows, hc] = _head_norm_gate(outs[h], norm_w, ho_ref[rows, hc])
        return carry

    lax.fori_loop(0, n_chunks, chunk_body, 0)


def _hgrn(p_q, p_r, p_f, lb_logits, hgrn_norm, layer, batch, seq, t_blk=512):
    w = p_f.shape[1]
    dk = HGRN_EXPAND
    n_heads = w // dk
    nt = seq // t_blk
    n_lb = lb_logits.shape[0]
    row = lambda b, t: b * nt + t
    return pl.pallas_call(
        functools.partial(_hgrn_kernel, n_chunks=t_blk // CHUNK, n_heads=n_heads, dk=dk,
                          layer=layer),
        out_shape=jax.ShapeDtypeStruct((batch * seq, w), BF16),
        grid=(batch, nt),
        in_specs=[pl.BlockSpec((t_blk, w), lambda b, t: (row(b, t), 0)),
                  pl.BlockSpec((t_blk, w), lambda b, t: (row(b, t), 0)),
                  pl.BlockSpec((t_blk, w), lambda b, t: (row(b, t), 1)),
                  pl.BlockSpec((t_blk, w), lambda b, t: (row(b, t), 0)),
                  pl.BlockSpec((n_lb, w), lambda b, t: (0, 0)),
                  pl.BlockSpec((1, dk), lambda b, t: (0, 0))],
        out_specs=pl.BlockSpec((t_blk, w), lambda b, t: (row(b, t), 0)),
        scratch_shapes=[pltpu.VMEM((n_heads, dk, dk), F32)],
        compiler_params=_cparams(2),
        name="hgrn_mixer",
    )(p_q, p_r, p_r, p_f, lb_logits, hgrn_norm)


def _merge_kernel(og_ref, oh_ref, zg_ref, zh_ref, wg_ref, wh_ref, bg_ref, bh_ref, wo_ref,
                  o_ref, wo_bf_ref, wg_bf, wh_bf):
    @pl.when(pl.program_id(1) == 0)
    def _():
        wg_bf[...] = wg_ref[...].astype(BF16)
        wh_bf[...] = wh_ref[...].astype(BF16)

    wo_bf_ref[...] = wo_ref[...].astype(BF16)
    a = jnp.dot(og_ref[...], wg_bf[...], preferred_element_type=F32)
    b = jnp.dot(oh_ref[...], wh_bf[...], preferred_element_type=F32)
    o_ref[...] = (_sigmoid(zg_ref[...].astype(F32) + bg_ref[...]) * a
                  + _sigmoid(zh_ref[...].astype(F32) + bh_ref[...]) * b).astype(o_ref.dtype)


def _merge(o_gla, o_hgrn, p_b, zg_col0, zh_col0, w_bg, w_bh, b_gates, w_out, tm=1024, tn=1024):
    m, kdim = o_gla.shape
    d = w_bg.shape[-1]
    ni = m // tm
    k_out, d_out = w_out.shape[1:]
    rb = k_out // ((d // tn) * ni)
    assert rb * (d // tn) * ni == k_out and rb % (2 * SUBLANES) == 0
    x_spec = pl.BlockSpec((tm, kdim), lambda j, i: (i, 0))
    w_spec = pl.BlockSpec((None, kdim, tn), lambda j, i: (0, 0, j))
    return pl.pallas_call(
        _merge_kernel,
        out_shape=[jax.ShapeDtypeStruct((m, d), BF16), jax.ShapeDtypeStruct((k_out, d_out), BF16)],
        grid=(d // tn, ni),
        in_specs=[x_spec, x_spec,
                  pl.BlockSpec((tm, tn), lambda j, i: (i, zg_col0 // tn + j)),
                  pl.BlockSpec((tm, tn), lambda j, i: (i, zh_col0 // tn + j)),
                  w_spec, w_spec,
                  pl.BlockSpec((None, 1, tn), lambda j, i: (0, 0, j)),
                  pl.BlockSpec((None, 1, tn), lambda j, i: (1, 0, j)),
                  pl.BlockSpec((None, rb, d_out), lambda j, i: (0, j * ni + i, 0))],
        out_specs=[pl.BlockSpec((tm, tn), lambda j, i: (i, j)),
                   pl.BlockSpec((rb, d_out), lambda j, i: (j * ni + i, 0))],
        scratch_shapes=[pltpu.VMEM((kdim, tn), BF16), pltpu.VMEM((kdim, tn), BF16)],
        compiler_params=_cparams(2),
        name="branch_merge",
    )(o_gla, o_hgrn, p_b, p_b, w_bg, w_bh, b_gates, b_gates, w_out)


def kernel(x, ffn1_pre_norm, ffn1_w_gate, ffn1_w_up, ffn1_w_down, ffn1_post_norm, mix_pre_norm, w_in, gla_w_gk_up, gla_b_gk, gla_norm, hgrn_lb_logits, hgrn_norm, w_branch_gla, w_branch_hgrn, b_branch_gates, w_out, mix_post_norm, ffn2_pre_norm, ffn2_w_gate, ffn2_w_up, ffn2_w_down, ffn2_post_norm):
    batch, seq, d_model = x.shape
    depth = ffn1_w_gate.shape[0]
    m = batch * seq
    kw = gla_w_gk_up.shape[-1]
    vw = d_model // 2
    a_cols = 2 * kw + 2 * vw
    code0 = a_cols
    hq0 = code0 + GLA_GATE_RANK
    hf0, hi0 = hq0 + vw, hq0 + 2 * vw

    h = x.reshape(m, d_model)
    u = _rmsnorm(h, ffn1_pre_norm[0:1])
    for l in range(depth):
        mid, w_down = _gateup(u, ffn1_w_gate[l:l + 1], ffn1_w_up[l:l + 1], ffn1_w_down[l:l + 1])
        h, u = _rows(mid, w_down, h, ffn1_post_norm[l:l + 1],
                     mix_pre_norm[l:l + 1], 0.5)

        w_l = jnp.swapaxes(w_in[l:l + 1], 1, 2)
        p_a = _proj(u, w_l, 0, a_cols, BF16)
        p_code = _proj(u, w_l, code0, LANES, F32, tn=LANES)
        p_q = _proj(u, w_l, hq0, vw, BF16)
        p_f = _proj(u, w_l, hf0, vw, F32)
        p_r = _proj(u, w_l, hi0, 2 * vw + 2 * d_model, BF16)
        wgk_pad = jnp.pad(gla_w_gk_up[l], ((0, LANES - GLA_GATE_RANK), (0, 0)))
        o_gla = _gla(p_a, p_code, wgk_pad, gla_b_gk[l:l + 1], gla_norm[l:l + 1], batch, seq)
        o_hgrn = _hgrn(p_q, p_r, p_f, hgrn_lb_logits, hgrn_norm[l:l + 1], l, batch, seq)
        merged, w_out_bf = _merge(o_gla, o_hgrn, p_r, 2 * vw, 2 * vw + d_model,
                                  w_branch_gla[l:l + 1], w_branch_hgrn[l:l + 1],
                                  b_branch_gates[l].reshape(2, 1, d_model), w_out[l:l + 1])
        h, u = _rows(merged, w_out_bf, h, mix_post_norm[l:l + 1],
                     ffn2_pre_norm[l:l + 1], 1.0, tm=512)

        mid, w_down = _gateup(u, ffn2_w_gate[l:l + 1], ffn2_w_up[l:l + 1], ffn2_w_down[l:l + 1])
        next_norm = ffn1_pre_norm[l + 1:l + 2] if l + 1 < depth else None
        h, u = _rows(mid, w_down, h, ffn2_post_norm[l:l + 1], next_norm, 0.5)
    return h.reshape(batch, seq, d_model)
```

```python
import functools

import jax
import jax.numpy as jnp
from jax import lax
from jax.experimental import pallas as pl
from jax.experimental.pallas import tpu as pltpu

F32 = jnp.float32
BF16 = jnp.bfloat16

EPS = 1e-6
CHUNK = 64
SUB = 16
GLA_HEADS = 4
GLA_GATE_RANK = 16
GLA_GATE_NORMALIZER = 16.0
HGRN_EXPAND = 128
LOG2_E = 1.4426950408889634
EXP2_CLAMP = 115.0

LANES = 128
SUBLANES = 8
VMEM_LIMIT = 62 * 1024 * 1024
MATMUL_SUB_ROWS = 1024


def _cparams(n_axes):
    return pltpu.CompilerParams(
        dimension_semantics=("arbitrary",) * n_axes, vmem_limit_bytes=VMEM_LIMIT)


def _sigmoid(x):
    return 1.0 / (1.0 + jnp.exp2(x * (-LOG2_E)))


def _silu(x):
    return x * _sigmoid(x)


def _rms_scale(x):
    return lax.rsqrt(jnp.mean(x * x, axis=-1, keepdims=True) + EPS)


def _rmsnorm_kernel(x_ref, w_ref, o_ref):
    x = x_ref[...]
    o_ref[...] = (x * _rms_scale(x) * w_ref[...]).astype(o_ref.dtype)


def _rmsnorm(x, w, tm=512):
    m, d = x.shape
    return pl.pallas_call(
        _rmsnorm_kernel,
        out_shape=jax.ShapeDtypeStruct((m, d), BF16),
        grid=(m // tm,),
        in_specs=[pl.BlockSpec((tm, d), lambda i: (i, 0)),
                  pl.BlockSpec((1, d), lambda i: (0, 0))],
        out_specs=pl.BlockSpec((tm, d), lambda i: (i, 0)),
        compiler_params=_cparams(1),
        name="rmsnorm",
    )(x, w)


def _gateup_kernel(u_ref, wg_ref, wu_ref, wd_ref, o_ref, wd_bf_ref, wg_bf, wu_bf):
    @pl.when(pl.program_id(1) == 0)
    def _():
        wg_bf[...] = wg_ref[...].astype(BF16)
        wu_bf[...] = wu_ref[...].astype(BF16)
        wd_bf_ref[...] = wd_ref[...].astype(BF16)

    for r in range(0, u_ref.shape[0], MATMUL_SUB_ROWS):
        rows = slice(r, r + MATMUL_SUB_ROWS)
        u = u_ref[rows, :]
        g = jnp.dot(u, wg_bf[...], preferred_element_type=F32)
        up = jnp.dot(u, wu_bf[...], preferred_element_type=F32)
        o_ref[rows, :] = (_silu(g) * up).astype(o_ref.dtype)


def _gateup(u, w_gate, w_up, w_down, tm=2048, tn=512):
    m, d = u.shape
    f = w_gate.shape[-1]
    w_spec = pl.BlockSpec((None, d, tn), lambda j, i: (0, 0, j))
    return pl.pallas_call(
        _gateup_kernel,
        out_shape=[jax.ShapeDtypeStruct((m, f), BF16), jax.ShapeDtypeStruct((f, d), BF16)],
        grid=(pl.cdiv(f, tn), m // tm),
        in_specs=[pl.BlockSpec((tm, d), lambda j, i: (i, 0)), w_spec, w_spec,
                  pl.BlockSpec((None, tn, d), lambda j, i: (0, j, 0))],
        out_specs=[pl.BlockSpec((tm, tn), lambda j, i: (i, j)),
                   pl.BlockSpec((tn, d), lambda j, i: (j, 0))],
        scratch_shapes=[pltpu.VMEM((d, tn), BF16), pltpu.VMEM((d, tn), BF16)],
        compiler_params=_cparams(2),
        name="ffn_gateup",
    )(u, w_gate, w_up, w_down)


def _proj_kernel(u_ref, w_ref, *rest, shift, tn):
    if shift:
        wn_ref, o_ref, w_bf = rest
    else:
        o_ref, w_bf = rest

    @pl.when(pl.program_id(1) == 0)
    def _():
        w = w_ref[...]
        if shift:
            w = jnp.concatenate([w, wn_ref[...]], axis=0)[shift:shift + tn]
        w_bf[...] = w.astype(BF16)

    sub = min(u_ref.shape[0], MATMUL_SUB_ROWS)
    for r in range(0, u_ref.shape[0], sub):
        o_ref[r:r + sub, :] = _dot_nt(u_ref[r:r + sub, :], w_bf[...]).astype(o_ref.dtype)


def _proj(u, wt, col0, n_cols, out_dtype, tn=1024):
    m, d = u.shape
    tm = 2048 if n_cols >= 3 * tn else 1024
    shift = col0 % tn
    blk0 = col0 // tn
    assert n_cols % tn == 0 and shift % SUBLANES == 0
    in_specs = [pl.BlockSpec((tm, d), lambda j, i: (i, 0)),
                pl.BlockSpec((None, tn, d), lambda j, i: (0, j + blk0, 0))]
    args = [u, wt]
    if shift:
        assert tn % shift == 0
        per_tile = tn // shift
        in_specs.append(pl.BlockSpec((None, shift, d), lambda j, i: (0, (j + blk0 + 1) * per_tile, 0)))
        args.append(wt)
    return pl.pallas_call(
        functools.partial(_proj_kernel, shift=shift, tn=tn),
        out_shape=jax.ShapeDtypeStruct((m, n_cols), out_dtype),
        grid=(n_cols // tn, m // tm),
        in_specs=in_specs,
        out_specs=pl.BlockSpec((tm, tn), lambda j, i: (i, j)),
        scratch_shapes=[pltpu.VMEM((tn, d), BF16)],
        compiler_params=_cparams(2),
        name="in_proj",
    )(*args)


def _rows_kernel(x_ref, w_ref, res_ref, post_ref, *rest, tn, res_scale, emit_next):
    if emit_next:
        next_ref, h_ref, u_ref, acc_a, acc_b = rest
    else:
        h_ref, acc_a, acc_b = rest
    i = pl.program_id(0)
    d = h_ref.shape[1]
    col_tiles = [slice(c, c + tn) for c in range(0, d, tn)]

    @pl.when(i == 0)
    def _():
        acc_b[...] = jnp.zeros_like(acc_b)

    def step(acc_cur, acc_prev):
        x = x_ref[...]
        for cols in col_tiles:
            acc_cur[:, cols] = jnp.dot(x, w_ref[:, cols], preferred_element_type=F32)

        ssq = None
        for cols in col_tiles:
            a = acc_prev[:, cols]
            s = jnp.sum(a * a, axis=-1, keepdims=True)
            ssq = s if ssq is None else ssq + s
        scale = lax.rsqrt(ssq / d + EPS) * res_scale
        hsq = None
        for cols in col_tiles:
            h = res_ref[:, cols] + acc_prev[:, cols] * scale * post_ref[:, cols]
            h_ref[:, cols] = h
            if emit_next:
                s = jnp.sum(h * h, axis=-1, keepdims=True)
                hsq = s if hsq is None else hsq + s
        if emit_next:
            nscale = lax.rsqrt(hsq / d + EPS)
            for cols in col_tiles:
                u_ref[:, cols] = (h_ref[:, cols] * nscale * next_ref[:, cols]).astype(u_ref.dtype)

    @pl.when(i % 2 == 0)
    def _():
        step(acc_a, acc_b)

    @pl.when(i % 2 == 1)
    def _():
        step(acc_b, acc_a)


def _rows(x, w, res, post_w, next_w, res_scale, tm=256, tn=512):
    m, k = x.shape
    d = w.shape[-1]
    n_tiles = m // tm
    emit_next = next_w is not None
    lag_spec = pl.BlockSpec((tm, d), lambda i: (jnp.maximum(i - 1, 0), 0))
    vec_spec = pl.BlockSpec((1, d), lambda i: (0, 0))
    in_specs = [pl.BlockSpec((tm, k), lambda i: (jnp.minimum(i, n_tiles - 1), 0)),
                pl.BlockSpec((k, d), lambda i: (0, 0), pipeline_mode=pl.Buffered(1)),
                lag_spec, vec_spec]
    args = [x, w, res, post_w]
    out_shape = [jax.ShapeDtypeStruct((m, d), F32)]
    out_specs = [lag_spec]
    if emit_next:
        in_specs.append(vec_spec)
        args.append(next_w)
        out_shape.append(jax.ShapeDtypeStruct((m, d), BF16))
        out_specs.append(lag_spec)
    outs = pl.pallas_call(
        functools.partial(_rows_kernel, tn=tn, res_scale=res_scale, emit_next=emit_next),
        out_shape=out_shape,
        grid=(n_tiles + 1,),
        in_specs=in_specs,
        out_specs=out_specs,
        scratch_shapes=[pltpu.VMEM((tm, d), F32), pltpu.VMEM((tm, d), F32)],
        compiler_params=_cparams(1),
        name="rows_matmul_norm",
    )(*args)
    return outs if emit_next else (outs[0], None)


def _split_bf16(x):
    hi = x.astype(BF16)
    lo = (x - hi.astype(F32)).astype(BF16)
    return hi, lo


def _dot_nt(a, b):
    return lax.dot_general(a, b, (((1,), (1,)), ((), ())), preferred_element_type=F32)


def _dot_tn(a, b):
    return lax.dot_general(a, b, (((0,), (0,)), ((), ())), preferred_element_type=F32)


N_SUB = CHUNK // SUB


def _chunk_consts():
    i = lax.broadcasted_iota(jnp.int32, (CHUNK, CHUNK), 0)
    j = lax.broadcasted_iota(jnp.int32, (CHUNK, CHUNK), 1)
    lo = (i // SUB) * SUB
    hi = lo + SUB
    groups = [(j >= lo) & (j <= i), (j > i) & (j < hi), j < lo, j >= hi]
    sel = jnp.concatenate([jnp.where(g, 1.0, 0.0) for g in groups], axis=0).astype(BF16)
    return jnp.concatenate([sel, sel], axis=1), i >= j


def _decay_sums(sel2, g):
    g_hi, g_lo = _split_bf16(g)
    return jnp.dot(sel2, jnp.concatenate([g_hi, g_lo], axis=0), preferred_element_type=F32)


def _gated_chunks(heads, causal):
    prep = [_chunk_operands(*h) for h in heads]
    scores = [_dot_nt(p[0], p[1]) for p in prep]
    kv = [jnp.dot(p[4], p[3], preferred_element_type=F32) for p in prep]
    outs, states = [], []
    for (q_t, k_t, q_st, k_st, v_t, st_bf, st_decay), s, upd, h in zip(prep, scores, kv, heads):
        s = jnp.where(causal, s, 0.0).astype(BF16)
        outs.append(_dot_nt(jnp.concatenate([q_st, s], axis=1),
                            jnp.concatenate([st_bf, v_t], axis=1)))
        states.append(h[4] * st_decay + upd)
    return outs, states


def _chunk_operands(q, k, v, dec, st):
    within, rest, before, after = dec
    dk = q.shape[1]
    cum = within + before
    q_st = (q * jnp.exp2(cum)).astype(BF16)
    k_st = (k * jnp.exp2(rest + after)).astype(BF16)
    q_in = (q * jnp.exp2(within)).astype(BF16)
    k_diag = (k * jnp.exp2(jnp.minimum(-within, EXP2_CLAMP))).astype(BF16)
    k_end = k * jnp.exp2(rest)
    k_end_bf = k_end.astype(BF16)

    blk = lambda x, b: x[b * SUB:(b + 1) * SUB]
    whole = {b: jnp.exp2(within[(b + 1) * SUB - 1:(b + 1) * SUB]) for b in range(1, N_SUB - 1)}
    skip = {}
    for bq in range(N_SUB):
        for bk in range(bq - 1):
            d = whole[bk + 1]
            for mid in range(bk + 2, bq):
                d = d * whole[mid]
            skip[bq, bk] = d
    zero = jnp.zeros((SUB, dk), BF16)
    k_cols, q_cols = [], []
    for bq in range(N_SUB):
        col = []
        for bk in range(N_SUB):
            if bk > bq:
                col.append(zero)
            elif bk == bq:
                col.append(blk(k_diag, bk))
            elif bk == bq - 1:
                col.append(blk(k_end_bf, bk))
            else:
                col.append((blk(k_end, bk) * skip[bq, bk]).astype(BF16))
        k_cols.append(jnp.concatenate(col, axis=0))
        q_cols.append(jnp.concatenate([blk(q_in, b) if b == bq else zero for b in range(N_SUB)],
                                      axis=0))
    q_tilde = jnp.concatenate(q_cols, axis=1)
    k_tilde = jnp.concatenate(k_cols, axis=1)
    v_t = v.T
    return q_tilde, k_tilde, q_st, k_st, v_t, st.astype(BF16), jnp.exp2(cum[CHUNK - 1:CHUNK])


def _head_norm_gate(o, norm_w, gate):
    return (o * _rms_scale(o) * norm_w * _silu(gate.astype(F32))).astype(BF16)


def _gla_kernel(q_ref, k_ref, v_ref, go_ref, code_ref, wgk_ref, bgk_ref, gn_ref, o_ref, st_ref,
                sums_ref, *, n_chunks, dk, dv):
    @pl.when(pl.program_id(1) == 0)
    def _():
        st_ref[...] = jnp.zeros_like(st_ref)

    sel2, causal = _chunk_consts()
    norm_w = gn_ref[...]
    q_scale = dk ** -0.5

    w_hi, w_lo = _split_bf16(wgk_ref[...])
    c_hi, c_lo = _split_bf16(code_ref[...])
    z = (jnp.dot(c_hi, w_hi, preferred_element_type=F32)
         + jnp.dot(c_lo, w_hi, preferred_element_type=F32)
         + jnp.dot(c_hi, w_lo, preferred_element_type=F32)) + bgk_ref[...]
    log2_a = ((jnp.minimum(z, 0.0) - jnp.log1p(jnp.exp2(jnp.abs(z) * (-LOG2_E))))
              * (LOG2_E / GLA_GATE_NORMALIZER))
    for c in range(n_chunks):
        sums_ref[c] = _decay_sums(sel2, log2_a[c * CHUNK:(c + 1) * CHUNK])

    def chunk_body(c, carry):
        rows = pl.ds(pl.multiple_of(c * CHUNK, CHUNK), CHUNK)
        heads = []
        for h in range(GLA_HEADS):
            kc = slice(h * dk, (h + 1) * dk)
            q = q_ref[rows, kc].astype(F32) * q_scale
            k = k_ref[rows, kc].astype(F32)
            dec = [sums_ref[c, n * CHUNK:(n + 1) * CHUNK, kc] for n in range(4)]
            heads.append((q, k, v_ref[rows, h * dv:(h + 1) * dv], dec, st_ref[h]))
        outs, states = _gated_chunks(heads, causal)
        for h in range(GLA_HEADS):
            vc = slice(h * dv, (h + 1) * dv)
            st_ref[h] = states[h]
            o_ref[rows, vc] = _head_norm_gate(outs[h], norm_w, go_ref[rows, vc])
        return carry

    lax.fori_loop(0, n_chunks, chunk_body, 0)


def _gla(p_a, p_code, wgk_pad, b_gk, gla_norm, batch, seq, t_blk=512):
    kw = wgk_pad.shape[1]
    vw = (p_a.shape[1] - 2 * kw) // 2
    dk = kw // GLA_HEADS
    dv = vw // GLA_HEADS
    nt = seq // t_blk
    row = lambda b, t: b * nt + t
    return pl.pallas_call(
        functools.partial(_gla_kernel, n_chunks=t_blk // CHUNK, dk=dk, dv=dv),
        out_shape=jax.ShapeDtypeStruct((batch * seq, vw), BF16),
        grid=(batch, nt),
        in_specs=[pl.BlockSpec((t_blk, kw), lambda b, t: (row(b, t), 0)),
                  pl.BlockSpec((t_blk, kw), lambda b, t: (row(b, t), 1)),
                  pl.BlockSpec((t_blk, vw), lambda b, t: (row(b, t), 1)),
                  pl.BlockSpec((t_blk, vw), lambda b, t: (row(b, t), 2)),
                  pl.BlockSpec((t_blk, LANES), lambda b, t: (row(b, t), 0)),
                  pl.BlockSpec((LANES, kw), lambda b, t: (0, 0)),
                  pl.BlockSpec((1, kw), lambda b, t: (0, 0)),
                  pl.BlockSpec((1, dv), lambda b, t: (0, 0))],
        out_specs=pl.BlockSpec((t_blk, vw), lambda b, t: (row(b, t), 0)),
        scratch_shapes=[pltpu.VMEM((GLA_HEADS, dv, dk), F32),
                        pltpu.VMEM((t_blk // CHUNK, 4 * CHUNK, kw), F32)],
        compiler_params=_cparams(2),
        name="gla_mixer",
    )(p_a, p_a, p_a, p_a, p_code, wgk_pad, b_gk, gla_norm)


def _hgrn_kernel(hq_ref, hi_ref, ho_ref, hf_ref, lbl_ref, hn_ref, o_ref, st_ref,
                 *, n_chunks, n_heads, dk, layer):
    @pl.when(pl.program_id(1) == 0)
    def _():
        st_ref[...] = jnp.zeros_like(st_ref)

    sel2, causal = _chunk_consts()
    logits = lbl_ref[...]
    p = jnp.exp(logits - jnp.max(logits, axis=0, keepdims=True))
    p = p / jnp.sum(p, axis=0, keepdims=True)
    lb = jnp.sum(p[:layer + 1], axis=0, keepdims=True)
    one_m_lb = jnp.sum(p[layer + 1:], axis=0, keepdims=True)
    norm_w = hn_ref[...]

    def chunk_body(c, carry):
        rows = pl.ds(pl.multiple_of(c * CHUNK, CHUNK), CHUNK)
        hf = hf_ref[rows, :]
        e = jnp.exp2(jnp.abs(hf) * (-LOG2_E))
        inv = 1.0 / (1.0 + e)
        pos = hf >= 0.0
        sig = jnp.where(pos, inv, e * inv)
        sig_neg = jnp.where(pos, e * inv, inv)
        log2_f = jnp.log2(lb + one_m_lb * sig)
        k_all = one_m_lb * sig_neg
        sums = _decay_sums(sel2, log2_f)
        heads = []
        for h in range(n_heads):
            hc = slice(h * dk, (h + 1) * dk)
            q = _silu(hq_ref[rows, hc].astype(F32))
            dec = [sums[n * CHUNK:(n + 1) * CHUNK, hc] for n in range(4)]
            heads.append((q, k_all[:, hc], hi_ref[rows, hc], dec, st_ref[h]))
        outs, states = _gated_chunks(heads, causal)
        for h in range(n_heads):
            hc = slice(h * dk, (h + 1) * dk)
            st_ref[h] = states[h]
            o_ref[rows, hc] = _head_norm_gate(outs[h], norm_w, ho_ref[rows, hc])
        return carry

    lax.fori_loop(0, n_chunks, chunk_body, 0)


def _hgrn(p_q, p_r, p_f, lb_logits, hgrn_norm, layer, batch, seq, t_blk=512):
    w = p_f.shape[1]
    dk = HGRN_EXPAND
    n_heads = w // dk
    nt = seq // t_blk
    n_lb = lb_logits.shape[0]
    row = lambda b, t: b * nt + t
    return pl.pallas_call(
        functools.partial(_hgrn_kernel, n_chunks=t_blk // CHUNK, n_heads=n_heads, dk=dk,
                          layer=layer),
        out_shape=jax.ShapeDtypeStruct((batch * seq, w), BF16),
        grid=(batch, nt),
        in_specs=[pl.BlockSpec((t_blk, w), lambda b, t: (row(b, t), 0)),
                  pl.BlockSpec((t_blk, w), lambda b, t: (row(b, t), 0)),
                  pl.BlockSpec((t_blk, w), lambda b, t: (row(b, t), 1)),
                  pl.BlockSpec((t_blk, w), lambda b, t: (row(b, t), 0)),
                  pl.BlockSpec((n_lb, w), lambda b, t: (0, 0)),
                  pl.BlockSpec((1, dk), lambda b, t: (0, 0))],
        out_specs=pl.BlockSpec((t_blk, w), lambda b, t: (row(b, t), 0)),
        scratch_shapes=[pltpu.VMEM((n_heads, dk, dk), F32)],
        compiler_params=_cparams(2),
        name="hgrn_mixer",
    )(p_q, p_r, p_r, p_f, lb_logits, hgrn_norm)


def _merge_kernel(og_ref, oh_ref, zg_ref, zh_ref, wg_ref, wh_ref, bg_ref, bh_ref, wo_ref,
                  o_ref, wo_bf_ref, wg_bf, wh_bf):
    @pl.when(pl.program_id(1) == 0)
    def _():
        wg_bf[...] = wg_ref[...].astype(BF16)
        wh_bf[...] = wh_ref[...].astype(BF16)

    wo_bf_ref[...] = wo_ref[...].astype(BF16)
    a = jnp.dot(og_ref[...], wg_bf[...], preferred_element_type=F32)
    b = jnp.dot(oh_ref[...], wh_bf[...], preferred_element_type=F32)
    o_ref[...] = (_sigmoid(zg_ref[...].astype(F32) + bg_ref[...]) * a
                  + _sigmoid(zh_ref[...].astype(F32) + bh_ref[...]) * b).astype(o_ref.dtype)


def _merge(o_gla, o_hgrn, p_b, zg_col0, zh_col0, w_bg, w_bh, b_gates, w_out, tm=1024, tn=1024):
    m, kdim = o_gla.shape
    d = w_bg.shape[-1]
    ni = m // tm
    k_out, d_out = w_out.shape[1:]
    rb = k_out // ((d // tn) * ni)
    assert rb * (d // tn) * ni == k_out and rb % (2 * SUBLANES) == 0
    x_spec = pl.BlockSpec((tm, kdim), lambda j, i: (i, 0))
    w_spec = pl.BlockSpec((None, kdim, tn), lambda j, i: (0, 0, j))
    return pl.pallas_call(
        _merge_kernel,
        out_shape=[jax.ShapeDtypeStruct((m, d), BF16), jax.ShapeDtypeStruct((k_out, d_out), BF16)],
        grid=(d // tn, ni),
        in_specs=[x_spec, x_spec,
                  pl.BlockSpec((tm, tn), lambda j, i: (i, zg_col0 // tn + j)),
                  pl.BlockSpec((tm, tn), lambda j, i: (i, zh_col0 // tn + j)),
                  w_spec, w_spec,
                  pl.BlockSpec((None, 1, tn), lambda j, i: (0, 0, j)),
                  pl.BlockSpec((None, 1, tn), lambda j, i: (1, 0, j)),
                  pl.BlockSpec((None, rb, d_out), lambda j, i: (0, j * ni + i, 0))],
        out_specs=[pl.BlockSpec((tm, tn), lambda j, i: (i, j)),
                   pl.BlockSpec((rb, d_out), lambda j, i: (j * ni + i, 0))],
        scratch_shapes=[pltpu.VMEM((kdim, tn), BF16), pltpu.VMEM((kdim, tn), BF16)],
        compiler_params=_cparams(2),
        name="branch_merge",
    )(o_gla, o_hgrn, p_b, p_b, w_bg, w_bh, b_gates, b_gates, w_out)


def kernel(x, ffn1_pre_norm, ffn1_w_gate, ffn1_w_up, ffn1_w_down, ffn1_post_norm, mix_pre_norm, w_in, gla_w_gk_up, gla_b_gk, gla_norm, hgrn_lb_logits, hgrn_norm, w_branch_gla, w_branch_hgrn, b_branch_gates, w_out, mix_post_norm, ffn2_pre_norm, ffn2_w_gate, ffn2_w_up, ffn2_w_down, ffn2_post_norm):
    batch, seq, d_model = x.shape
    depth = ffn1_w_gate.shape[0]
    m = batch * seq
    kw = gla_w_gk_up.shape[-1]
    vw = d_model // 2
    a_cols = 2 * kw + 2 * vw
    code0 = a_cols
    hq0 = code0 + GLA_GATE_RANK
    hf0, hi0 = hq0 + vw, hq0 + 2 * vw

    h = x.reshape(m, d_model)
    u = _rmsnorm(h, ffn1_pre_norm[0:1])
    for l in range(depth):
        mid, w_down = _gateup(u, ffn1_w_gate[l:l + 1], ffn1_w_up[l:l + 1], ffn1_w_down[l:l + 1])
        h, u = _rows(mid, w_down, h, ffn1_post_norm[l:l + 1],
                     mix_pre_norm[l:l + 1], 0.5)

        w_l = jnp.swapaxes(w_in[l:l + 1], 1, 2)
        p_a = _proj(u, w_l, 0, a_cols, BF16)
        p_code = _proj(u, w_l, code0, LANES, F32, tn=LANES)
        p_q = _proj(u, w_l, hq0, vw, BF16)
        p_f = _proj(u, w_l, hf0, vw, F32)
        p_r = _proj(u, w_l, hi0, 2 * vw + 2 * d_model, BF16)
        wgk_pad = jnp.pad(gla_w_gk_up[l], ((0, LANES - GLA_GATE_RANK), (0, 0)))
        o_gla = _gla(p_a, p_code, wgk_pad, gla_b_gk[l:l + 1], gla_norm[l:l + 1], batch, seq)
        o_hgrn = _hgrn(p_q, p_r, p_f, hgrn_lb_logits, hgrn_norm[l:l + 1], l, batch, seq)
        merged, w_out_bf = _merge(o_gla, o_hgrn, p_r, 2 * vw, 2 * vw + d_model,
                                  w_branch_gla[l:l + 1], w_branch_hgrn[l:l + 1],
                                  b_branch_gates[l].reshape(2, 1, d_model), w_out[l:l + 1])
        h, u = _rows(merged, w_out_bf, h, mix_post_norm[l:l + 1],
                     ffn2_pre_norm[l:l + 1], 1.0, tm=512)

        mid, w_down = _gateup(u, ffn2_w_gate[l:l + 1], ffn2_w_up[l:l + 1], ffn2_w_down[l:l + 1])
        next_norm = ffn1_pre_norm[l + 1:l + 2] if l + 1 < depth else None
        h, u = _rows(mid, w_down, h, ffn2_post_norm[l:l + 1], next_norm, 0.5)
    return h.reshape(batch, seq, d_model)
```

```python
import functools

import jax
import jax.numpy as jnp
from jax import lax
from jax.experimental import pallas as pl
from jax.experimental.pallas import tpu as pltpu

F32 = jnp.float32
BF16 = jnp.bfloat16

EPS = 1e-6
CHUNK = 64
SUB = 16
GLA_HEADS = 4
GLA_GATE_RANK = 16
GLA_GATE_NORMALIZER = 16.0
HGRN_EXPAND = 128
LOG2_E = 1.4426950408889634
EXP2_CLAMP = 115.0

LANES = 128
SUBLANES = 8
VMEM_LIMIT = 62 * 1024 * 1024
MATMUL_SUB_ROWS = 1024


def _cparams(n_axes):
    return pltpu.CompilerParams(
        dimension_semantics=("arbitrary",) * n_axes, vmem_limit_bytes=VMEM_LIMIT)


def _sigmoid(x):
    return 1.0 / (1.0 + jnp.exp2(x * (-LOG2_E)))


def _silu(x):
    return x * _sigmoid(x)


def _rms_scale(x):
    return lax.rsqrt(jnp.mean(x * x, axis=-1, keepdims=True) + EPS)


def _rmsnorm_kernel(x_ref, w_ref, o_ref):
    x = x_ref[...]
    o_ref[...] = (x * _rms_scale(x) * w_ref[...]).astype(o_ref.dtype)


def _rmsnorm(x, w, tm=512):
    m, d = x.shape
    return pl.pallas_call(
        _rmsnorm_kernel,
        out_shape=jax.ShapeDtypeStruct((m, d), BF16),
        grid=(m // tm,),
        in_specs=[pl.BlockSpec((tm, d), lambda i: (i, 0)),
                  pl.BlockSpec((1, d), lambda i: (0, 0))],
        out_specs=pl.BlockSpec((tm, d), lambda i: (i, 0)),
        compiler_params=_cparams(1),
        name="rmsnorm",
    )(x, w)


def _gateup_kernel(u_ref, wg_ref, wu_ref, wd_ref, o_ref, wd_bf_ref, wg_bf, wu_bf):
    @pl.when(pl.program_id(1) == 0)
    def _():
        wg_bf[...] = wg_ref[...].astype(BF16)
        wu_bf[...] = wu_ref[...].astype(BF16)
        wd_bf_ref[...] = wd_ref[...].astype(BF16)

    for r in range(0, u_ref.shape[0], MATMUL_SUB_ROWS):
        rows = slice(r, r + MATMUL_SUB_ROWS)
        u = u_ref[rows, :]
        g = jnp.dot(u, wg_bf[...], preferred_element_type=F32)
        up = jnp.dot(u, wu_bf[...], preferred_element_type=F32)
        o_ref[rows, :] = (_silu(g) * up).astype(o_ref.dtype)


def _gateup(u, w_gate, w_up, w_down, tm=2048, tn=512):
    m, d = u.shape
    f = w_gate.shape[-1]
    w_spec = pl.BlockSpec((None, d, tn), lambda j, i: (0, 0, j))
    return pl.pallas_call(
        _gateup_kernel,
        out_shape=[jax.ShapeDtypeStruct((m, f), BF16), jax.ShapeDtypeStruct((f, d), BF16)],
        grid=(pl.cdiv(f, tn), m // tm),
        in_specs=[pl.BlockSpec((tm, d), lambda j, i: (i, 0)), w_spec, w_spec,
                  pl.BlockSpec((None, tn, d), lambda j, i: (0, j, 0))],
        out_specs=[pl.BlockSpec((tm, tn), lambda j, i: (i, j)),
                   pl.BlockSpec((tn, d), lambda j, i: (j, 0))],
        scratch_shapes=[pltpu.VMEM((d, tn), BF16), pltpu.VMEM((d, tn), BF16)],
        compiler_params=_cparams(2),
        name="ffn_gateup",
    )(u, w_gate, w_up, w_down)


def _proj_kernel(u_ref, w_ref, *rest, shift, tn, side):
    rest = list(rest)
    wn_ref = rest.pop(0) if shift else None
    ws_ref = rest.pop(0) if side else None
    o_ref = rest.pop(0)
    os_ref = rest.pop(0) if side else None
    (w_bf,) = rest

    @pl.when(pl.program_id(1) == 0)
    def _():
        w = w_ref[...]
        if shift:
            w = jnp.concatenate([w, wn_ref[...]], axis=0)[shift:shift + tn]
        w_bf[...] = w.astype(BF16)

    sub = min(u_ref.shape[0], MATMUL_SUB_ROWS)
    for r in range(0, u_ref.shape[0], sub):
        o_ref[r:r + sub, :] = _dot_nt(u_ref[r:r + sub, :], w_bf[...]).astype(o_ref.dtype)
    if side:
        os_ref[...] = _dot_nt(u_ref[...], ws_ref[...].astype(BF16))


def _proj(u, wt, windows, out_dtype, tn=1024, side=None):
    m, d = u.shape
    shift = windows[0][0] % tn
    starts = []
    for c0, n in windows:
        assert c0 % tn == shift and n % tn == 0 and shift % SUBLANES == 0
        starts += [c0 // tn + t for t in range(n // tn)]
    n_tiles = len(starts)
    tm = 2048 if n_tiles >= 3 else 1024

    def blk(j):
        b = jnp.int32(starts[0])
        for t in range(1, n_tiles):
            b = jnp.where(j >= t, starts[t], b)
        return b

    in_specs = [pl.BlockSpec((tm, d), lambda j, i: (i, 0)),
                pl.BlockSpec((None, tn, d), lambda j, i: (0, blk(j), 0))]
    args = [u, wt]
    if shift:
        assert tn % shift == 0
        per_tile = tn // shift
        in_specs.append(pl.BlockSpec((None, shift, d), lambda j, i: (0, (blk(j) + 1) * per_tile, 0)))
        args.append(wt)
    out_shape = [jax.ShapeDtypeStruct((m, n_tiles * tn), out_dtype)]
    out_specs = [pl.BlockSpec((tm, tn), lambda j, i: (i, j))]
    if side:
        s0, sn = side
        assert n_tiles == 1 and s0 % sn == 0
        in_specs.append(pl.BlockSpec((None, sn, d), lambda j, i: (0, s0 // sn, 0)))
        args.append(wt)
        out_shape.append(jax.ShapeDtypeStruct((m, sn), F32))
        out_specs.append(pl.BlockSpec((tm, sn), lambda j, i: (i, 0)))
    outs = pl.pallas_call(
        functools.partial(_proj_kernel, shift=shift, tn=tn, side=bool(side)),
        out_shape=out_shape,
        grid=(n_tiles, m // tm),
        in_specs=in_specs,
        out_specs=out_specs,
        scratch_shapes=[pltpu.VMEM((tn, d), BF16)],
        compiler_params=_cparams(2),
        name="in_proj",
    )(*args)
    return outs if side else outs[0]


def _rows_kernel(x_ref, w_ref, res_ref, post_ref, *rest, tn, res_scale, emit_next):
    if emit_next:
        next_ref, h_ref, u_ref, acc_a, acc_b = rest
    else:
        h_ref, acc_a, acc_b = rest
    i = pl.program_id(0)
    d = h_ref.shape[1]
    col_tiles = [slice(c, c + tn) for c in range(0, d, tn)]

    @pl.when(i == 0)
    def _():
        acc_b[...] = jnp.zeros_like(acc_b)

    def step(acc_cur, acc_prev):
        x = x_ref[...]
        for cols in col_tiles:
            acc_cur[:, cols] = jnp.dot(x, w_ref[:, cols], preferred_element_type=F32)

        ssq = None
        for cols in col_tiles:
            a = acc_prev[:, cols]
            s = jnp.sum(a * a, axis=-1, keepdims=True)
            ssq = s if ssq is None else ssq + s
        scale = lax.rsqrt(ssq / d + EPS) * res_scale
        hsq = None
        for cols in col_tiles:
            h = res_ref[:, cols] + acc_prev[:, cols] * scale * post_ref[:, cols]
            h_ref[:, cols] = h
            if emit_next:
                s = jnp.sum(h * h, axis=-1, keepdims=True)
                hsq = s if hsq is None else hsq + s
        if emit_next:
            nscale = lax.rsqrt(hsq / d + EPS)
            for cols in col_tiles:
                u_ref[:, cols] = (h_ref[:, cols] * nscale * next_ref[:, cols]).astype(u_ref.dtype)

    @pl.when(i % 2 == 0)
    def _():
        step(acc_a, acc_b)

    @pl.when(i % 2 == 1)
    def _():
        step(acc_b, acc_a)


def _rows(x, w, res, post_w, next_w, res_scale, tm=256, tn=512):
    m, k = x.shape
    d = w.shape[-1]
    n_tiles = m // tm
    emit_next = next_w is not None
    lag_spec = pl.BlockSpec((tm, d), lambda i: (jnp.maximum(i - 1, 0), 0))
    vec_spec = pl.BlockSpec((1, d), lambda i: (0, 0))
    in_specs = [pl.BlockSpec((tm, k), lambda i: (jnp.minimum(i, n_tiles - 1), 0)),
                pl.BlockSpec((k, d), lambda i: (0, 0), pipeline_mode=pl.Buffered(1)),
                lag_spec, vec_spec]
    args = [x, w, res, post_w]
    out_shape = [jax.ShapeDtypeStruct((m, d), F32)]
    out_specs = [lag_spec]
    if emit_next:
        in_specs.append(vec_spec)
        args.append(next_w)
        out_shape.append(jax.ShapeDtypeStruct((m, d), BF16))
        out_specs.append(lag_spec)
    outs = pl.pallas_call(
        functools.partial(_rows_kernel, tn=tn, res_scale=res_scale, emit_next=emit_next),
        out_shape=out_shape,
        grid=(n_tiles + 1,),
        in_specs=in_specs,
        out_specs=out_specs,
        scratch_shapes=[pltpu.VMEM((tm, d), F32), pltpu.VMEM((tm, d), F32)],
        compiler_params=_cparams(1),
        name="rows_matmul_norm",
    )(*args)
    return outs if emit_next else (outs[0], None)


def _split_bf16(x):
    hi = x.astype(BF16)
    lo = (x - hi.astype(F32)).astype(BF16)
    return hi, lo


def _dot_nt(a, b):
    return lax.dot_general(a, b, (((1,), (1,)), ((), ())), preferred_element_type=F32)


def _dot_tn(a, b):
    return lax.dot_general(a, b, (((0,), (0,)), ((), ())), preferred_element_type=F32)


N_SUB = CHUNK // SUB


def _chunk_consts():
    i = lax.broadcasted_iota(jnp.int32, (CHUNK, CHUNK), 0)
    j = lax.broadcasted_iota(jnp.int32, (CHUNK, CHUNK), 1)
    lo = (i // SUB) * SUB
    hi = lo + SUB
    groups = [(j >= lo) & (j <= i), (j > i) & (j < hi), j < lo, j >= hi]
    sel = jnp.concatenate([jnp.where(g, 1.0, 0.0) for g in groups], axis=0).astype(BF16)
    return jnp.concatenate([sel, sel], axis=1), i >= j


def _decay_sums(sel2, g):
    g_hi, g_lo = _split_bf16(g)
    return jnp.dot(sel2, jnp.concatenate([g_hi, g_lo], axis=0), preferred_element_type=F32)


def _gated_chunks(heads, causal):
    prep = [_chunk_operands(*h) for h in heads]
    scores = [_dot_nt(p[0], p[1]) for p in prep]
    kv = [jnp.dot(p[4], p[3], preferred_element_type=F32) for p in prep]
    outs, states = [], []
    for (q_t, k_t, q_st, k_st, v_t, st_bf, st_decay), s, upd, h in zip(prep, scores, kv, heads):
        s = jnp.where(causal, s, 0.0).astype(BF16)
        outs.append(_dot_nt(jnp.concatenate([q_st, s], axis=1),
                            jnp.concatenate([st_bf, v_t], axis=1)))
        states.append(h[4] * st_decay + upd)
    return outs, states


def _chunk_operands(q, k, v, dec, st):
    within, rest, before, after = dec
    dk = q.shape[1]
    cum = within + before
    q_st = (q * jnp.exp2(cum)).astype(BF16)
    k_st = (k * jnp.exp2(rest + after)).astype(BF16)
    q_in = (q * jnp.exp2(within)).astype(BF16)
    k_diag = (k * jnp.exp2(jnp.minimum(-within, EXP2_CLAMP))).astype(BF16)
    k_end = k * jnp.exp2(rest)
    k_end_bf = k_end.astype(BF16)

    blk = lambda x, b: x[b * SUB:(b + 1) * SUB]
    whole = {b: jnp.exp2(within[(b + 1) * SUB - 1:(b + 1) * SUB]) for b in range(1, N_SUB - 1)}
    skip = {}
    for bq in range(N_SUB):
        for bk in range(bq - 1):
            d = whole[bk + 1]
            for mid in range(bk + 2, bq):
                d = d * whole[mid]
            skip[bq, bk] = d
    zero = jnp.zeros((SUB, dk), BF16)
    k_cols, q_cols = [], []
    for bq in range(N_SUB):
        col = []
        for bk in range(N_SUB):
            if bk > bq:
                col.append(zero)
            elif bk == bq:
                col.append(blk(k_diag, bk))
            elif bk == bq - 1:
                col.append(blk(k_end_bf, bk))
            else:
                col.append((blk(k_end, bk) * skip[bq, bk]).astype(BF16))
        k_cols.append(jnp.concatenate(col, axis=0))
        q_cols.append(jnp.concatenate([blk(q_in, b) if b == bq else zero for b in range(N_SUB)],
                                      axis=0))
    q_tilde = jnp.concatenate(q_cols, axis=1)
    k_tilde = jnp.concatenate(k_cols, axis=1)
    v_t = v.T
    return q_tilde, k_tilde, q_st, k_st, v_t, st.astype(BF16), jnp.exp2(cum[CHUNK - 1:CHUNK])


def _head_norm_gate(o, norm_w, gate):
    return (o * _rms_scale(o) * norm_w * _silu(gate.astype(F32))).astype(BF16)


def _gla_kernel(q_ref, k_ref, v_ref, go_ref, code_ref, wgk_ref, bgk_ref, gn_ref, o_ref, st_ref,
                sums_ref, *, n_chunks, dk, dv):
    @pl.when(pl.program_id(1) == 0)
    def _():
        st_ref[...] = jnp.zeros_like(st_ref)

    sel2, causal = _chunk_consts()
    norm_w = gn_ref[...]
    q_scale = dk ** -0.5

    w_hi, w_lo = _split_bf16(wgk_ref[...])
    c_hi, c_lo = _split_bf16(code_ref[...])
    z = (jnp.dot(c_hi, w_hi, preferred_element_type=F32)
         + jnp.dot(c_lo, w_hi, preferred_element_type=F32)
         + jnp.dot(c_hi, w_lo, preferred_element_type=F32)) + bgk_ref[...]
    log2_a = ((jnp.minimum(z, 0.0) - jnp.log1p(jnp.exp2(jnp.abs(z) * (-LOG2_E))))
              * (LOG2_E / GLA_GATE_NORMALIZER))
    for c in range(n_chunks):
        sums_ref[c] = _decay_sums(sel2, log2_a[c * CHUNK:(c + 1) * CHUNK])

    def chunk_body(c, carry):
        rows = pl.ds(pl.multiple_of(c * CHUNK, CHUNK), CHUNK)
        heads = []
        for h in range(GLA_HEADS):
            kc = slice(h * dk, (h + 1) * dk)
            q = q_ref[rows, kc].astype(F32) * q_scale
            k = k_ref[rows, kc].astype(F32)
            dec = [sums_ref[c, n * CHUNK:(n + 1) * CHUNK, kc] for n in range(4)]
            heads.append((q, k, v_ref[rows, h * dv:(h + 1) * dv], dec, st_ref[h]))
        outs, states = _gated_chunks(heads, causal)
        for h in range(GLA_HEADS):
            vc = slice(h * dv, (h + 1) * dv)
            st_ref[h] = states[h]
            o_ref[rows, vc] = _head_norm_gate(outs[h], norm_w, go_ref[rows, vc])
        return carry

    lax.fori_loop(0, n_chunks, chunk_body, 0)


def _gla(p_a, p_code, wgk_pad, b_gk, gla_norm, batch, seq, t_blk=512):
    kw = wgk_pad.shape[1]
    vw = (p_a.shape[1] - 2 * kw) // 2
    dk = kw // GLA_HEADS
    dv = vw // GLA_HEADS
    nt = seq // t_blk
    row = lambda b, t: b * nt + t
    return pl.pallas_call(
        functools.partial(_gla_kernel, n_chunks=t_blk // CHUNK, dk=dk, dv=dv),
        out_shape=jax.ShapeDtypeStruct((batch * seq, vw), BF16),
        grid=(batch, nt),
        in_specs=[pl.BlockSpec((t_blk, kw), lambda b, t: (row(b, t), 0)),
                  pl.BlockSpec((t_blk, kw), lambda b, t: (row(b, t), 1)),
                  pl.BlockSpec((t_blk, vw), lambda b, t: (row(b, t), 1)),
                  pl.BlockSpec((t_blk, vw), lambda b, t: (row(b, t), 2)),
                  pl.BlockSpec((t_blk, LANES), lambda b, t: (row(b, t), 0)),
                  pl.BlockSpec((LANES, kw), lambda b, t: (0, 0)),
                  pl.BlockSpec((1, kw), lambda b, t: (0, 0)),
                  pl.BlockSpec((1, dv), lambda b, t: (0, 0))],
        out_specs=pl.BlockSpec((t_blk, vw), lambda b, t: (row(b, t), 0)),
        scratch_shapes=[pltpu.VMEM((GLA_HEADS, dv, dk), F32),
                        pltpu.VMEM((t_blk // CHUNK, 4 * CHUNK, kw), F32)],
        compiler_params=_cparams(2),
        name="gla_mixer",
    )(p_a, p_a, p_a, p_a, p_code, wgk_pad, b_gk, gla_norm)


def _hgrn_kernel(hq_ref, hi_ref, ho_ref, hf_ref, lbl_ref, hn_ref, o_ref, st_ref,
                 *, n_chunks, n_heads, dk, layer):
    @pl.when(pl.program_id(1) == 0)
    def _():
        st_ref[...] = jnp.zeros_like(st_ref)

    sel2, causal = _chunk_consts()
    logits = lbl_ref[...]
    p = jnp.exp(logits - jnp.max(logits, axis=0, keepdims=True))
    p = p / jnp.sum(p, axis=0, keepdims=True)
    lb = jnp.sum(p[:layer + 1], axis=0, keepdims=True)
    one_m_lb = jnp.sum(p[layer + 1:], axis=0, keepdims=True)
    norm_w = hn_ref[...]

    def chunk_body(c, carry):
        rows = pl.ds(pl.multiple_of(c * CHUNK, CHUNK), CHUNK)
        hf = hf_ref[rows, :]
        e = jnp.exp2(jnp.abs(hf) * (-LOG2_E))
        inv = 1.0 / (1.0 + e)
        pos = hf >= 0.0
        sig = jnp.where(pos, inv, e * inv)
        sig_neg = jnp.where(pos, e * inv, inv)
        log2_f = jnp.log2(lb + one_m_lb * sig)
        k_all = one_m_lb * sig_neg
        sums = _decay_sums(sel2, log2_f)
        heads = []
        for h in range(n_heads):
            hc = slice(h * dk, (h + 1) * dk)
            q = _silu(hq_ref[rows, hc].astype(F32))
            dec = [sums[n * CHUNK:(n + 1) * CHUNK, hc] for n in range(4)]
            heads.append((q, k_all[:, hc], hi_ref[rows, hc], dec, st_ref[h]))
        outs, states = _gated_chunks(heads, causal)
        for h in range(n_heads):
            hc = slice(h * dk, (h + 1) * dk)
            st_ref[h] = states[h]
            o_ref[rows, hc] = _head_norm_gate(outs[h], norm_w, ho_ref[rows, hc])
        return carry

    lax.fori_loop(0, n_chunks, chunk_body, 0)


def _hgrn(p_r, p_f, lb_logits, hgrn_norm, layer, batch, seq, t_blk=512):
    w = p_f.shape[1]
    dk = HGRN_EXPAND
    n_heads = w // dk
    nt = seq // t_blk
    n_lb = lb_logits.shape[0]
    row = lambda b, t: b * nt + t
    return pl.pallas_call(
        functools.partial(_hgrn_kernel, n_chunks=t_blk // CHUNK, n_heads=n_heads, dk=dk,
                          layer=layer),
        out_shape=jax.ShapeDtypeStruct((batch * seq, w), BF16),
        grid=(batch, nt),
        in_specs=[pl.BlockSpec((t_blk, w), lambda b, t: (row(b, t), 0)),
                  pl.BlockSpec((t_blk, w), lambda b, t: (row(b, t), 1)),
                  pl.BlockSpec((t_blk, w), lambda b, t: (row(b, t), 2)),
                  pl.BlockSpec((t_blk, w), lambda b, t: (row(b, t), 0)),
                  pl.BlockSpec((n_lb, w), lambda b, t: (0, 0)),
                  pl.BlockSpec((1, dk), lambda b, t: (0, 0))],
        out_specs=pl.BlockSpec((t_blk, w), lambda b, t: (row(b, t), 0)),
        scratch_shapes=[pltpu.VMEM((n_heads, dk, dk), F32)],
        compiler_params=_cparams(2),
        name="hgrn_mixer",
    )(p_r, p_r, p_r, p_f, lb_logits, hgrn_norm)


def _merge_kernel(og_ref, oh_ref, zg_ref, zh_ref, wg_ref, wh_ref, bg_ref, bh_ref, wo_ref,
                  o_ref, wo_bf_ref, wg_bf, wh_bf):
    @pl.when(pl.program_id(1) == 0)
    def _():
        wg_bf[...] = wg_ref[...].astype(BF16)
        wh_bf[...] = wh_ref[...].astype(BF16)

    wo_bf_ref[...] = wo_ref[...].astype(BF16)
    a = jnp.dot(og_ref[...], wg_bf[...], preferred_element_type=F32)
    b = jnp.dot(oh_ref[...], wh_bf[...], preferred_element_type=F32)
    o_ref[...] = (_sigmoid(zg_ref[...].astype(F32) + bg_ref[...]) * a
                  + _sigmoid(zh_ref[...].astype(F32) + bh_ref[...]) * b).astype(o_ref.dtype)


def _merge(o_gla, o_hgrn, p_b, zg_col0, zh_col0, w_bg, w_bh, b_gates, w_out, tm=1024, tn=1024):
    m, kdim = o_gla.shape
    d = w_bg.shape[-1]
    ni = m // tm
    k_out, d_out = w_out.shape[1:]
    rb = k_out // ((d // tn) * ni)
    assert rb * (d // tn) * ni == k_out and rb % (2 * SUBLANES) == 0
    x_spec = pl.BlockSpec((tm, kdim), lambda j, i: (i, 0))
    w_spec = pl.BlockSpec((None, kdim, tn), lambda j, i: (0, 0, j))
    return pl.pallas_call(
        _merge_kernel,
        out_shape=[jax.ShapeDtypeStruct((m, d), BF16), jax.ShapeDtypeStruct((k_out, d_out), BF16)],
        grid=(d // tn, ni),
        in_specs=[x_spec, x_spec,
                  pl.BlockSpec((tm, tn), lambda j, i: (i, zg_col0 // tn + j)),
                  pl.BlockSpec((tm, tn), lambda j, i: (i, zh_col0 // tn + j)),
                  w_spec, w_spec,
                  pl.BlockSpec((None, 1, tn), lambda j, i: (0, 0, j)),
                  pl.BlockSpec((None, 1, tn), lambda j, i: (1, 0, j)),
                  pl.BlockSpec((None, rb, d_out), lambda j, i: (0, j * ni + i, 0))],
        out_specs=[pl.BlockSpec((tm, tn), lambda j, i: (i, j)),
                   pl.BlockSpec((rb, d_out), lambda j, i: (j * ni + i, 0))],
        scratch_shapes=[pltpu.VMEM((kdim, tn), BF16), pltpu.VMEM((kdim, tn), BF16)],
        compiler_params=_cparams(2),
        name="branch_merge",
    )(o_gla, o_hgrn, p_b, p_b, w_bg, w_bh, b_gates, b_gates, w_out)


def kernel(x, ffn1_pre_norm, ffn1_w_gate, ffn1_w_up, ffn1_w_down, ffn1_post_norm, mix_pre_norm, w_in, gla_w_gk_up, gla_b_gk, gla_norm, hgrn_lb_logits, hgrn_norm, w_branch_gla, w_branch_hgrn, b_branch_gates, w_out, mix_post_norm, ffn2_pre_norm, ffn2_w_gate, ffn2_w_up, ffn2_w_down, ffn2_post_norm):
    batch, seq, d_model = x.shape
    depth = ffn1_w_gate.shape[0]
    m = batch * seq
    kw = gla_w_gk_up.shape[-1]
    vw = d_model // 2
    a_cols = 2 * kw + 2 * vw
    code0 = a_cols
    hq0 = code0 + GLA_GATE_RANK
    hf0, hi0 = hq0 + vw, hq0 + 2 * vw

    h = x.reshape(m, d_model)
    u = _rmsnorm(h, ffn1_pre_norm[0:1])
    for l in range(depth):
        mid, w_down = _gateup(u, ffn1_w_gate[l:l + 1], ffn1_w_up[l:l + 1], ffn1_w_down[l:l + 1])
        h, u = _rows(mid, w_down, h, ffn1_post_norm[l:l + 1],
                     mix_pre_norm[l:l + 1], 0.5)

        w_l = jnp.swapaxes(w_in[l:l + 1], 1, 2)
        p_a = _proj(u, w_l, [(0, a_cols)], BF16)
        p_f, p_code = _proj(u, w_l, [(hf0, vw)], F32, side=(code0, LANES))
        p_r = _proj(u, w_l, [(hq0, vw), (hi0, 2 * vw + 2 * d_model)], BF16)
        wgk_pad = jnp.pad(gla_w_gk_up[l], ((0, LANES - GLA_GATE_RANK), (0, 0)))
        o_gla = _gla(p_a, p_code, wgk_pad, gla_b_gk[l:l + 1], gla_norm[l:l + 1], batch, seq)
        o_hgrn = _hgrn(p_r, p_f, hgrn_lb_logits, hgrn_norm[l:l + 1], l, batch, seq)
        merged, w_out_bf = _merge(o_gla, o_hgrn, p_r, 3 * vw, 3 * vw + d_model,
                                  w_branch_gla[l:l + 1], w_branch_hgrn[l:l + 1],
                                  b_branch_gates[l].reshape(2, 1, d_model), w_out[l:l + 1])
        h, u = _rows(merged, w_out_bf, h, mix_post_norm[l:l + 1],
                     ffn2_pre_norm[l:l + 1], 1.0, tm=512)

        mid, w_down = _gateup(u, ffn2_w_gate[l:l + 1], ffn2_w_up[l:l + 1], ffn2_w_down[l:l + 1])
        next_norm = ffn1_pre_norm[l + 1:l + 2] if l + 1 < depth else None
        h, u = _rows(mid, w_down, h, ffn2_post_norm[l:l + 1], next_norm, 0.5)
    return h.reshape(batch, seq, d_model)
```

```python
import functools

import jax
import jax.numpy as jnp
from jax import lax
from jax.experimental import pallas as pl
from jax.experimental.pallas import tpu as pltpu

F32 = jnp.float32
BF16 = jnp.bfloat16

EPS = 1e-6
CHUNK = 64
SUB = 16
GLA_HEADS = 4
GLA_GATE_RANK = 16
GLA_GATE_NORMALIZER = 16.0
HGRN_EXPAND = 128
LOG2_E = 1.4426950408889634
EXP2_CLAMP = 115.0

LANES = 128
SUBLANES = 8
VMEM_LIMIT = 62 * 1024 * 1024
MATMUL_SUB_ROWS = 1024


def _cparams(n_axes):
    return pltpu.CompilerParams(
        dimension_semantics=("arbitrary",) * n_axes, vmem_limit_bytes=VMEM_LIMIT)


def _sigmoid(x):
    return 1.0 / (1.0 + jnp.exp2(x * (-LOG2_E)))


def _silu(x):
    return x * _sigmoid(x)


def _rms_scale(x):
    return lax.rsqrt(jnp.mean(x * x, axis=-1, keepdims=True) + EPS)


def _rmsnorm_kernel(x_ref, w_ref, o_ref):
    x = x_ref[...]
    o_ref[...] = (x * _rms_scale(x) * w_ref[...]).astype(o_ref.dtype)


def _rmsnorm(x, w, tm=512):
    m, d = x.shape
    return pl.pallas_call(
        _rmsnorm_kernel,
        out_shape=jax.ShapeDtypeStruct((m, d), BF16),
        grid=(m // tm,),
        in_specs=[pl.BlockSpec((tm, d), lambda i: (i, 0)),
                  pl.BlockSpec((1, d), lambda i: (0, 0))],
        out_specs=pl.BlockSpec((tm, d), lambda i: (i, 0)),
        compiler_params=_cparams(1),
        name="rmsnorm",
    )(x, w)


def _gateup_kernel(u_ref, wg_ref, wu_ref, wd_ref, o_ref, wd_bf_ref, wg_bf, wu_bf):
    @pl.when(pl.program_id(1) == 0)
    def _():
        wg_bf[...] = wg_ref[...].astype(BF16)
        wu_bf[...] = wu_ref[...].astype(BF16)
        wd_bf_ref[...] = wd_ref[...].astype(BF16)

    for r in range(0, u_ref.shape[0], MATMUL_SUB_ROWS):
        rows = slice(r, r + MATMUL_SUB_ROWS)
        u = u_ref[rows, :]
        g = jnp.dot(u, wg_bf[...], preferred_element_type=F32)
        up = jnp.dot(u, wu_bf[...], preferred_element_type=F32)
        o_ref[rows, :] = (_silu(g) * up).astype(o_ref.dtype)


def _gateup(u, w_gate, w_up, w_down, tm=2048, tn=512):
    m, d = u.shape
    f = w_gate.shape[-1]
    w_spec = pl.BlockSpec((None, d, tn), lambda j, i: (0, 0, j))
    return pl.pallas_call(
        _gateup_kernel,
        out_shape=[jax.ShapeDtypeStruct((m, f), BF16), jax.ShapeDtypeStruct((f, d), BF16)],
        grid=(pl.cdiv(f, tn), m // tm),
        in_specs=[pl.BlockSpec((tm, d), lambda j, i: (i, 0)), w_spec, w_spec,
                  pl.BlockSpec((None, tn, d), lambda j, i: (0, j, 0))],
        out_specs=[pl.BlockSpec((tm, tn), lambda j, i: (i, j)),
                   pl.BlockSpec((tn, d), lambda j, i: (j, 0))],
        scratch_shapes=[pltpu.VMEM((d, tn), BF16), pltpu.VMEM((d, tn), BF16)],
        compiler_params=_cparams(2),
        name="ffn_gateup",
    )(u, w_gate, w_up, w_down)


def _proj_kernel(u_ref, w_ref, *rest, shift, tn, side):
    rest = list(rest)
    wn_ref = rest.pop(0) if shift else None
    ws_ref = rest.pop(0) if side else None
    o_ref = rest.pop(0)
    os_ref = rest.pop(0) if side else None
    (w_bf,) = rest

    @pl.when(pl.program_id(1) == 0)
    def _():
        w = w_ref[...]
        if shift:
            w = jnp.concatenate([w, wn_ref[...]], axis=0)[shift:shift + tn]
        w_bf[...] = w.astype(BF16)

    sub = min(u_ref.shape[0], MATMUL_SUB_ROWS)
    for r in range(0, u_ref.shape[0], sub):
        o_ref[r:r + sub, :] = _dot_nt(u_ref[r:r + sub, :], w_bf[...]).astype(o_ref.dtype)
    if side:
        os_ref[...] = _dot_nt(u_ref[...], ws_ref[...].astype(BF16))


def _proj(u, wt, windows, out_dtype, tn=1024, side=None):
    m, d = u.shape
    shift = windows[0][0] % tn
    starts = []
    for c0, n in windows:
        assert c0 % tn == shift and n % tn == 0 and shift % SUBLANES == 0
        starts += [c0 // tn + t for t in range(n // tn)]
    n_tiles = len(starts)
    tm = 2048 if n_tiles >= 3 else 1024

    def blk(j):
        b = jnp.int32(starts[0])
        for t in range(1, n_tiles):
            b = jnp.where(j >= t, starts[t], b)
        return b

    in_specs = [pl.BlockSpec((tm, d), lambda j, i: (i, 0)),
                pl.BlockSpec((None, tn, d), lambda j, i: (0, blk(j), 0))]
    args = [u, wt]
    if shift:
        assert tn % shift == 0
        per_tile = tn // shift
        in_specs.append(pl.BlockSpec((None, shift, d), lambda j, i: (0, (blk(j) + 1) * per_tile, 0)))
        args.append(wt)
    out_shape = [jax.ShapeDtypeStruct((m, n_tiles * tn), out_dtype)]
    out_specs = [pl.BlockSpec((tm, tn), lambda j, i: (i, j))]
    if side:
        s0, sn = side
        assert n_tiles == 1 and s0 % sn == 0
        in_specs.append(pl.BlockSpec((None, sn, d), lambda j, i: (0, s0 // sn, 0)))
        args.append(wt)
        out_shape.append(jax.ShapeDtypeStruct((m, sn), F32))
        out_specs.append(pl.BlockSpec((tm, sn), lambda j, i: (i, 0)))
    outs = pl.pallas_call(
        functools.partial(_proj_kernel, shift=shift, tn=tn, side=bool(side)),
        out_shape=out_shape,
        grid=(n_tiles, m // tm),
        in_specs=in_specs,
        out_specs=out_specs,
        scratch_shapes=[pltpu.VMEM((tn, d), BF16)],
        compiler_params=_cparams(2),
        name="in_proj",
    )(*args)
    return outs if side else outs[0]


def _rows_kernel(x_ref, w_ref, res_ref, post_ref, *rest, tn, res_scale, emit_next):
    if emit_next:
        next_ref, h_ref, u_ref, acc_ref = rest
    else:
        h_ref, acc_ref = rest
    tm, d = h_ref.shape
    col_tiles = [slice(c, c + tn) for c in range(0, d, tn)]

    for r in range(0, tm, ROWS_SUB):
        rows = slice(r, r + ROWS_SUB)
        x = x_ref[rows, :]
        for cols in col_tiles:
            acc_ref[rows, cols] = jnp.dot(x, w_ref[:, cols], preferred_element_type=F32)

        ssq = None
        for cols in col_tiles:
            a = acc_ref[rows, cols]
            s = jnp.sum(a * a, axis=-1, keepdims=True)
            ssq = s if ssq is None else ssq + s
        scale = lax.rsqrt(ssq / d + EPS) * res_scale
        hsq = None
        for cols in col_tiles:
            h = res_ref[rows, cols] + acc_ref[rows, cols] * scale * post_ref[:, cols]
            h_ref[rows, cols] = h
            if emit_next:
                s = jnp.sum(h * h, axis=-1, keepdims=True)
                hsq = s if hsq is None else hsq + s
        if emit_next:
            nscale = lax.rsqrt(hsq / d + EPS)
            for cols in col_tiles:
                u_ref[rows, cols] = (h_ref[rows, cols] * nscale
                                     * next_ref[:, cols]).astype(u_ref.dtype)


ROWS_SUB = 256


def _rows(x, w, res, post_w, next_w, res_scale, tm=512, tn=512):
    m, k = x.shape
    d = w.shape[-1]
    emit_next = next_w is not None
    row_spec = pl.BlockSpec((tm, d), lambda i: (i, 0))
    vec_spec = pl.BlockSpec((1, d), lambda i: (0, 0))
    in_specs = [pl.BlockSpec((tm, k), lambda i: (i, 0)),
                pl.BlockSpec((k, d), lambda i: (0, 0), pipeline_mode=pl.Buffered(1)),
                row_spec, vec_spec]
    args = [x, w, res, post_w]
    out_shape = [jax.ShapeDtypeStruct((m, d), F32)]
    out_specs = [row_spec]
    if emit_next:
        in_specs.append(vec_spec)
        args.append(next_w)
        out_shape.append(jax.ShapeDtypeStruct((m, d), BF16))
        out_specs.append(row_spec)
    outs = pl.pallas_call(
        functools.partial(_rows_kernel, tn=tn, res_scale=res_scale, emit_next=emit_next),
        out_shape=out_shape,
        grid=(m // tm,),
        in_specs=in_specs,
        out_specs=out_specs,
        scratch_shapes=[pltpu.VMEM((tm, d), F32)],
        compiler_params=_cparams(1),
        name="rows_matmul_norm",
    )(*args)
    return outs if emit_next else (outs[0], None)


def _split_bf16(x):
    hi = x.astype(BF16)
    lo = (x - hi.astype(F32)).astype(BF16)
    return hi, lo


def _dot_nt(a, b):
    return lax.dot_general(a, b, (((1,), (1,)), ((), ())), preferred_element_type=F32)


def _dot_tn(a, b):
    return lax.dot_general(a, b, (((0,), (0,)), ((), ())), preferred_element_type=F32)


N_SUB = CHUNK // SUB


def _chunk_consts():
    i = lax.broadcasted_iota(jnp.int32, (CHUNK, CHUNK), 0)
    j = lax.broadcasted_iota(jnp.int32, (CHUNK, CHUNK), 1)
    lo = (i // SUB) * SUB
    hi = lo + SUB
    groups = [(j >= lo) & (j <= i), (j > i) & (j < hi), j < lo, j >= hi]
    sel = jnp.concatenate([jnp.where(g, 1.0, 0.0) for g in groups], axis=0).astype(BF16)
    return jnp.concatenate([sel, sel], axis=1), i >= j


def _decay_sums(sel2, g):
    g_hi, g_lo = _split_bf16(g)
    return jnp.dot(sel2, jnp.concatenate([g_hi, g_lo], axis=0), preferred_element_type=F32)


def _gated_chunks(heads, causal):
    prep = [_chunk_operands(*h) for h in heads]
    scores = [_dot_nt(p[0], p[1]) for p in prep]
    kv = [jnp.dot(p[4], p[3], preferred_element_type=F32) for p in prep]
    outs, states = [], []
    for (q_t, k_t, q_st, k_st, v_t, st_bf, st_decay), s, upd, h in zip(prep, scores, kv, heads):
        s = jnp.where(causal, s, 0.0).astype(BF16)
        outs.append(_dot_nt(jnp.concatenate([q_st, s], axis=1),
                            jnp.concatenate([st_bf, v_t], axis=1)))
        states.append(h[4] * st_decay + upd)
    return outs, states


def _chunk_operands(q, k, v, dec, st):
    within, rest, before, after = dec
    dk = q.shape[1]
    cum = within + before
    q_st = (q * jnp.exp2(cum)).astype(BF16)
    k_st = (k * jnp.exp2(rest + after)).astype(BF16)
    q_in = (q * jnp.exp2(within)).astype(BF16)
    k_diag = (k * jnp.exp2(jnp.minimum(-within, EXP2_CLAMP))).astype(BF16)
    k_end = k * jnp.exp2(rest)
    k_end_bf = k_end.astype(BF16)

    blk = lambda x, b: x[b * SUB:(b + 1) * SUB]
    whole = {b: jnp.exp2(within[(b + 1) * SUB - 1:(b + 1) * SUB]) for b in range(1, N_SUB - 1)}
    skip = {}
    for bq in range(N_SUB):
        for bk in range(bq - 1):
            d = whole[bk + 1]
            for mid in range(bk + 2, bq):
                d = d * whole[mid]
            skip[bq, bk] = d
    zero = jnp.zeros((SUB, dk), BF16)
    k_cols, q_cols = [], []
    for bq in range(N_SUB):
        col = []
        for bk in range(N_SUB):
            if bk > bq:
                col.append(zero)
            elif bk == bq:
                col.append(blk(k_diag, bk))
            elif bk == bq - 1:
                col.append(blk(k_end_bf, bk))
            else:
                col.append((blk(k_end, bk) * skip[bq, bk]).astype(BF16))
        k_cols.append(jnp.concatenate(col, axis=0))
        q_cols.append(jnp.concatenate([blk(q_in, b) if b == bq else zero for b in range(N_SUB)],
                                      axis=0))
    q_tilde = jnp.concatenate(q_cols, axis=1)
    k_tilde = jnp.concatenate(k_cols, axis=1)
    v_t = v.T
    return q_tilde, k_tilde, q_st, k_st, v_t, st.astype(BF16), jnp.exp2(cum[CHUNK - 1:CHUNK])


def _head_norm_gate(o, norm_w, gate):
    return (o * _rms_scale(o) * norm_w * _silu(gate.astype(F32))).astype(BF16)


def _gla_kernel(q_ref, k_ref, v_ref, go_ref, code_ref, wgk_ref, bgk_ref, gn_ref, o_ref, st_ref,
                sums_ref, *, n_chunks, dk, dv):
    @pl.when(pl.program_id(1) == 0)
    def _():
        st_ref[...] = jnp.zeros_like(st_ref)

    sel2, causal = _chunk_consts()
    norm_w = gn_ref[...]
    q_scale = dk ** -0.5

    w_hi, w_lo = _split_bf16(wgk_ref[...])
    c_hi, c_lo = _split_bf16(code_ref[...])
    z = (jnp.dot(c_hi, w_hi, preferred_element_type=F32)
         + jnp.dot(c_lo, w_hi, preferred_element_type=F32)
         + jnp.dot(c_hi, w_lo, preferred_element_type=F32)) + bgk_ref[...]
    log2_a = ((jnp.minimum(z, 0.0) - jnp.log1p(jnp.exp2(jnp.abs(z) * (-LOG2_E))))
              * (LOG2_E / GLA_GATE_NORMALIZER))
    for c in range(n_chunks):
        sums_ref[c] = _decay_sums(sel2, log2_a[c * CHUNK:(c + 1) * CHUNK])

    def chunk_body(c, carry):
        rows = pl.ds(pl.multiple_of(c * CHUNK, CHUNK), CHUNK)
        heads = []
        for h in range(GLA_HEADS):
            kc = slice(h * dk, (h + 1) * dk)
            q = q_ref[rows, kc].astype(F32) * q_scale
            k = k_ref[rows, kc].astype(F32)
            dec = [sums_ref[c, n * CHUNK:(n + 1) * CHUNK, kc] for n in range(4)]
            heads.append((q, k, v_ref[rows, h * dv:(h + 1) * dv], dec, st_ref[h]))
        outs, states = _gated_chunks(heads, causal)
        for h in range(GLA_HEADS):
            vc = slice(h * dv, (h + 1) * dv)
            st_ref[h] = states[h]
            o_ref[rows, vc] = _head_norm_gate(outs[h], norm_w, go_ref[rows, vc])
        return carry

    lax.fori_loop(0, n_chunks, chunk_body, 0)


def _gla(p_a, p_code, wgk_pad, b_gk, gla_norm, batch, seq, t_blk=512):
    kw = wgk_pad.shape[1]
    vw = (p_a.shape[1] - 2 * kw) // 2
    dk = kw // GLA_HEADS
    dv = vw // GLA_HEADS
    nt = seq // t_blk
    row = lambda b, t: b * nt + t
    return pl.pallas_call(
        functools.partial(_gla_kernel, n_chunks=t_blk // CHUNK, dk=dk, dv=dv),
        out_shape=jax.ShapeDtypeStruct((batch * seq, vw), BF16),
        grid=(batch, nt),
        in_specs=[pl.BlockSpec((t_blk, kw), lambda b, t: (row(b, t), 0)),
                  pl.BlockSpec((t_blk, kw), lambda b, t: (row(b, t), 1)),
                  pl.BlockSpec((t_blk, vw), lambda b, t: (row(b, t), 1)),
                  pl.BlockSpec((t_blk, vw), lambda b, t: (row(b, t), 2)),
                  pl.BlockSpec((t_blk, LANES), lambda b, t: (row(b, t), 0)),
                  pl.BlockSpec((LANES, kw), lambda b, t: (0, 0)),
                  pl.BlockSpec((1, kw), lambda b, t: (0, 0)),
                  pl.BlockSpec((1, dv), lambda b, t: (0, 0))],
        out_specs=pl.BlockSpec((t_blk, vw), lambda b, t: (row(b, t), 0)),
        scratch_shapes=[pltpu.VMEM((GLA_HEADS, dv, dk), F32),
                        pltpu.VMEM((t_blk // CHUNK, 4 * CHUNK, kw), F32)],
        compiler_params=_cparams(2),
        name="gla_mixer",
    )(p_a, p_a, p_a, p_a, p_code, wgk_pad, b_gk, gla_norm)


def _hgrn_kernel(hq_ref, hi_ref, ho_ref, hf_ref, lbl_ref, hn_ref, o_ref, st_ref,
                 *, n_chunks, n_heads, dk, layer):
    @pl.when(pl.program_id(1) == 0)
    def _():
        st_ref[...] = jnp.zeros_like(st_ref)

    sel2, causal = _chunk_consts()
    logits = lbl_ref[...]
    p = jnp.exp(logits - jnp.max(logits, axis=0, keepdims=True))
    p = p / jnp.sum(p, axis=0, keepdims=True)
    lb = jnp.sum(p[:layer + 1], axis=0, keepdims=True)
    one_m_lb = jnp.sum(p[layer + 1:], axis=0, keepdims=True)
    norm_w = hn_ref[...]

    def chunk_body(c, carry):
        rows = pl.ds(pl.multiple_of(c * CHUNK, CHUNK), CHUNK)
        hf = hf_ref[rows, :]
        e = jnp.exp2(jnp.abs(hf) * (-LOG2_E))
        inv = 1.0 / (1.0 + e)
        pos = hf >= 0.0
        sig = jnp.where(pos, inv, e * inv)
        sig_neg = jnp.where(pos, e * inv, inv)
        log2_f = jnp.log2(lb + one_m_lb * sig)
        k_all = one_m_lb * sig_neg
        sums = _decay_sums(sel2, log2_f)
        heads = []
        for h in range(n_heads):
            hc = slice(h * dk, (h + 1) * dk)
            q = _silu(hq_ref[rows, hc].astype(F32))
            dec = [sums[n * CHUNK:(n + 1) * CHUNK, hc] for n in range(4)]
            heads.append((q, k_all[:, hc], hi_ref[rows, hc], dec, st_ref[h]))
        outs, states = _gated_chunks(heads, causal)
        for h in range(n_heads):
            hc = slice(h * dk, (h + 1) * dk)
            st_ref[h] = states[h]
            o_ref[rows, hc] = _head_norm_gate(outs[h], norm_w, ho_ref[rows, hc])
        return carry

    lax.fori_loop(0, n_chunks, chunk_body, 0)


def _hgrn(p_r, p_f, lb_logits, hgrn_norm, layer, batch, seq, t_blk=512):
    w = p_f.shape[1]
    dk = HGRN_EXPAND
    n_heads = w // dk
    nt = seq // t_blk
    n_lb = lb_logits.shape[0]
    row = lambda b, t: b * nt + t
    return pl.pallas_call(
        functools.partial(_hgrn_kernel, n_chunks=t_blk // CHUNK, n_heads=n_heads, dk=dk,
                          layer=layer),
        out_shape=jax.ShapeDtypeStruct((batch * seq, w), BF16),
        grid=(batch, nt),
        in_specs=[pl.BlockSpec((t_blk, w), lambda b, t: (row(b, t), 0)),
                  pl.BlockSpec((t_blk, w), lambda b, t: (row(b, t), 1)),
                  pl.BlockSpec((t_blk, w), lambda b, t: (row(b, t), 2)),
                  pl.BlockSpec((t_blk, w), lambda b, t: (row(b, t), 0)),
                  pl.BlockSpec((n_lb, w), lambda b, t: (0, 0)),
                  pl.BlockSpec((1, dk), lambda b, t: (0, 0))],
        out_specs=pl.BlockSpec((t_blk, w), lambda b, t: (row(b, t), 0)),
        scratch_shapes=[pltpu.VMEM((n_heads, dk, dk), F32)],
        compiler_params=_cparams(2),
        name="hgrn_mixer",
    )(p_r, p_r, p_r, p_f, lb_logits, hgrn_norm)


def _merge_kernel(og_ref, oh_ref, zg_ref, zh_ref, wg_ref, wh_ref, bg_ref, bh_ref, wo_ref,
                  o_ref, wo_bf_ref, wg_bf, wh_bf):
    @pl.when(pl.program_id(1) == 0)
    def _():
        wg_bf[...] = wg_ref[...].astype(BF16)
        wh_bf[...] = wh_ref[...].astype(BF16)

    wo_bf_ref[...] = wo_ref[...].astype(BF16)
    a = jnp.dot(og_ref[...], wg_bf[...], preferred_element_type=F32)
    b = jnp.dot(oh_ref[...], wh_bf[...], preferred_element_type=F32)
    o_ref[...] = (_sigmoid(zg_ref[...].astype(F32) + bg_ref[...]) * a
                  + _sigmoid(zh_ref[...].astype(F32) + bh_ref[...]) * b).astype(o_ref.dtype)


def _merge(o_gla, o_hgrn, p_b, zg_col0, zh_col0, w_bg, w_bh, b_gates, w_out, tm=1024, tn=1024):
    m, kdim = o_gla.shape
    d = w_bg.shape[-1]
    ni = m // tm
    k_out, d_out = w_out.shape[1:]
    rb = k_out // ((d // tn) * ni)
    assert rb * (d // tn) * ni == k_out and rb % (2 * SUBLANES) == 0
    x_spec = pl.BlockSpec((tm, kdim), lambda j, i: (i, 0))
    w_spec = pl.BlockSpec((None, kdim, tn), lambda j, i: (0, 0, j))
    return pl.pallas_call(
        _merge_kernel,
        out_shape=[jax.ShapeDtypeStruct((m, d), BF16), jax.ShapeDtypeStruct((k_out, d_out), BF16)],
        grid=(d // tn, ni),
        in_specs=[x_spec, x_spec,
                  pl.BlockSpec((tm, tn), lambda j, i: (i, zg_col0 // tn + j)),
                  pl.BlockSpec((tm, tn), lambda j, i: (i, zh_col0 // tn + j)),
                  w_spec, w_spec,
                  pl.BlockSpec((None, 1, tn), lambda j, i: (0, 0, j)),
                  pl.BlockSpec((None, 1, tn), lambda j, i: (1, 0, j)),
                  pl.BlockSpec((None, rb, d_out), lambda j, i: (0, j * ni + i, 0))],
        out_specs=[pl.BlockSpec((tm, tn), lambda j, i: (i, j)),
                   pl.BlockSpec((rb, d_out), lambda j, i: (j * ni + i, 0))],
        scratch_shapes=[pltpu.VMEM((kdim, tn), BF16), pltpu.VMEM((kdim, tn), BF16)],
        compiler_params=_cparams(2),
        name="branch_merge",
    )(o_gla, o_hgrn, p_b, p_b, w_bg, w_bh, b_gates, b_gates, w_out)


def kernel(x, ffn1_pre_norm, ffn1_w_gate, ffn1_w_up, ffn1_w_down, ffn1_post_norm, mix_pre_norm, w_in, gla_w_gk_up, gla_b_gk, gla_norm, hgrn_lb_logits, hgrn_norm, w_branch_gla, w_branch_hgrn, b_branch_gates, w_out, mix_post_norm, ffn2_pre_norm, ffn2_w_gate, ffn2_w_up, ffn2_w_down, ffn2_post_norm):
    batch, seq, d_model = x.shape
    depth = ffn1_w_gate.shape[0]
    m = batch * seq
    kw = gla_w_gk_up.shape[-1]
    vw = d_model // 2
    a_cols = 2 * kw + 2 * vw
    code0 = a_cols
    hq0 = code0 + GLA_GATE_RANK
    hf0, hi0 = hq0 + vw, hq0 + 2 * vw

    h = x.reshape(m, d_model)
    u = _rmsnorm(h, ffn1_pre_norm[0:1])
    for l in range(depth):
        mid, w_down = _gateup(u, ffn1_w_gate[l:l + 1], ffn1_w_up[l:l + 1], ffn1_w_down[l:l + 1])
        h, u = _rows(mid, w_down, h, ffn1_post_norm[l:l + 1],
                     mix_pre_norm[l:l + 1], 0.5)

        w_l = jnp.swapaxes(w_in[l:l + 1], 1, 2)
        p_a = _proj(u, w_l, [(0, a_cols)], BF16)
        p_f, p_code = _proj(u, w_l, [(hf0, vw)], F32, side=(code0, LANES))
        p_r = _proj(u, w_l, [(hq0, vw), (hi0, 2 * vw + 2 * d_model)], BF16)
        wgk_pad = jnp.pad(gla_w_gk_up[l], ((0, LANES - GLA_GATE_RANK), (0, 0)))
        o_gla = _gla(p_a, p_code, wgk_pad, gla_b_gk[l:l + 1], gla_norm[l:l + 1], batch, seq)
        o_hgrn = _hgrn(p_r, p_f, hgrn_lb_logits, hgrn_norm[l:l + 1], l, batch, seq)
        merged, w_out_bf = _merge(o_gla, o_hgrn, p_r, 3 * vw, 3 * vw + d_model,
                                  w_branch_gla[l:l + 1], w_branch_hgrn[l:l + 1],
                                  b_branch_gates[l].reshape(2, 1, d_model), w_out[l:l + 1])
        h, u = _rows(merged, w_out_bf, h, mix_post_norm[l:l + 1],
                     ffn2_pre_norm[l:l + 1], 1.0, tm=512)

        mid, w_down = _gateup(u, ffn2_w_gate[l:l + 1], ffn2_w_up[l:l + 1], ffn2_w_down[l:l + 1])
        next_norm = ffn1_pre_norm[l + 1:l + 2] if l + 1 < depth else None
        h, u = _rows(mid, w_down, h, ffn2_post_norm[l:l + 1], next_norm, 0.5)
    return h.reshape(batch, seq, d_model)
```

```python
import functools

import jax
import jax.numpy as jnp
from jax import lax
from jax.experimental import pallas as pl
from jax.experimental.pallas import tpu as pltpu

F32 = jnp.float32
BF16 = jnp.bfloat16

EPS = 1e-6
CHUNK = 64
SUB = 16
GLA_HEADS = 4
GLA_GATE_RANK = 16
GLA_GATE_NORMALIZER = 16.0
HGRN_EXPAND = 128
CHUNK_UNROLL = 4
LOG2_E = 1.4426950408889634
EXP2_CLAMP = 115.0

LANES = 128
SUBLANES = 8
VMEM_LIMIT = 62 * 1024 * 1024
MATMUL_SUB_ROWS = 1024


def _cparams(n_axes):
    return pltpu.CompilerParams(
        dimension_semantics=("arbitrary",) * n_axes, vmem_limit_bytes=VMEM_LIMIT)


def _sigmoid(x):
    return 1.0 / (1.0 + jnp.exp2(x * (-LOG2_E)))


def _silu(x):
    return x * _sigmoid(x)


def _rms_scale(x):
    return lax.rsqrt(jnp.mean(x * x, axis=-1, keepdims=True) + EPS)


def _rmsnorm_kernel(x_ref, w_ref, o_ref):
    x = x_ref[...]
    o_ref[...] = (x * _rms_scale(x) * w_ref[...]).astype(o_ref.dtype)


def _rmsnorm(x, w, tm=512):
    m, d = x.shape
    return pl.pallas_call(
        _rmsnorm_kernel,
        out_shape=jax.ShapeDtypeStruct((m, d), BF16),
        grid=(m // tm,),
        in_specs=[pl.BlockSpec((tm, d), lambda i: (i, 0)),
                  pl.BlockSpec((1, d), lambda i: (0, 0))],
        out_specs=pl.BlockSpec((tm, d), lambda i: (i, 0)),
        compiler_params=_cparams(1),
        name="rmsnorm",
    )(x, w)


def _gateup_kernel(u_ref, wg_ref, wu_ref, wd_ref, o_ref, wd_bf_ref, wg_bf, wu_bf):
    @pl.when(pl.program_id(1) == 0)
    def _():
        wg_bf[...] = wg_ref[...].astype(BF16)
        wu_bf[...] = wu_ref[...].astype(BF16)
        wd_bf_ref[...] = wd_ref[...].astype(BF16)

    for r in range(0, u_ref.shape[0], MATMUL_SUB_ROWS):
        rows = slice(r, r + MATMUL_SUB_ROWS)
        u = u_ref[rows, :]
        g = jnp.dot(u, wg_bf[...], preferred_element_type=F32)
        up = jnp.dot(u, wu_bf[...], preferred_element_type=F32)
        o_ref[rows, :] = (_silu(g) * up).astype(o_ref.dtype)


def _gateup(u, w_gate, w_up, w_down, tm=2048, tn=512):
    m, d = u.shape
    f = w_gate.shape[-1]
    w_spec = pl.BlockSpec((None, d, tn), lambda j, i: (0, 0, j))
    return pl.pallas_call(
        _gateup_kernel,
        out_shape=[jax.ShapeDtypeStruct((m, f), BF16), jax.ShapeDtypeStruct((f, d), BF16)],
        grid=(pl.cdiv(f, tn), m // tm),
        in_specs=[pl.BlockSpec((tm, d), lambda j, i: (i, 0)), w_spec, w_spec,
                  pl.BlockSpec((None, tn, d), lambda j, i: (0, j, 0))],
        out_specs=[pl.BlockSpec((tm, tn), lambda j, i: (i, j)),
                   pl.BlockSpec((tn, d), lambda j, i: (j, 0))],
        scratch_shapes=[pltpu.VMEM((d, tn), BF16), pltpu.VMEM((d, tn), BF16)],
        compiler_params=_cparams(2),
        name="ffn_gateup",
    )(u, w_gate, w_up, w_down)


def _proj_kernel(u_ref, w_ref, *rest, shift, tn, side):
    rest = list(rest)
    wn_ref = rest.pop(0) if shift else None
    ws_ref = rest.pop(0) if side else None
    o_ref = rest.pop(0)
    os_ref = rest.pop(0) if side else None
    (w_bf,) = rest

    @pl.when(pl.program_id(1) == 0)
    def _():
        w = w_ref[...]
        if shift:
            w = jnp.concatenate([w, wn_ref[...]], axis=0)[shift:shift + tn]
        w_bf[...] = w.astype(BF16)

    sub = min(u_ref.shape[0], MATMUL_SUB_ROWS)
    for r in range(0, u_ref.shape[0], sub):
        o_ref[r:r + sub, :] = _dot_nt(u_ref[r:r + sub, :], w_bf[...]).astype(o_ref.dtype)
    if side:
        os_ref[...] = _dot_nt(u_ref[...], ws_ref[...].astype(BF16))


def _proj(u, wt, windows, out_dtype, tn=1024, side=None):
    m, d = u.shape
    shift = windows[0][0] % tn
    starts = []
    for c0, n in windows:
        assert c0 % tn == shift and n % tn == 0 and shift % SUBLANES == 0
        starts += [c0 // tn + t for t in range(n // tn)]
    n_tiles = len(starts)
    tm = 2048 if n_tiles >= 3 else 1024

    def blk(j):
        b = jnp.int32(starts[0])
        for t in range(1, n_tiles):
            b = jnp.where(j >= t, starts[t], b)
        return b

    in_specs = [pl.BlockSpec((tm, d), lambda j, i: (i, 0)),
                pl.BlockSpec((None, tn, d), lambda j, i: (0, blk(j), 0))]
    args = [u, wt]
    if shift:
        assert tn % shift == 0
        per_tile = tn // shift
        in_specs.append(pl.BlockSpec((None, shift, d), lambda j, i: (0, (blk(j) + 1) * per_tile, 0)))
        args.append(wt)
    out_shape = [jax.ShapeDtypeStruct((m, n_tiles * tn), out_dtype)]
    out_specs = [pl.BlockSpec((tm, tn), lambda j, i: (i, j))]
    if side:
        s0, sn = side
        assert n_tiles == 1 and s0 % sn == 0
        in_specs.append(pl.BlockSpec((None, sn, d), lambda j, i: (0, s0 // sn, 0)))
        args.append(wt)
        out_shape.append(jax.ShapeDtypeStruct((m, sn), F32))
        out_specs.append(pl.BlockSpec((tm, sn), lambda j, i: (i, 0)))
    outs = pl.pallas_call(
        functools.partial(_proj_kernel, shift=shift, tn=tn, side=bool(side)),
        out_shape=out_shape,
        grid=(n_tiles, m // tm),
        in_specs=in_specs,
        out_specs=out_specs,
        scratch_shapes=[pltpu.VMEM((tn, d), BF16)],
        compiler_params=_cparams(2),
        name="in_proj",
    )(*args)
    return outs if side else outs[0]


def _rows_kernel(x_ref, w_ref, res_ref, post_ref, *rest, tn, res_scale, emit_next):
    if emit_next:
        next_ref, h_ref, u_ref, acc_ref = rest
    else:
        h_ref, acc_ref = rest
    tm, d = h_ref.shape
    col_tiles = [slice(c, c + tn) for c in range(0, d, tn)]

    for r in range(0, tm, ROWS_SUB):
        rows = slice(r, r + ROWS_SUB)
        x = x_ref[rows, :]
        for cols in col_tiles:
            acc_ref[rows, cols] = jnp.dot(x, w_ref[:, cols], preferred_element_type=F32)

        ssq = None
        for cols in col_tiles:
            a = acc_ref[rows, cols]
            s = jnp.sum(a * a, axis=-1, keepdims=True)
            ssq = s if ssq is None else ssq + s
        scale = lax.rsqrt(ssq / d + EPS) * res_scale
        hsq = None
        for cols in col_tiles:
            h = res_ref[rows, cols] + acc_ref[rows, cols] * scale * post_ref[:, cols]
            h_ref[rows, cols] = h
            if emit_next:
                s = jnp.sum(h * h, axis=-1, keepdims=True)
                hsq = s if hsq is None else hsq + s
        if emit_next:
            nscale = lax.rsqrt(hsq / d + EPS)
            for cols in col_tiles:
                u_ref[rows, cols] = (h_ref[rows, cols] * nscale
                                     * next_ref[:, cols]).astype(u_ref.dtype)


ROWS_SUB = 256


def _rows(x, w, res, post_w, next_w, res_scale, tm=512, tn=512):
    m, k = x.shape
    d = w.shape[-1]
    emit_next = next_w is not None
    row_spec = pl.BlockSpec((tm, d), lambda i: (i, 0))
    vec_spec = pl.BlockSpec((1, d), lambda i: (0, 0))
    in_specs = [pl.BlockSpec((tm, k), lambda i: (i, 0)),
                pl.BlockSpec((k, d), lambda i: (0, 0), pipeline_mode=pl.Buffered(1)),
                row_spec, vec_spec]
    args = [x, w, res, post_w]
    out_shape = [jax.ShapeDtypeStruct((m, d), F32)]
    out_specs = [row_spec]
    if emit_next:
        in_specs.append(vec_spec)
        args.append(next_w)
        out_shape.append(jax.ShapeDtypeStruct((m, d), BF16))
        out_specs.append(row_spec)
    outs = pl.pallas_call(
        functools.partial(_rows_kernel, tn=tn, res_scale=res_scale, emit_next=emit_next),
        out_shape=out_shape,
        grid=(m // tm,),
        in_specs=in_specs,
        out_specs=out_specs,
        scratch_shapes=[pltpu.VMEM((tm, d), F32)],
        compiler_params=_cparams(1),
        name="rows_matmul_norm",
    )(*args)
    return outs if emit_next else (outs[0], None)


def _split_bf16(x):
    hi = x.astype(BF16)
    lo = (x - hi.astype(F32)).astype(BF16)
    return hi, lo


def _dot_nt(a, b):
    return lax.dot_general(a, b, (((1,), (1,)), ((), ())), preferred_element_type=F32)


def _dot_tn(a, b):
    return lax.dot_general(a, b, (((0,), (0,)), ((), ())), preferred_element_type=F32)


N_SUB = CHUNK // SUB


def _chunk_consts():
    i = lax.broadcasted_iota(jnp.int32, (CHUNK, CHUNK), 0)
    j = lax.broadcasted_iota(jnp.int32, (CHUNK, CHUNK), 1)
    lo = (i // SUB) * SUB
    hi = lo + SUB
    groups = [(j >= lo) & (j <= i), (j > i) & (j < hi), j < lo, j >= hi]
    sel = jnp.concatenate([jnp.where(g, 1.0, 0.0) for g in groups], axis=0).astype(BF16)
    return jnp.concatenate([sel, sel], axis=1), i >= j


def _decay_sums(sel2, g):
    g_hi, g_lo = _split_bf16(g)
    return jnp.dot(sel2, jnp.concatenate([g_hi, g_lo], axis=0), preferred_element_type=F32)


def _gated_chunks(heads, causal):
    prep = [_chunk_operands(*h) for h in heads]
    scores = [_dot_nt(p[0], p[1]) for p in prep]
    kv = [jnp.dot(p[4], p[3], preferred_element_type=F32) for p in prep]
    outs, states = [], []
    for (q_t, k_t, q_st, k_st, v_t, st_bf, st_decay), s, upd, h in zip(prep, scores, kv, heads):
        s = jnp.where(causal, s, 0.0).astype(BF16)
        outs.append(_dot_nt(jnp.concatenate([q_st, s], axis=1),
                            jnp.concatenate([st_bf, v_t], axis=1)))
        states.append(h[4] * st_decay + upd)
    return outs, states


def _chunk_operands(q, k, v, dec, st):
    within, rest, before, after = dec
    dk = q.shape[1]
    cum = within + before
    q_st = (q * jnp.exp2(cum)).astype(BF16)
    k_st = (k * jnp.exp2(rest + after)).astype(BF16)
    q_in = (q * jnp.exp2(within)).astype(BF16)
    k_diag = (k * jnp.exp2(jnp.minimum(-within, EXP2_CLAMP))).astype(BF16)
    k_end = k * jnp.exp2(rest)
    k_end_bf = k_end.astype(BF16)

    blk = lambda x, b: x[b * SUB:(b + 1) * SUB]
    whole = {b: jnp.exp2(within[(b + 1) * SUB - 1:(b + 1) * SUB]) for b in range(1, N_SUB - 1)}
    skip = {}
    for bq in range(N_SUB):
        for bk in range(bq - 1):
            d = whole[bk + 1]
            for mid in range(bk + 2, bq):
                d = d * whole[mid]
            skip[bq, bk] = d
    zero = jnp.zeros((SUB, dk), BF16)
    k_cols, q_cols = [], []
    for bq in range(N_SUB):
        col = []
        for bk in range(N_SUB):
            if bk > bq:
                col.append(zero)
            elif bk == bq:
                col.append(blk(k_diag, bk))
            elif bk == bq - 1:
                col.append(blk(k_end_bf, bk))
            else:
                col.append((blk(k_end, bk) * skip[bq, bk]).astype(BF16))
        k_cols.append(jnp.concatenate(col, axis=0))
        q_cols.append(jnp.concatenate([blk(q_in, b) if b == bq else zero for b in range(N_SUB)],
                                      axis=0))
    q_tilde = jnp.concatenate(q_cols, axis=1)
    k_tilde = jnp.concatenate(k_cols, axis=1)
    v_t = v.T
    return q_tilde, k_tilde, q_st, k_st, v_t, st.astype(BF16), jnp.exp2(cum[CHUNK - 1:CHUNK])


def _head_norm_gate(o, norm_w, gate):
    return (o * _rms_scale(o) * norm_w * _silu(gate.astype(F32))).astype(BF16)


def _gla_kernel(q_ref, k_ref, v_ref, go_ref, code_ref, wgk_ref, bgk_ref, gn_ref, o_ref, st_ref,
                sums_ref, *, n_chunks, dk, dv):
    @pl.when(pl.program_id(1) == 0)
    def _():
        st_ref[...] = jnp.zeros_like(st_ref)

    sel2, causal = _chunk_consts()
    norm_w = gn_ref[...]
    q_scale = dk ** -0.5

    w_hi, w_lo = _split_bf16(wgk_ref[...])
    c_hi, c_lo = _split_bf16(code_ref[...])
    z = (jnp.dot(c_hi, w_hi, preferred_element_type=F32)
         + jnp.dot(c_lo, w_hi, preferred_element_type=F32)
         + jnp.dot(c_hi, w_lo, preferred_element_type=F32)) + bgk_ref[...]
    log2_a = ((jnp.minimum(z, 0.0) - jnp.log1p(jnp.exp2(jnp.abs(z) * (-LOG2_E))))
              * (LOG2_E / GLA_GATE_NORMALIZER))
    for c in range(n_chunks):
        sums_ref[c] = _decay_sums(sel2, log2_a[c * CHUNK:(c + 1) * CHUNK])

    def chunk_body(c, carry):
        rows = pl.ds(pl.multiple_of(c * CHUNK, CHUNK), CHUNK)
        heads = []
        for h in range(GLA_HEADS):
            kc = slice(h * dk, (h + 1) * dk)
            q = q_ref[rows, kc].astype(F32) * q_scale
            k = k_ref[rows, kc].astype(F32)
            dec = [sums_ref[c, n * CHUNK:(n + 1) * CHUNK, kc] for n in range(4)]
            heads.append((q, k, v_ref[rows, h * dv:(h + 1) * dv], dec, st_ref[h]))
        outs, states = _gated_chunks(heads, causal)
        for h in range(GLA_HEADS):
            vc = slice(h * dv, (h + 1) * dv)
            st_ref[h] = states[h]
            o_ref[rows, vc] = _head_norm_gate(outs[h], norm_w, go_ref[rows, vc])
        return carry

    lax.fori_loop(0, n_chunks, chunk_body, 0, unroll=CHUNK_UNROLL)


def _gla(p_a, p_code, wgk_pad, b_gk, gla_norm, batch, seq, t_blk=512):
    kw = wgk_pad.shape[1]
    vw = (p_a.shape[1] - 2 * kw) // 2
    dk = kw // GLA_HEADS
    dv = vw // GLA_HEADS
    nt = seq // t_blk
    row = lambda b, t: b * nt + t
    return pl.pallas_call(
        functools.partial(_gla_kernel, n_chunks=t_blk // CHUNK, dk=dk, dv=dv),
        out_shape=jax.ShapeDtypeStruct((batch * seq, vw), BF16),
        grid=(batch, nt),
        in_specs=[pl.BlockSpec((t_blk, kw), lambda b, t: (row(b, t), 0)),
                  pl.BlockSpec((t_blk, kw), lambda b, t: (row(b, t), 1)),
                  pl.BlockSpec((t_blk, vw), lambda b, t: (row(b, t), 1)),
                  pl.BlockSpec((t_blk, vw), lambda b, t: (row(b, t), 2)),
                  pl.BlockSpec((t_blk, LANES), lambda b, t: (row(b, t), 0)),
                  pl.BlockSpec((LANES, kw), lambda b, t: (0, 0)),
                  pl.BlockSpec((1, kw), lambda b, t: (0, 0)),
                  pl.BlockSpec((1, dv), lambda b, t: (0, 0))],
        out_specs=pl.BlockSpec((t_blk, vw), lambda b, t: (row(b, t), 0)),
        scratch_shapes=[pltpu.VMEM((GLA_HEADS, dv, dk), F32),
                        pltpu.VMEM((t_blk // CHUNK, 4 * CHUNK, kw), F32)],
        compiler_params=_cparams(2),
        name="gla_mixer",
    )(p_a, p_a, p_a, p_a, p_code, wgk_pad, b_gk, gla_norm)


def _hgrn_kernel(hq_ref, hi_ref, ho_ref, hf_ref, lbl_ref, hn_ref, o_ref, st_ref,
                 *, n_chunks, n_heads, dk, layer):
    @pl.when(pl.program_id(1) == 0)
    def _():
        st_ref[...] = jnp.zeros_like(st_ref)

    sel2, causal = _chunk_consts()
    logits = lbl_ref[...]
    p = jnp.exp(logits - jnp.max(logits, axis=0, keepdims=True))
    p = p / jnp.sum(p, axis=0, keepdims=True)
    lb = jnp.sum(p[:layer + 1], axis=0, keepdims=True)
    one_m_lb = jnp.sum(p[layer + 1:], axis=0, keepdims=True)
    norm_w = hn_ref[...]

    def chunk_body(c, carry):
        rows = pl.ds(pl.multiple_of(c * CHUNK, CHUNK), CHUNK)
        hf = hf_ref[rows, :]
        e = jnp.exp2(jnp.abs(hf) * (-LOG2_E))
        inv = 1.0 / (1.0 + e)
        pos = hf >= 0.0
        sig = jnp.where(pos, inv, e * inv)
        sig_neg = jnp.where(pos, e * inv, inv)
        log2_f = jnp.log2(lb + one_m_lb * sig)
        k_all = one_m_lb * sig_neg
        sums = _decay_sums(sel2, log2_f)
        heads = []
        for h in range(n_heads):
            hc = slice(h * dk, (h + 1) * dk)
            q = _silu(hq_ref[rows, hc].astype(F32))
            dec = [sums[n * CHUNK:(n + 1) * CHUNK, hc] for n in range(4)]
            heads.append((q, k_all[:, hc], hi_ref[rows, hc], dec, st_ref[h]))
        outs, states = _gated_chunks(heads, causal)
        for h in range(n_heads):
            hc = slice(h * dk, (h + 1) * dk)
            st_ref[h] = states[h]
            o_ref[rows, hc] = _head_norm_gate(outs[h], norm_w, ho_ref[rows, hc])
        return carry

    lax.fori_loop(0, n_chunks, chunk_body, 0, unroll=CHUNK_UNROLL)


def _hgrn(p_r, p_f, lb_logits, hgrn_norm, layer, batch, seq, t_blk=512):
    w = p_f.shape[1]
    dk = HGRN_EXPAND
    n_heads = w // dk
    nt = seq // t_blk
    n_lb = lb_logits.shape[0]
    row = lambda b, t: b * nt + t
    return pl.pallas_call(
        functools.partial(_hgrn_kernel, n_chunks=t_blk // CHUNK, n_heads=n_heads, dk=dk,
                          layer=layer),
        out_shape=jax.ShapeDtypeStruct((batch * seq, w), BF16),
        grid=(batch, nt),
        in_specs=[pl.BlockSpec((t_blk, w), lambda b, t: (row(b, t), 0)),
                  pl.BlockSpec((t_blk, w), lambda b, t: (row(b, t), 1)),
                  pl.BlockSpec((t_blk, w), lambda b, t: (row(b, t), 2)),
                  pl.BlockSpec((t_blk, w), lambda b, t: (row(b, t), 0)),
                  pl.BlockSpec((n_lb, w), lambda b, t: (0, 0)),
                  pl.BlockSpec((1, dk), lambda b, t: (0, 0))],
        out_specs=pl.BlockSpec((t_blk, w), lambda b, t: (row(b, t), 0)),
        scratch_shapes=[pltpu.VMEM((n_heads, dk, dk), F32)],
        compiler_params=_cparams(2),
        name="hgrn_mixer",
    )(p_r, p_r, p_r, p_f, lb_logits, hgrn_norm)


def _merge_kernel(og_ref, oh_ref, zg_ref, zh_ref, wg_ref, wh_ref, bg_ref, bh_ref, wo_ref,
                  o_ref, wo_bf_ref, wg_bf, wh_bf):
    @pl.when(pl.program_id(1) == 0)
    def _():
        wg_bf[...] = wg_ref[...].astype(BF16)
        wh_bf[...] = wh_ref[...].astype(BF16)

    wo_bf_ref[...] = wo_ref[...].astype(BF16)
    a = jnp.dot(og_ref[...], wg_bf[...], preferred_element_type=F32)
    b = jnp.dot(oh_ref[...], wh_bf[...], preferred_element_type=F32)
    o_ref[...] = (_sigmoid(zg_ref[...].astype(F32) + bg_ref[...]) * a
                  + _sigmoid(zh_ref[...].astype(F32) + bh_ref[...]) * b).astype(o_ref.dtype)


def _merge(o_gla, o_hgrn, p_b, zg_col0, zh_col0, w_bg, w_bh, b_gates, w_out, tm=1024, tn=1024):
    m, kdim = o_gla.shape
    d = w_bg.shape[-1]
    ni = m // tm
    k_out, d_out = w_out.shape[1:]
    rb = k_out // ((d // tn) * ni)
    assert rb * (d // tn) * ni == k_out and rb % (2 * SUBLANES) == 0
    x_spec = pl.BlockSpec((tm, kdim), lambda j, i: (i, 0))
    w_spec = pl.BlockSpec((None, kdim, tn), lambda j, i: (0, 0, j))
    return pl.pallas_call(
        _merge_kernel,
        out_shape=[jax.ShapeDtypeStruct((m, d), BF16), jax.ShapeDtypeStruct((k_out, d_out), BF16)],
        grid=(d // tn, ni),
        in_specs=[x_spec, x_spec,
                  pl.BlockSpec((tm, tn), lambda j, i: (i, zg_col0 // tn + j)),
                  pl.BlockSpec((tm, tn), lambda j, i: (i, zh_col0 // tn + j)),
                  w_spec, w_spec,
                  pl.BlockSpec((None, 1, tn), lambda j, i: (0, 0, j)),
                  pl.BlockSpec((None, 1, tn), lambda j, i: (1, 0, j)),
                  pl.BlockSpec((None, rb, d_out), lambda j, i: (0, j * ni + i, 0))],
        out_specs=[pl.BlockSpec((tm, tn), lambda j, i: (i, j)),
                   pl.BlockSpec((rb, d_out), lambda j, i: (j * ni + i, 0))],
        scratch_shapes=[pltpu.VMEM((kdim, tn), BF16), pltpu.VMEM((kdim, tn), BF16)],
        compiler_params=_cparams(2),
        name="branch_merge",
    )(o_gla, o_hgrn, p_b, p_b, w_bg, w_bh, b_gates, b_gates, w_out)


def kernel(x, ffn1_pre_norm, ffn1_w_gate, ffn1_w_up, ffn1_w_down, ffn1_post_norm, mix_pre_norm, w_in, gla_w_gk_up, gla_b_gk, gla_norm, hgrn_lb_logits, hgrn_norm, w_branch_gla, w_branch_hgrn, b_branch_gates, w_out, mix_post_norm, ffn2_pre_norm, ffn2_w_gate, ffn2_w_up, ffn2_w_down, ffn2_post_norm):
    batch, seq, d_model = x.shape
    depth = ffn1_w_gate.shape[0]
    m = batch * seq
    kw = gla_w_gk_up.shape[-1]
    vw = d_model // 2
    a_cols = 2 * kw + 2 * vw
    code0 = a_cols
    hq0 = code0 + GLA_GATE_RANK
    hf0, hi0 = hq0 + vw, hq0 + 2 * vw

    h = x.reshape(m, d_model)
    u = _rmsnorm(h, ffn1_pre_norm[0:1])
    for l in range(depth):
        mid, w_down = _gateup(u, ffn1_w_gate[l:l + 1], ffn1_w_up[l:l + 1], ffn1_w_down[l:l + 1])
        h, u = _rows(mid, w_down, h, ffn1_post_norm[l:l + 1],
                     mix_pre_norm[l:l + 1], 0.5)

        w_l = jnp.swapaxes(w_in[l:l + 1], 1, 2)
        p_a = _proj(u, w_l, [(0, a_cols)], BF16)
        p_f, p_code = _proj(u, w_l, [(hf0, vw)], F32, side=(code0, LANES))
        p_r = _proj(u, w_l, [(hq0, vw), (hi0, 2 * vw + 2 * d_model)], BF16)
        wgk_pad = jnp.pad(gla_w_gk_up[l], ((0, LANES - GLA_GATE_RANK), (0, 0)))
        o_gla = _gla(p_a, p_code, wgk_pad, gla_b_gk[l:l + 1], gla_norm[l:l + 1], batch, seq)
        o_hgrn = _hgrn(p_r, p_f, hgrn_lb_logits, hgrn_norm[l:l + 1], l, batch, seq)
        merged, w_out_bf = _merge(o_gla, o_hgrn, p_r, 3 * vw, 3 * vw + d_model,
                                  w_branch_gla[l:l + 1], w_branch_hgrn[l:l + 1],
                                  b_branch_gates[l].reshape(2, 1, d_model), w_out[l:l + 1])
        h, u = _rows(merged, w_out_bf, h, mix_post_norm[l:l + 1],
                     ffn2_pre_norm[l:l + 1], 1.0, tm=512)

        mid, w_down = _gateup(u, ffn2_w_gate[l:l + 1], ffn2_w_up[l:l + 1], ffn2_w_down[l:l + 1])
        next_norm = ffn1_pre_norm[l + 1:l + 2] if l + 1 < depth else None
        h, u = _rows(mid, w_down, h, ffn2_post_norm[l:l + 1], next_norm, 0.5)
    return h.reshape(batch, seq, d_model)
```

```python
import functools

import jax
import jax.numpy as jnp
from jax import lax
from jax.experimental import pallas as pl
from jax.experimental.pallas import tpu as pltpu

F32 = jnp.float32
BF16 = jnp.bfloat16

EPS = 1e-6
CHUNK = 64
SUB = 16
GLA_HEADS = 4
GLA_GATE_RANK = 16
GLA_GATE_NORMALIZER = 16.0
HGRN_EXPAND = 128
CHUNK_UNROLL = 4
LOG2_E = 1.4426950408889634
EXP2_CLAMP = 115.0

LANES = 128
SUBLANES = 8
VMEM_LIMIT = 62 * 1024 * 1024
MATMUL_SUB_ROWS = 1024


def _cparams(n_axes):
    return pltpu.CompilerParams(
        dimension_semantics=("arbitrary",) * n_axes, vmem_limit_bytes=VMEM_LIMIT)


def _sigmoid(x):
    return 1.0 / (1.0 + jnp.exp2(x * (-LOG2_E)))


def _silu(x):
    return x * _sigmoid(x)


def _rms_scale(x):
    return lax.rsqrt(jnp.mean(x * x, axis=-1, keepdims=True) + EPS)


def _rmsnorm_kernel(x_ref, w_ref, o_ref):
    x = x_ref[...]
    o_ref[...] = (x * _rms_scale(x) * w_ref[...]).astype(o_ref.dtype)


def _rmsnorm(x, w, tm=512):
    m, d = x.shape
    return pl.pallas_call(
        _rmsnorm_kernel,
        out_shape=jax.ShapeDtypeStruct((m, d), BF16),
        grid=(m // tm,),
        in_specs=[pl.BlockSpec((tm, d), lambda i: (i, 0)),
                  pl.BlockSpec((1, d), lambda i: (0, 0))],
        out_specs=pl.BlockSpec((tm, d), lambda i: (i, 0)),
        compiler_params=_cparams(1),
        name="rmsnorm",
    )(x, w)


def _gateup_kernel(u_ref, wg_ref, wu_ref, wd_ref, o_ref, wd_bf_ref, wg_bf, wu_bf):
    @pl.when(pl.program_id(1) == 0)
    def _():
        wg_bf[...] = wg_ref[...].astype(BF16)
        wu_bf[...] = wu_ref[...].astype(BF16)
        wd_bf_ref[...] = wd_ref[...].astype(BF16)

    for r in range(0, u_ref.shape[0], MATMUL_SUB_ROWS):
        rows = slice(r, r + MATMUL_SUB_ROWS)
        u = u_ref[rows, :]
        g = jnp.dot(u, wg_bf[...], preferred_element_type=F32)
        up = jnp.dot(u, wu_bf[...], preferred_element_type=F32)
        o_ref[rows, :] = (_silu(g) * up).astype(o_ref.dtype)


def _gateup(u, w_gate, w_up, w_down, tm=2048, tn=512):
    m, d = u.shape
    f = w_gate.shape[-1]
    w_spec = pl.BlockSpec((None, d, tn), lambda j, i: (0, 0, j))
    return pl.pallas_call(
        _gateup_kernel,
        out_shape=[jax.ShapeDtypeStruct((m, f), BF16), jax.ShapeDtypeStruct((f, d), BF16)],
        grid=(pl.cdiv(f, tn), m // tm),
        in_specs=[pl.BlockSpec((tm, d), lambda j, i: (i, 0)), w_spec, w_spec,
                  pl.BlockSpec((None, tn, d), lambda j, i: (0, j, 0))],
        out_specs=[pl.BlockSpec((tm, tn), lambda j, i: (i, j)),
                   pl.BlockSpec((tn, d), lambda j, i: (j, 0))],
        scratch_shapes=[pltpu.VMEM((d, tn), BF16), pltpu.VMEM((d, tn), BF16)],
        compiler_params=_cparams(2),
        name="ffn_gateup",
    )(u, w_gate, w_up, w_down)


def _proj_kernel(u_ref, w_ref, *rest, shift, n_plain, tn, side):
    rest = list(rest)
    wn_ref = rest.pop(0) if shift else None
    ws_ref = rest.pop(0) if side else None
    o_ref = rest.pop(0)
    os_ref = rest.pop(0) if side else None
    (w_bf,) = rest
    j, i = pl.program_id(0), pl.program_id(1)

    @pl.when((i == 0) & (j < n_plain))
    def _():
        w_bf[...] = w_ref[...].astype(BF16)

    if shift:
        @pl.when((i == 0) & (j >= n_plain))
        def _():
            w = jnp.concatenate([w_ref[...], wn_ref[...]], axis=0)[shift:shift + tn]
            w_bf[...] = w.astype(BF16)

    sub = min(u_ref.shape[0], MATMUL_SUB_ROWS)
    for r in range(0, u_ref.shape[0], sub):
        o_ref[r:r + sub, :] = _dot_nt(u_ref[r:r + sub, :], w_bf[...]).astype(o_ref.dtype)
    if side:
        os_ref[...] = _dot_nt(u_ref[...], ws_ref[...].astype(BF16))


def _proj(u, wt, windows, out_dtype, tn=1024, side=None):
    m, d = u.shape
    shift = windows[-1][0] % tn
    starts = []
    n_plain = 0
    for c0, n in windows:
        assert c0 % tn in (0, shift) and n % tn == 0 and shift % SUBLANES == 0
        if c0 % tn == 0 and n_plain == len(starts):
            n_plain += n // tn
        else:
            assert c0 % tn == shift
        starts += [c0 // tn + t for t in range(n // tn)]
    n_tiles = len(starts)
    tm = 2048 if n_tiles >= 3 else 1024

    def blk(j):
        b = jnp.int32(starts[0])
        for t in range(1, n_tiles):
            b = jnp.where(j >= t, starts[t], b)
        return b

    in_specs = [pl.BlockSpec((tm, d), lambda j, i: (i, 0)),
                pl.BlockSpec((None, tn, d), lambda j, i: (0, blk(j), 0))]
    args = [u, wt]
    if shift:
        assert tn % shift == 0
        per_tile = tn // shift
        in_specs.append(pl.BlockSpec((None, shift, d), lambda j, i: (0, (blk(j) + 1) * per_tile, 0)))
        args.append(wt)
    out_shape = [jax.ShapeDtypeStruct((m, n_tiles * tn), out_dtype)]
    out_specs = [pl.BlockSpec((tm, tn), lambda j, i: (i, j))]
    if side:
        s0, sn = side
        assert n_tiles == 1 and s0 % sn == 0
        in_specs.append(pl.BlockSpec((None, sn, d), lambda j, i: (0, s0 // sn, 0)))
        args.append(wt)
        out_shape.append(jax.ShapeDtypeStruct((m, sn), F32))
        out_specs.append(pl.BlockSpec((tm, sn), lambda j, i: (i, 0)))
    outs = pl.pallas_call(
        functools.partial(_proj_kernel, shift=shift, n_plain=n_plain, tn=tn, side=bool(side)),
        out_shape=out_shape,
        grid=(n_tiles, m // tm),
        in_specs=in_specs,
        out_specs=out_specs,
        scratch_shapes=[pltpu.VMEM((tn, d), BF16)],
        compiler_params=_cparams(2),
        name="in_proj",
    )(*args)
    return outs if side else outs[0]


def _rows_kernel(x_ref, w_ref, res_ref, post_ref, *rest, tn, res_scale, emit_next):
    if emit_next:
        next_ref, h_ref, u_ref, acc_ref = rest
    else:
        h_ref, acc_ref = rest
    tm, d = h_ref.shape
    col_tiles = [slice(c, c + tn) for c in range(0, d, tn)]

    for r in range(0, tm, ROWS_SUB):
        rows = slice(r, r + ROWS_SUB)
        x = x_ref[rows, :]
        for cols in col_tiles:
            acc_ref[rows, cols] = jnp.dot(x, w_ref[:, cols], preferred_element_type=F32)

        ssq = None
        for cols in col_tiles:
            a = acc_ref[rows, cols]
            s = jnp.sum(a * a, axis=-1, keepdims=True)
            ssq = s if ssq is None else ssq + s
        scale = lax.rsqrt(ssq / d + EPS) * res_scale
        hsq = None
        for cols in col_tiles:
            h = res_ref[rows, cols] + acc_ref[rows, cols] * scale * post_ref[:, cols]
            h_ref[rows, cols] = h
            if emit_next:
                s = jnp.sum(h * h, axis=-1, keepdims=True)
                hsq = s if hsq is None else hsq + s
        if emit_next:
            nscale = lax.rsqrt(hsq / d + EPS)
            for cols in col_tiles:
                u_ref[rows, cols] = (h_ref[rows, cols] * nscale
                                     * next_ref[:, cols]).astype(u_ref.dtype)


ROWS_SUB = 256


def _rows(x, w, res, post_w, next_w, res_scale, tm=512, tn=512):
    m, k = x.shape
    d = w.shape[-1]
    emit_next = next_w is not None
    row_spec = pl.BlockSpec((tm, d), lambda i: (i, 0))
    vec_spec = pl.BlockSpec((1, d), lambda i: (0, 0))
    in_specs = [pl.BlockSpec((tm, k), lambda i: (i, 0)),
                pl.BlockSpec((k, d), lambda i: (0, 0), pipeline_mode=pl.Buffered(1)),
                row_spec, vec_spec]
    args = [x, w, res, post_w]
    out_shape = [jax.ShapeDtypeStruct((m, d), F32)]
    out_specs = [row_spec]
    if emit_next:
        in_specs.append(vec_spec)
        args.append(next_w)
        out_shape.append(jax.ShapeDtypeStruct((m, d), BF16))
        out_specs.append(row_spec)
    outs = pl.pallas_call(
        functools.partial(_rows_kernel, tn=tn, res_scale=res_scale, emit_next=emit_next),
        out_shape=out_shape,
        grid=(m // tm,),
        in_specs=in_specs,
        out_specs=out_specs,
        scratch_shapes=[pltpu.VMEM((tm, d), F32)],
        compiler_params=_cparams(1),
        name="rows_matmul_norm",
    )(*args)
    return outs if emit_next else (outs[0], None)


def _split_bf16(x):
    hi = x.astype(BF16)
    lo = (x - hi.astype(F32)).astype(BF16)
    return hi, lo


def _dot_nt(a, b):
    return lax.dot_general(a, b, (((1,), (1,)), ((), ())), preferred_element_type=F32)


def _dot_tn(a, b):
    return lax.dot_general(a, b, (((0,), (0,)), ((), ())), preferred_element_type=F32)


N_SUB = CHUNK // SUB


def _chunk_consts():
    i = lax.broadcasted_iota(jnp.int32, (CHUNK, CHUNK), 0)
    j = lax.broadcasted_iota(jnp.int32, (CHUNK, CHUNK), 1)
    lo = (i // SUB) * SUB
    hi = lo + SUB
    groups = [(j >= lo) & (j <= i), (j > i) & (j < hi), j < lo, j >= hi]
    sel = jnp.concatenate([jnp.where(g, 1.0, 0.0) for g in groups], axis=0).astype(BF16)
    return jnp.concatenate([sel, sel], axis=1), i >= j


def _decay_sums(sel2, g):
    g_hi, g_lo = _split_bf16(g)
    return jnp.dot(sel2, jnp.concatenate([g_hi, g_lo], axis=0), preferred_element_type=F32)


def _gated_chunks(heads, causal):
    prep = [_chunk_operands(*h) for h in heads]
    scores = [_dot_nt(p[0], p[1]) for p in prep]
    kv = [jnp.dot(p[4], p[3], preferred_element_type=F32) for p in prep]
    outs, states = [], []
    for (q_t, k_t, q_st, k_st, v_t, st_bf, st_decay), s, upd, h in zip(prep, scores, kv, heads):
        s = jnp.where(causal, s, 0.0).astype(BF16)
        outs.append(_dot_nt(jnp.concatenate([q_st, s], axis=1),
                            jnp.concatenate([st_bf, v_t], axis=1)))
        states.append(h[4] * st_decay + upd)
    return outs, states


def _chunk_operands(q, k, v, dec, st):
    within, rest, before, after = dec
    dk = q.shape[1]
    cum = within + before
    q_st = (q * jnp.exp2(cum)).astype(BF16)
    k_st = (k * jnp.exp2(rest + after)).astype(BF16)
    q_in = (q * jnp.exp2(within)).astype(BF16)
    k_diag = (k * jnp.exp2(jnp.minimum(-within, EXP2_CLAMP))).astype(BF16)
    k_end = k * jnp.exp2(rest)
    k_end_bf = k_end.astype(BF16)

    blk = lambda x, b: x[b * SUB:(b + 1) * SUB]
    whole = {b: jnp.exp2(within[(b + 1) * SUB - 1:(b + 1) * SUB]) for b in range(1, N_SUB - 1)}
    skip = {}
    for bq in range(N_SUB):
        for bk in range(bq - 1):
            d = whole[bk + 1]
            for mid in range(bk + 2, bq):
                d = d * whole[mid]
            skip[bq, bk] = d
    zero = jnp.zeros((SUB, dk), BF16)
    k_cols, q_cols = [], []
    for bq in range(N_SUB):
        col = []
        for bk in range(N_SUB):
            if bk > bq:
                col.append(zero)
            elif bk == bq:
                col.append(blk(k_diag, bk))
            elif bk == bq - 1:
                col.append(blk(k_end_bf, bk))
            else:
                col.append((blk(k_end, bk) * skip[bq, bk]).astype(BF16))
        k_cols.append(jnp.concatenate(col, axis=0))
        q_cols.append(jnp.concatenate([blk(q_in, b) if b == bq else zero for b in range(N_SUB)],
                                      axis=0))
    q_tilde = jnp.concatenate(q_cols, axis=1)
    k_tilde = jnp.concatenate(k_cols, axis=1)
    v_t = v.T
    return q_tilde, k_tilde, q_st, k_st, v_t, st.astype(BF16), jnp.exp2(cum[CHUNK - 1:CHUNK])


def _head_norm_gate(o, norm_w, gate):
    return (o * _rms_scale(o) * norm_w * _silu(gate.astype(F32))).astype(BF16)


def _gla_kernel(q_ref, k_ref, v_ref, go_ref, code_ref, wgk_ref, bgk_ref, gn_ref, o_ref, st_ref,
                sums_ref, *, n_chunks, dk, dv):
    @pl.when(pl.program_id(1) == 0)
    def _():
        st_ref[...] = jnp.zeros_like(st_ref)

    sel2, causal = _chunk_consts()
    norm_w = gn_ref[...]
    q_scale = dk ** -0.5

    w_hi, w_lo = _split_bf16(wgk_ref[...])
    c_hi, c_lo = _split_bf16(code_ref[...])
    z = (jnp.dot(c_hi, w_hi, preferred_element_type=F32)
         + jnp.dot(c_lo, w_hi, preferred_element_type=F32)
         + jnp.dot(c_hi, w_lo, preferred_element_type=F32)) + bgk_ref[...]
    log2_a = ((jnp.minimum(z, 0.0) - jnp.log1p(jnp.exp2(jnp.abs(z) * (-LOG2_E))))
              * (LOG2_E / GLA_GATE_NORMALIZER))
    for c in range(n_chunks):
        sums_ref[c] = _decay_sums(sel2, log2_a[c * CHUNK:(c + 1) * CHUNK])

    def chunk_body(c, carry):
        rows = pl.ds(pl.multiple_of(c * CHUNK, CHUNK), CHUNK)
        heads = []
        for h in range(GLA_HEADS):
            kc = slice(h * dk, (h + 1) * dk)
            q = q_ref[rows, kc].astype(F32) * q_scale
            k = k_ref[rows, kc].astype(F32)
            dec = [sums_ref[c, n * CHUNK:(n + 1) * CHUNK, kc] for n in range(4)]
            heads.append((q, k, v_ref[rows, h * dv:(h + 1) * dv], dec, st_ref[h]))
        outs, states = _gated_chunks(heads, causal)
        for h in range(GLA_HEADS):
            vc = slice(h * dv, (h + 1) * dv)
            st_ref[h] = states[h]
            o_ref[rows, vc] = _head_norm_gate(outs[h], norm_w, go_ref[rows, vc])
        return carry

    lax.fori_loop(0, n_chunks, chunk_body, 0, unroll=CHUNK_UNROLL)


def _gla(p_a, p_code, wgk_pad, b_gk, gla_norm, vw, batch, seq, t_blk=512):
    kw = wgk_pad.shape[1]
    dk = kw // GLA_HEADS
    dv = vw // GLA_HEADS
    nt = seq // t_blk
    row = lambda b, t: b * nt + t
    return pl.pallas_call(
        functools.partial(_gla_kernel, n_chunks=t_blk // CHUNK, dk=dk, dv=dv),
        out_shape=jax.ShapeDtypeStruct((batch * seq, vw), BF16),
        grid=(batch, nt),
        in_specs=[pl.BlockSpec((t_blk, kw), lambda b, t: (row(b, t), 0)),
                  pl.BlockSpec((t_blk, kw), lambda b, t: (row(b, t), 1)),
                  pl.BlockSpec((t_blk, vw), lambda b, t: (row(b, t), 1)),
                  pl.BlockSpec((t_blk, vw), lambda b, t: (row(b, t), 2)),
                  pl.BlockSpec((t_blk, LANES), lambda b, t: (row(b, t), 0)),
                  pl.BlockSpec((LANES, kw), lambda b, t: (0, 0)),
                  pl.BlockSpec((1, kw), lambda b, t: (0, 0)),
                  pl.BlockSpec((1, dv), lambda b, t: (0, 0))],
        out_specs=pl.BlockSpec((t_blk, vw), lambda b, t: (row(b, t), 0)),
        scratch_shapes=[pltpu.VMEM((GLA_HEADS, dv, dk), F32),
                        pltpu.VMEM((t_blk // CHUNK, 4 * CHUNK, kw), F32)],
        compiler_params=_cparams(2),
        name="gla_mixer",
    )(p_a, p_a, p_a, p_a, p_code, wgk_pad, b_gk, gla_norm)


def _hgrn_kernel(hq_ref, hi_ref, ho_ref, hf_ref, lbl_ref, hn_ref, o_ref, st_ref,
                 *, n_chunks, n_heads, dk, layer):
    @pl.when(pl.program_id(1) == 0)
    def _():
        st_ref[...] = jnp.zeros_like(st_ref)

    sel2, causal = _chunk_consts()
    logits = lbl_ref[...]
    p = jnp.exp(logits - jnp.max(logits, axis=0, keepdims=True))
    p = p / jnp.sum(p, axis=0, keepdims=True)
    lb = jnp.sum(p[:layer + 1], axis=0, keepdims=True)
    one_m_lb = jnp.sum(p[layer + 1:], axis=0, keepdims=True)
    norm_w = hn_ref[...]

    def chunk_body(c, carry):
        rows = pl.ds(pl.multiple_of(c * CHUNK, CHUNK), CHUNK)
        hf = hf_ref[rows, :]
        e = jnp.exp2(jnp.abs(hf) * (-LOG2_E))
        inv = 1.0 / (1.0 + e)
        pos = hf >= 0.0
        sig = jnp.where(pos, inv, e * inv)
        sig_neg = jnp.where(pos, e * inv, inv)
        log2_f = jnp.log2(lb + one_m_lb * sig)
        k_all = one_m_lb * sig_neg
        sums = _decay_sums(sel2, log2_f)
        heads = []
        for h in range(n_heads):
            hc = slice(h * dk, (h + 1) * dk)
            q = _silu(hq_ref[rows, hc].astype(F32))
            dec = [sums[n * CHUNK:(n + 1) * CHUNK, hc] for n in range(4)]
            heads.append((q, k_all[:, hc], hi_ref[rows, hc], dec, st_ref[h]))
        outs, states = _gated_chunks(heads, causal)
        for h in range(n_heads):
            hc = slice(h * dk, (h + 1) * dk)
            st_ref[h] = states[h]
            o_ref[rows, hc] = _head_norm_gate(outs[h], norm_w, ho_ref[rows, hc])
        return carry

    lax.fori_loop(0, n_chunks, chunk_body, 0, unroll=CHUNK_UNROLL)


def _hgrn(p_r, q_col0, p_f, lb_logits, hgrn_norm, layer, batch, seq, t_blk=512):
    w = p_f.shape[1]
    dk = HGRN_EXPAND
    n_heads = w // dk
    nt = seq // t_blk
    n_lb = lb_logits.shape[0]
    row = lambda b, t: b * nt + t
    c0 = q_col0 // w
    assert c0 * w == q_col0
    return pl.pallas_call(
        functools.partial(_hgrn_kernel, n_chunks=t_blk // CHUNK, n_heads=n_heads, dk=dk,
                          layer=layer),
        out_shape=jax.ShapeDtypeStruct((batch * seq, w), BF16),
        grid=(batch, nt),
        in_specs=[pl.BlockSpec((t_blk, w), lambda b, t: (row(b, t), c0)),
                  pl.BlockSpec((t_blk, w), lambda b, t: (row(b, t), c0 + 1)),
                  pl.BlockSpec((t_blk, w), lambda b, t: (row(b, t), c0 + 2)),
                  pl.BlockSpec((t_blk, w), lambda b, t: (row(b, t), 0)),
                  pl.BlockSpec((n_lb, w), lambda b, t: (0, 0)),
                  pl.BlockSpec((1, dk), lambda b, t: (0, 0))],
        out_specs=pl.BlockSpec((t_blk, w), lambda b, t: (row(b, t), 0)),
        scratch_shapes=[pltpu.VMEM((n_heads, dk, dk), F32)],
        compiler_params=_cparams(2),
        name="hgrn_mixer",
    )(p_r, p_r, p_r, p_f, lb_logits, hgrn_norm)


def _merge_kernel(og_ref, oh_ref, zg_ref, zh_ref, wg_ref, wh_ref, bg_ref, bh_ref, wo_ref,
                  o_ref, wo_bf_ref, wg_bf, wh_bf):
    @pl.when(pl.program_id(1) == 0)
    def _():
        wg_bf[...] = wg_ref[...].astype(BF16)
        wh_bf[...] = wh_ref[...].astype(BF16)

    wo_bf_ref[...] = wo_ref[...].astype(BF16)
    a = jnp.dot(og_ref[...], wg_bf[...], preferred_element_type=F32)
    b = jnp.dot(oh_ref[...], wh_bf[...], preferred_element_type=F32)
    o_ref[...] = (_sigmoid(zg_ref[...].astype(F32) + bg_ref[...]) * a
                  + _sigmoid(zh_ref[...].astype(F32) + bh_ref[...]) * b).astype(o_ref.dtype)


def _merge(o_gla, o_hgrn, p_b, zg_col0, zh_col0, w_bg, w_bh, b_gates, w_out, tm=1024, tn=1024):
    m, kdim = o_gla.shape
    d = w_bg.shape[-1]
    ni = m // tm
    k_out, d_out = w_out.shape[1:]
    rb = k_out // ((d // tn) * ni)
    assert rb * (d // tn) * ni == k_out and rb % (2 * SUBLANES) == 0
    x_spec = pl.BlockSpec((tm, kdim), lambda j, i: (i, 0))
    w_spec = pl.BlockSpec((None, kdim, tn), lambda j, i: (0, 0, j))
    return pl.pallas_call(
        _merge_kernel,
        out_shape=[jax.ShapeDtypeStruct((m, d), BF16), jax.ShapeDtypeStruct((k_out, d_out), BF16)],
        grid=(d // tn, ni),
        in_specs=[x_spec, x_spec,
                  pl.BlockSpec((tm, tn), lambda j, i: (i, zg_col0 // tn + j)),
                  pl.BlockSpec((tm, tn), lambda j, i: (i, zh_col0 // tn + j)),
                  w_spec, w_spec,
                  pl.BlockSpec((None, 1, tn), lambda j, i: (0, 0, j)),
                  pl.BlockSpec((None, 1, tn), lambda j, i: (1, 0, j)),
                  pl.BlockSpec((None, rb, d_out), lambda j, i: (0, j * ni + i, 0))],
        out_specs=[pl.BlockSpec((tm, tn), lambda j, i: (i, j)),
                   pl.BlockSpec((rb, d_out), lambda j, i: (j * ni + i, 0))],
        scratch_shapes=[pltpu.VMEM((kdim, tn), BF16), pltpu.VMEM((kdim, tn), BF16)],
        compiler_params=_cparams(2),
        name="branch_merge",
    )(o_gla, o_hgrn, p_b, p_b, w_bg, w_bh, b_gates, b_gates, w_out)


def kernel(x, ffn1_pre_norm, ffn1_w_gate, ffn1_w_up, ffn1_w_down, ffn1_post_norm, mix_pre_norm, w_in, gla_w_gk_up, gla_b_gk, gla_norm, hgrn_lb_logits, hgrn_norm, w_branch_gla, w_branch_hgrn, b_branch_gates, w_out, mix_post_norm, ffn2_pre_norm, ffn2_w_gate, ffn2_w_up, ffn2_w_down, ffn2_post_norm):
    batch, seq, d_model = x.shape
    depth = ffn1_w_gate.shape[0]
    m = batch * seq
    kw = gla_w_gk_up.shape[-1]
    vw = d_model // 2
    a_cols = 2 * kw + 2 * vw
    code0 = a_cols
    hq0 = code0 + GLA_GATE_RANK
    hf0, hi0 = hq0 + vw, hq0 + 2 * vw

    h = x.reshape(m, d_model)
    u = _rmsnorm(h, ffn1_pre_norm[0:1])
    for l in range(depth):
        mid, w_down = _gateup(u, ffn1_w_gate[l:l + 1], ffn1_w_up[l:l + 1], ffn1_w_down[l:l + 1])
        h, u = _rows(mid, w_down, h, ffn1_post_norm[l:l + 1],
                     mix_pre_norm[l:l + 1], 0.5)

        w_l = jnp.swapaxes(w_in[l:l + 1], 1, 2)
        p_f, p_code = _proj(u, w_l, [(hf0, vw)], F32, side=(code0, LANES))
        p = _proj(u, w_l, [(0, a_cols), (hq0, vw), (hi0, 2 * vw + 2 * d_model)], BF16)
        wgk_pad = jnp.pad(gla_w_gk_up[l], ((0, LANES - GLA_GATE_RANK), (0, 0)))
        o_gla = _gla(p, p_code, wgk_pad, gla_b_gk[l:l + 1], gla_norm[l:l + 1], vw, batch, seq)
        o_hgrn = _hgrn(p, a_cols, p_f, hgrn_lb_logits, hgrn_norm[l:l + 1], l, batch, seq)
        merged, w_out_bf = _merge(o_gla, o_hgrn, p, a_cols + 3 * vw, a_cols + 3 * vw + d_model,
                                  w_branch_gla[l:l + 1], w_branch_hgrn[l:l + 1],
                                  b_branch_gates[l].reshape(2, 1, d_model), w_out[l:l + 1])
        h, u = _rows(merged, w_out_bf, h, mix_post_norm[l:l + 1],
                     ffn2_pre_norm[l:l + 1], 1.0, tm=512)

        mid, w_down = _gateup(u, ffn2_w_gate[l:l + 1], ffn2_w_up[l:l + 1], ffn2_w_down[l:l + 1])
        next_norm = ffn1_pre_norm[l + 1:l + 2] if l + 1 < depth else None
        h, u = _rows(mid, w_down, h, ffn2_post_norm[l:l + 1], next_norm, 0.5)
    return h.reshape(batch, seq, d_model)
```

```python
import functools

import jax
import jax.numpy as jnp
from jax import lax
from jax.experimental import pallas as pl
from jax.experimental.pallas import tpu as pltpu

F32 = jnp.float32
BF16 = jnp.bfloat16

EPS = 1e-6
CHUNK = 64
SUB = 16
GLA_HEADS = 4
GLA_GATE_RANK = 16
GLA_GATE_NORMALIZER = 16.0
HGRN_EXPAND = 128
CHUNK_UNROLL = 4
LOG2_E = 1.4426950408889634
EXP2_CLAMP = 115.0

LANES = 128
SUBLANES = 8
VMEM_LIMIT = 62 * 1024 * 1024
MATMUL_SUB_ROWS = 1024


def _cparams(n_axes):
    return pltpu.CompilerParams(
        dimension_semantics=("arbitrary",) * n_axes, vmem_limit_bytes=VMEM_LIMIT)


def _sigmoid(x):
    return 1.0 / (1.0 + jnp.exp2(x * (-LOG2_E)))


def _silu(x):
    return x * _sigmoid(x)


def _fold_lanes(x):
    acc = x[:, :LANES]
    for c in range(LANES, x.shape[1], LANES):
        acc = acc + x[:, c:c + LANES]
    return acc


def _rms_scale(x):
    ssq = jnp.sum(_fold_lanes(x * x), axis=-1, keepdims=True)
    return lax.rsqrt(ssq / x.shape[-1] + EPS)


def _rmsnorm_kernel(x_ref, w_ref, o_ref):
    x = x_ref[...]
    o_ref[...] = (x * _rms_scale(x) * w_ref[...]).astype(o_ref.dtype)


def _rmsnorm(x, w, tm=1024):
    m, d = x.shape
    return pl.pallas_call(
        _rmsnorm_kernel,
        out_shape=jax.ShapeDtypeStruct((m, d), BF16),
        grid=(m // tm,),
        in_specs=[pl.BlockSpec((tm, d), lambda i: (i, 0)),
                  pl.BlockSpec((1, d), lambda i: (0, 0))],
        out_specs=pl.BlockSpec((tm, d), lambda i: (i, 0)),
        compiler_params=_cparams(1),
        name="rmsnorm",
    )(x, w)


def _gateup_kernel(u_ref, wg_ref, wu_ref, wd_ref, o_ref, wd_bf_ref, wg_bf, wu_bf):
    @pl.when(pl.program_id(1) == 0)
    def _():
        wg_bf[...] = wg_ref[...].astype(BF16)
        wu_bf[...] = wu_ref[...].astype(BF16)
        wd_bf_ref[...] = wd_ref[...].astype(BF16)

    for r in range(0, u_ref.shape[0], MATMUL_SUB_ROWS):
        rows = slice(r, r + MATMUL_SUB_ROWS)
        u = u_ref[rows, :]
        g = jnp.dot(u, wg_bf[...], preferred_element_type=F32)
        up = jnp.dot(u, wu_bf[...], preferred_element_type=F32)
        o_ref[rows, :] = (_silu(g) * up).astype(o_ref.dtype)


def _gateup(u, w_gate, w_up, w_down, tm=2048, tn=512):
    m, d = u.shape
    f = w_gate.shape[-1]
    w_spec = pl.BlockSpec((None, d, tn), lambda j, i: (0, 0, j))
    return pl.pallas_call(
        _gateup_kernel,
        out_shape=[jax.ShapeDtypeStruct((m, f), BF16), jax.ShapeDtypeStruct((f, d), BF16)],
        grid=(pl.cdiv(f, tn), m // tm),
        in_specs=[pl.BlockSpec((tm, d), lambda j, i: (i, 0)), w_spec, w_spec,
                  pl.BlockSpec((None, tn, d), lambda j, i: (0, j, 0))],
        out_specs=[pl.BlockSpec((tm, tn), lambda j, i: (i, j)),
                   pl.BlockSpec((tn, d), lambda j, i: (j, 0))],
        scratch_shapes=[pltpu.VMEM((d, tn), BF16), pltpu.VMEM((d, tn), BF16)],
        compiler_params=_cparams(2),
        name="ffn_gateup",
    )(u, w_gate, w_up, w_down)


def _proj_kernel(u_ref, w_ref, *rest, shift, n_plain, tn, side):
    rest = list(rest)
    wn_ref = rest.pop(0) if shift else None
    ws_ref = rest.pop(0) if side else None
    o_ref = rest.pop(0)
    os_ref = rest.pop(0) if side else None
    (w_bf,) = rest
    j, i = pl.program_id(0), pl.program_id(1)

    @pl.when((i == 0) & (j < n_plain))
    def _():
        w_bf[...] = w_ref[...].astype(BF16)

    if shift:
        @pl.when((i == 0) & (j >= n_plain))
        def _():
            w = jnp.concatenate([w_ref[...], wn_ref[...]], axis=0)[shift:shift + tn]
            w_bf[...] = w.astype(BF16)

    sub = min(u_ref.shape[0], MATMUL_SUB_ROWS)
    for r in range(0, u_ref.shape[0], sub):
        o_ref[r:r + sub, :] = _dot_nt(u_ref[r:r + sub, :], w_bf[...]).astype(o_ref.dtype)
    if side:
        os_ref[...] = _dot_nt(u_ref[...], ws_ref[...].astype(BF16))


def _proj(u, wt, windows, out_dtype, tn=1024, side=None):
    m, d = u.shape
    shift = windows[-1][0] % tn
    starts = []
    n_plain = 0
    for c0, n in windows:
        assert c0 % tn in (0, shift) and n % tn == 0 and shift % SUBLANES == 0
        if c0 % tn == 0 and n_plain == len(starts):
            n_plain += n // tn
        else:
            assert c0 % tn == shift
        starts += [c0 // tn + t for t in range(n // tn)]
    n_tiles = len(starts)
    tm = 2048 if n_tiles >= 3 else 1024

    def blk(j):
        b = jnp.int32(starts[0])
        for t in range(1, n_tiles):
            b = jnp.where(j >= t, starts[t], b)
        return b

    in_specs = [pl.BlockSpec((tm, d), lambda j, i: (i, 0)),
                pl.BlockSpec((None, tn, d), lambda j, i: (0, blk(j), 0))]
    args = [u, wt]
    if shift:
        assert tn % shift == 0
        per_tile = tn // shift
        in_specs.append(pl.BlockSpec((None, shift, d), lambda j, i: (0, (blk(j) + 1) * per_tile, 0)))
        args.append(wt)
    out_shape = [jax.ShapeDtypeStruct((m, n_tiles * tn), out_dtype)]
    out_specs = [pl.BlockSpec((tm, tn), lambda j, i: (i, j))]
    if side:
        s0, sn = side
        assert n_tiles == 1 and s0 % sn == 0
        in_specs.append(pl.BlockSpec((None, sn, d), lambda j, i: (0, s0 // sn, 0)))
        args.append(wt)
        out_shape.append(jax.ShapeDtypeStruct((m, sn), F32))
        out_specs.append(pl.BlockSpec((tm, sn), lambda j, i: (i, 0)))
    outs = pl.pallas_call(
        functools.partial(_proj_kernel, shift=shift, n_plain=n_plain, tn=tn, side=bool(side)),
        out_shape=out_shape,
        grid=(n_tiles, m // tm),
        in_specs=in_specs,
        out_specs=out_specs,
        scratch_shapes=[pltpu.VMEM((tn, d), BF16)],
        compiler_params=_cparams(2),
        name="in_proj",
    )(*args)
    return outs if side else outs[0]


def _rows_kernel(x_ref, w_ref, res_ref, post_ref, *rest, tn, res_scale, emit_next):
    if emit_next:
        next_ref, h_ref, u_ref, acc_ref = rest
    else:
        h_ref, acc_ref = rest
    tm, d = h_ref.shape
    col_tiles = [slice(c, c + tn) for c in range(0, d, tn)]

    for r in range(0, tm, ROWS_SUB):
        rows = slice(r, r + ROWS_SUB)
        x = x_ref[rows, :]
        for cols in col_tiles:
            acc_ref[rows, cols] = jnp.dot(x, w_ref[:, cols], preferred_element_type=F32)

        ssq = None
        for cols in col_tiles:
            a = acc_ref[rows, cols]
            s = _fold_lanes(a * a)
            ssq = s if ssq is None else ssq + s
        ssq = jnp.sum(ssq, axis=-1, keepdims=True)
        scale = lax.rsqrt(ssq / d + EPS) * res_scale
        hsq = None
        for cols in col_tiles:
            h = res_ref[rows, cols] + acc_ref[rows, cols] * scale * post_ref[:, cols]
            h_ref[rows, cols] = h
            if emit_next:
                s = _fold_lanes(h * h)
                hsq = s if hsq is None else hsq + s
        if emit_next:
            hsq = jnp.sum(hsq, axis=-1, keepdims=True)
            nscale = lax.rsqrt(hsq / d + EPS)
            for cols in col_tiles:
                u_ref[rows, cols] = (h_ref[rows, cols] * nscale
                                     * next_ref[:, cols]).astype(u_ref.dtype)


ROWS_SUB = 256


def _rows(x, w, res, post_w, next_w, res_scale, tm=512, tn=512):
    m, k = x.shape
    d = w.shape[-1]
    emit_next = next_w is not None
    row_spec = pl.BlockSpec((tm, d), lambda i: (i, 0))
    vec_spec = pl.BlockSpec((1, d), lambda i: (0, 0))
    in_specs = [pl.BlockSpec((tm, k), lambda i: (i, 0)),
                pl.BlockSpec((k, d), lambda i: (0, 0), pipeline_mode=pl.Buffered(1)),
                row_spec, vec_spec]
    args = [x, w, res, post_w]
    out_shape = [jax.ShapeDtypeStruct((m, d), F32)]
    out_specs = [row_spec]
    if emit_next:
        in_specs.append(vec_spec)
        args.append(next_w)
        out_shape.append(jax.ShapeDtypeStruct((m, d), BF16))
        out_specs.append(row_spec)
    outs = pl.pallas_call(
        functools.partial(_rows_kernel, tn=tn, res_scale=res_scale, emit_next=emit_next),
        out_shape=out_shape,
        grid=(m // tm,),
        in_specs=in_specs,
        out_specs=out_specs,
        scratch_shapes=[pltpu.VMEM((tm, d), F32)],
        compiler_params=_cparams(1),
        name="rows_matmul_norm",
    )(*args)
    return outs if emit_next else (outs[0], None)


def _split_bf16(x):
    hi = x.astype(BF16)
    lo = (x - hi.astype(F32)).astype(BF16)
    return hi, lo


def _dot_nt(a, b):
    return lax.dot_general(a, b, (((1,), (1,)), ((), ())), preferred_element_type=F32)


def _dot_tn(a, b):
    return lax.dot_general(a, b, (((0,), (0,)), ((), ())), preferred_element_type=F32)


N_SUB = CHUNK // SUB


def _chunk_consts():
    i = lax.broadcasted_iota(jnp.int32, (CHUNK, CHUNK), 0)
    j = lax.broadcasted_iota(jnp.int32, (CHUNK, CHUNK), 1)
    lo = (i // SUB) * SUB
    hi = lo + SUB
    groups = [(j >= lo) & (j <= i), (j > i) & (j < hi), j < lo, j >= hi]
    sel = jnp.concatenate([jnp.where(g, 1.0, 0.0) for g in groups], axis=0).astype(BF16)
    return jnp.concatenate([sel, sel], axis=1), i >= j


def _decay_sums(sel2, g):
    g_hi, g_lo = _split_bf16(g)
    return jnp.dot(sel2, jnp.concatenate([g_hi, g_lo], axis=0), preferred_element_type=F32)


def _gated_chunks(heads, causal):
    prep = [_chunk_operands(*h) for h in heads]
    scores = [_dot_nt(p[0], p[1]) for p in prep]
    kv = [jnp.dot(p[4], p[3], preferred_element_type=F32) for p in prep]
    outs, states = [], []
    for (q_t, k_t, q_st, k_st, v_t, st_bf, st_decay), s, upd, h in zip(prep, scores, kv, heads):
        s = jnp.where(causal, s, 0.0).astype(BF16)
        outs.append(_dot_nt(jnp.concatenate([q_st, s], axis=1),
                            jnp.concatenate([st_bf, v_t], axis=1)))
        states.append(h[4] * st_decay + upd)
    return outs, states


def _chunk_operands(q, k, v, dec, st):
    within, rest, before, after = dec
    dk = q.shape[1]
    cum = within + before
    q_st = (q * jnp.exp2(cum)).astype(BF16)
    k_st = (k * jnp.exp2(rest + after)).astype(BF16)
    q_in = (q * jnp.exp2(within)).astype(BF16)
    k_diag = (k * jnp.exp2(jnp.minimum(-within, EXP2_CLAMP))).astype(BF16)
    k_end = k * jnp.exp2(rest)
    k_end_bf = k_end.astype(BF16)

    blk = lambda x, b: x[b * SUB:(b + 1) * SUB]
    whole = {b: jnp.exp2(within[(b + 1) * SUB - 1:(b + 1) * SUB]) for b in range(1, N_SUB - 1)}
    skip = {}
    for bq in range(N_SUB):
        for bk in range(bq - 1):
            d = whole[bk + 1]
            for mid in range(bk + 2, bq):
                d = d * whole[mid]
            skip[bq, bk] = d
    zero = jnp.zeros((SUB, dk), BF16)
    k_cols, q_cols = [], []
    for bq in range(N_SUB):
        col = []
        for bk in range(N_SUB):
            if bk > bq:
                col.append(zero)
            elif bk == bq:
                col.append(blk(k_diag, bk))
            elif bk == bq - 1:
                col.append(blk(k_end_bf, bk))
            else:
                col.append((blk(k_end, bk) * skip[bq, bk]).astype(BF16))
        k_cols.append(jnp.concatenate(col, axis=0))
        q_cols.append(jnp.concatenate([blk(q_in, b) if b == bq else zero for b in range(N_SUB)],
                                      axis=0))
    q_tilde = jnp.concatenate(q_cols, axis=1)
    k_tilde = jnp.concatenate(k_cols, axis=1)
    v_t = v.T
    return q_tilde, k_tilde, q_st, k_st, v_t, st.astype(BF16), jnp.exp2(cum[CHUNK - 1:CHUNK])


def _head_norm_gate(o, norm_w, gate):
    return (o * _rms_scale(o) * norm_w * _silu(gate.astype(F32))).astype(BF16)


def _gla_kernel(q_ref, k_ref, v_ref, go_ref, code_ref, wgk_ref, bgk_ref, gn_ref, o_ref, st_ref,
                sums_ref, *, n_chunks, dk, dv):
    @pl.when(pl.program_id(1) == 0)
    def _():
        st_ref[...] = jnp.zeros_like(st_ref)

    sel2, causal = _chunk_consts()
    norm_w = gn_ref[...]
    q_scale = dk ** -0.5

    w_hi, w_lo = _split_bf16(wgk_ref[...])
    c_hi, c_lo = _split_bf16(code_ref[...])
    z = (jnp.dot(c_hi, w_hi, preferred_element_type=F32)
         + jnp.dot(c_lo, w_hi, preferred_element_type=F32)
         + jnp.dot(c_hi, w_lo, preferred_element_type=F32)) + bgk_ref[...]
    log2_a = ((jnp.minimum(z, 0.0) - jnp.log1p(jnp.exp2(jnp.abs(z) * (-LOG2_E))))
              * (LOG2_E / GLA_GATE_NORMALIZER))
    for c in range(n_chunks):
        sums_ref[c] = _decay_sums(sel2, log2_a[c * CHUNK:(c + 1) * CHUNK])

    def chunk_body(c, carry):
        rows = pl.ds(pl.multiple_of(c * CHUNK, CHUNK), CHUNK)
        heads = []
        for h in range(GLA_HEADS):
            kc = slice(h * dk, (h + 1) * dk)
            q = q_ref[rows, kc].astype(F32) * q_scale
            k = k_ref[rows, kc].astype(F32)
            dec = [sums_ref[c, n * CHUNK:(n + 1) * CHUNK, kc] for n in range(4)]
            heads.append((q, k, v_ref[rows, h * dv:(h + 1) * dv], dec, st_ref[h]))
        outs, states = _gated_chunks(heads, causal)
        for h in range(GLA_HEADS):
            vc = slice(h * dv, (h + 1) * dv)
            st_ref[h] = states[h]
            o_ref[rows, vc] = _head_norm_gate(outs[h], norm_w, go_ref[rows, vc])
        return carry

    lax.fori_loop(0, n_chunks, chunk_body, 0, unroll=CHUNK_UNROLL)


def _gla(p_a, p_code, wgk_pad, b_gk, gla_norm, vw, batch, seq, t_blk=1024):
    kw = wgk_pad.shape[1]
    dk = kw // GLA_HEADS
    dv = vw // GLA_HEADS
    nt = seq // t_blk
    row = lambda b, t: b * nt + t
    return pl.pallas_call(
        functools.partial(_gla_kernel, n_chunks=t_blk // CHUNK, dk=dk, dv=dv),
        out_shape=jax.ShapeDtypeStruct((batch * seq, vw), BF16),
        grid=(batch, nt),
        in_specs=[pl.BlockSpec((t_blk, kw), lambda b, t: (row(b, t), 0)),
                  pl.BlockSpec((t_blk, kw), lambda b, t: (row(b, t), 1)),
                  pl.BlockSpec((t_blk, vw), lambda b, t: (row(b, t), 1)),
                  pl.BlockSpec((t_blk, vw), lambda b, t: (row(b, t), 2)),
                  pl.BlockSpec((t_blk, LANES), lambda b, t: (row(b, t), 0)),
                  pl.BlockSpec((LANES, kw), lambda b, t: (0, 0)),
                  pl.BlockSpec((1, kw), lambda b, t: (0, 0)),
                  pl.BlockSpec((1, dv), lambda b, t: (0, 0))],
        out_specs=pl.BlockSpec((t_blk, vw), lambda b, t: (row(b, t), 0)),
        scratch_shapes=[pltpu.VMEM((GLA_HEADS, dv, dk), F32),
                        pltpu.VMEM((t_blk // CHUNK, 4 * CHUNK, kw), F32)],
        compiler_params=_cparams(2),
        name="gla_mixer",
    )(p_a, p_a, p_a, p_a, p_code, wgk_pad, b_gk, gla_norm)


def _hgrn_kernel(hq_ref, hi_ref, ho_ref, hf_ref, lbl_ref, hn_ref, o_ref, st_ref,
                 *, n_chunks, n_heads, dk, layer):
    @pl.when(pl.program_id(1) == 0)
    def _():
        st_ref[...] = jnp.zeros_like(st_ref)

    sel2, causal = _chunk_consts()
    logits = lbl_ref[...]
    p = jnp.exp(logits - jnp.max(logits, axis=0, keepdims=True))
    p = p / jnp.sum(p, axis=0, keepdims=True)
    lb = jnp.sum(p[:layer + 1], axis=0, keepdims=True)
    one_m_lb = jnp.sum(p[layer + 1:], axis=0, keepdims=True)
    norm_w = hn_ref[...]

    def chunk_body(c, carry):
        rows = pl.ds(pl.multiple_of(c * CHUNK, CHUNK), CHUNK)
        hf = hf_ref[rows, :]
        e = jnp.exp2(jnp.abs(hf) * (-LOG2_E))
        inv = 1.0 / (1.0 + e)
        pos = hf >= 0.0
        sig = jnp.where(pos, inv, e * inv)
        sig_neg = jnp.where(pos, e * inv, inv)
        log2_f = jnp.log2(lb + one_m_lb * sig)
        k_all = one_m_lb * sig_neg
        sums = _decay_sums(sel2, log2_f)
        heads = []
        for h in range(n_heads):
            hc = slice(h * dk, (h + 1) * dk)
            q = _silu(hq_ref[rows, hc].astype(F32))
            dec = [sums[n * CHUNK:(n + 1) * CHUNK, hc] for n in range(4)]
            heads.append((q, k_all[:, hc], hi_ref[rows, hc], dec, st_ref[h]))
        outs, states = _gated_chunks(heads, causal)
        for h in range(n_heads):
            hc = slice(h * dk, (h + 1) * dk)
            st_ref[h] = states[h]
            o_ref[rows, hc] = _head_norm_gate(outs[h], norm_w, ho_ref[rows, hc])
        return carry

    lax.fori_loop(0, n_chunks, chunk_body, 0, unroll=CHUNK_UNROLL)


def _hgrn(p_r, q_col0, p_f, lb_logits, hgrn_norm, layer, batch, seq, t_blk=1024):
    w = p_f.shape[1]
    dk = HGRN_EXPAND
    n_heads = w // dk
    nt = seq // t_blk
    n_lb = lb_logits.shape[0]
    row = lambda b, t: b * nt + t
    c0 = q_col0 // w
    assert c0 * w == q_col0
    return pl.pallas_call(
        functools.partial(_hgrn_kernel, n_chunks=t_blk // CHUNK, n_heads=n_heads, dk=dk,
                          layer=layer),
        out_shape=jax.ShapeDtypeStruct((batch * seq, w), BF16),
        grid=(batch, nt),
        in_specs=[pl.BlockSpec((t_blk, w), lambda b, t: (row(b, t), c0)),
                  pl.BlockSpec((t_blk, w), lambda b, t: (row(b, t), c0 + 1)),
                  pl.BlockSpec((t_blk, w), lambda b, t: (row(b, t), c0 + 2)),
                  pl.BlockSpec((t_blk, w), lambda b, t: (row(b, t), 0)),
                  pl.BlockSpec((n_lb, w), lambda b, t: (0, 0)),
                  pl.BlockSpec((1, dk), lambda b, t: (0, 0))],
        out_specs=pl.BlockSpec((t_blk, w), lambda b, t: (row(b, t), 0)),
        scratch_shapes=[pltpu.VMEM((n_heads, dk, dk), F32)],
        compiler_params=_cparams(2),
        name="hgrn_mixer",
    )(p_r, p_r, p_r, p_f, lb_logits, hgrn_norm)


def _merge_kernel(og_ref, oh_ref, zg_ref, zh_ref, wg_ref, wh_ref, bg_ref, bh_ref, wo_ref,
                  o_ref, wo_bf_ref, wg_bf, wh_bf):
    @pl.when(pl.program_id(1) == 0)
    def _():
        wg_bf[...] = wg_ref[...].astype(BF16)
        wh_bf[...] = wh_ref[...].astype(BF16)

    wo_bf_ref[...] = wo_ref[...].astype(BF16)
    a = jnp.dot(og_ref[...], wg_bf[...], preferred_element_type=F32)
    b = jnp.dot(oh_ref[...], wh_bf[...], preferred_element_type=F32)
    o_ref[...] = (_sigmoid(zg_ref[...].astype(F32) + bg_ref[...]) * a
                  + _sigmoid(zh_ref[...].astype(F32) + bh_ref[...]) * b).astype(o_ref.dtype)


def _merge(o_gla, o_hgrn, p_b, zg_col0, zh_col0, w_bg, w_bh, b_gates, w_out, tm=1024, tn=1024):
    m, kdim = o_gla.shape
    d = w_bg.shape[-1]
    ni = m // tm
    k_out, d_out = w_out.shape[1:]
    rb = k_out // ((d // tn) * ni)
    assert rb * (d // tn) * ni == k_out and rb % (2 * SUBLANES) == 0
    x_spec = pl.BlockSpec((tm, kdim), lambda j, i: (i, 0))
    w_spec = pl.BlockSpec((None, kdim, tn), lambda j, i: (0, 0, j))
    return pl.pallas_call(
        _merge_kernel,
        out_shape=[jax.ShapeDtypeStruct((m, d), BF16), jax.ShapeDtypeStruct((k_out, d_out), BF16)],
        grid=(d // tn, ni),
        in_specs=[x_spec, x_spec,
                  pl.BlockSpec((tm, tn), lambda j, i: (i, zg_col0 // tn + j)),
                  pl.BlockSpec((tm, tn), lambda j, i: (i, zh_col0 // tn + j)),
                  w_spec, w_spec,
                  pl.BlockSpec((None, 1, tn), lambda j, i: (0, 0, j)),
                  pl.BlockSpec((None, 1, tn), lambda j, i: (1, 0, j)),
                  pl.BlockSpec((None, rb, d_out), lambda j, i: (0, j * ni + i, 0))],
        out_specs=[pl.BlockSpec((tm, tn), lambda j, i: (i, j)),
                   pl.BlockSpec((rb, d_out), lambda j, i: (j * ni + i, 0))],
        scratch_shapes=[pltpu.VMEM((kdim, tn), BF16), pltpu.VMEM((kdim, tn), BF16)],
        compiler_params=_cparams(2),
        name="branch_merge",
    )(o_gla, o_hgrn, p_b, p_b, w_bg, w_bh, b_gates, b_gates, w_out)


def kernel(x, ffn1_pre_norm, ffn1_w_gate, ffn1_w_up, ffn1_w_down, ffn1_post_norm, mix_pre_norm, w_in, gla_w_gk_up, gla_b_gk, gla_norm, hgrn_lb_logits, hgrn_norm, w_branch_gla, w_branch_hgrn, b_branch_gates, w_out, mix_post_norm, ffn2_pre_norm, ffn2_w_gate, ffn2_w_up, ffn2_w_down, ffn2_post_norm):
    batch, seq, d_model = x.shape
    depth = ffn1_w_gate.shape[0]
    m = batch * seq
    kw = gla_w_gk_up.shape[-1]
    vw = d_model // 2
    a_cols = 2 * kw + 2 * vw
    code0 = a_cols
    hq0 = code0 + GLA_GATE_RANK
    hf0, hi0 = hq0 + vw, hq0 + 2 * vw

    h = x.reshape(m, d_model)
    u = _rmsnorm(h, ffn1_pre_norm[0:1])
    for l in range(depth):
        mid, w_down = _gateup(u, ffn1_w_gate[l:l + 1], ffn1_w_up[l:l + 1], ffn1_w_down[l:l + 1])
        h, u = _rows(mid, w_down, h, ffn1_post_norm[l:l + 1],
                     mix_pre_norm[l:l + 1], 0.5)

        w_l = jnp.swapaxes(w_in[l:l + 1], 1, 2)
        p_f, p_code = _proj(u, w_l, [(hf0, vw)], F32, side=(code0, LANES))
        p = _proj(u, w_l, [(0, a_cols), (hq0, vw), (hi0, 2 * vw + 2 * d_model)], BF16)
        wgk_pad = jnp.pad(gla_w_gk_up[l], ((0, LANES - GLA_GATE_RANK), (0, 0)))
        o_gla = _gla(p, p_code, wgk_pad, gla_b_gk[l:l + 1], gla_norm[l:l + 1], vw, batch, seq)
        o_hgrn = _hgrn(p, a_cols, p_f, hgrn_lb_logits, hgrn_norm[l:l + 1], l, batch, seq)
        merged, w_out_bf = _merge(o_gla, o_hgrn, p, a_cols + 3 * vw, a_cols + 3 * vw + d_model,
                                  w_branch_gla[l:l + 1], w_branch_hgrn[l:l + 1],
                                  b_branch_gates[l].reshape(2, 1, d_model), w_out[l:l + 1])
        h, u = _rows(merged, w_out_bf, h, mix_post_norm[l:l + 1],
                     ffn2_pre_norm[l:l + 1], 1.0, tm=512)

        mid, w_down = _gateup(u, ffn2_w_gate[l:l + 1], ffn2_w_up[l:l + 1], ffn2_w_down[l:l + 1])
        next_norm = ffn1_pre_norm[l + 1:l + 2] if l + 1 < depth else None
        h, u = _rows(mid, w_down, h, ffn2_post_norm[l:l + 1], next_norm, 0.5)
    return h.reshape(batch, seq, d_model)
```

```python
import functools

import jax
import jax.numpy as jnp
from jax import lax
from jax.experimental import pallas as pl
from jax.experimental.pallas import tpu as pltpu

F32 = jnp.float32
BF16 = jnp.bfloat16

EPS = 1e-6
CHUNK = 64
SUB = 16
GLA_HEADS = 4
GLA_GATE_RANK = 16
GLA_GATE_NORMALIZER = 16.0
HGRN_EXPAND = 128
CHUNK_UNROLL = 4
LOG2_E = 1.4426950408889634
EXP2_CLAMP = 115.0

LANES = 128
SUBLANES = 8
VMEM_LIMIT = 62 * 1024 * 1024
MATMUL_SUB_ROWS = 1024


def _cparams(n_axes):
    return pltpu.CompilerParams(
        dimension_semantics=("arbitrary",) * n_axes, vmem_limit_bytes=VMEM_LIMIT)


def _sigmoid(x):
    return 1.0 / (1.0 + jnp.exp2(x * (-LOG2_E)))


def _silu(x):
    return x * _sigmoid(x)


def _fold_lanes(x):
    acc = x[:, :LANES]
    for c in range(LANES, x.shape[1], LANES):
        acc = acc + x[:, c:c + LANES]
    return acc


def _rms_scale(x):
    ssq = jnp.sum(_fold_lanes(x * x), axis=-1, keepdims=True)
    return lax.rsqrt(ssq / x.shape[-1] + EPS)


def _rmsnorm_kernel(x_ref, w_ref, o_ref):
    x = x_ref[...]
    o_ref[...] = (x * _rms_scale(x) * w_ref[...]).astype(o_ref.dtype)


def _rmsnorm(x, w, tm=1024):
    m, d = x.shape
    return pl.pallas_call(
        _rmsnorm_kernel,
        out_shape=jax.ShapeDtypeStruct((m, d), BF16),
        grid=(m // tm,),
        in_specs=[pl.BlockSpec((tm, d), lambda i: (i, 0)),
                  pl.BlockSpec((1, d), lambda i: (0, 0))],
        out_specs=pl.BlockSpec((tm, d), lambda i: (i, 0)),
        compiler_params=_cparams(1),
        name="rmsnorm",
    )(x, w)


def _gateup_kernel(u_ref, wg_ref, wu_ref, wd_ref, o_ref, wd_bf_ref, wg_bf, wu_bf):
    @pl.when(pl.program_id(1) == 0)
    def _():
        wg_bf[...] = wg_ref[...].astype(BF16)
        wu_bf[...] = wu_ref[...].astype(BF16)
        wd_bf_ref[...] = wd_ref[...].astype(BF16)

    for r in range(0, u_ref.shape[0], MATMUL_SUB_ROWS):
        rows = slice(r, r + MATMUL_SUB_ROWS)
        u = u_ref[rows, :]
        g = jnp.dot(u, wg_bf[...], preferred_element_type=F32)
        up = jnp.dot(u, wu_bf[...], preferred_element_type=F32)
        o_ref[rows, :] = (_silu(g) * up).astype(o_ref.dtype)


def _gateup(u, w_gate, w_up, w_down, tm=2048, tn=512):
    m, d = u.shape
    f = w_gate.shape[-1]
    w_spec = pl.BlockSpec((None, d, tn), lambda j, i: (0, 0, j))
    return pl.pallas_call(
        _gateup_kernel,
        out_shape=[jax.ShapeDtypeStruct((m, f), BF16), jax.ShapeDtypeStruct((f, d), BF16)],
        grid=(pl.cdiv(f, tn), m // tm),
        in_specs=[pl.BlockSpec((tm, d), lambda j, i: (i, 0)), w_spec, w_spec,
                  pl.BlockSpec((None, tn, d), lambda j, i: (0, j, 0))],
        out_specs=[pl.BlockSpec((tm, tn), lambda j, i: (i, j)),
                   pl.BlockSpec((tn, d), lambda j, i: (j, 0))],
        scratch_shapes=[pltpu.VMEM((d, tn), BF16), pltpu.VMEM((d, tn), BF16)],
        compiler_params=_cparams(2),
        name="ffn_gateup",
    )(u, w_gate, w_up, w_down)


def _proj_kernel(u_ref, w_ref, *rest, shift, n_plain, tn, side):
    rest = list(rest)
    wn_ref = rest.pop(0) if shift else None
    ws_ref = rest.pop(0) if side else None
    o_ref = rest.pop(0)
    os_ref = rest.pop(0) if side else None
    (w_bf,) = rest
    j, i = pl.program_id(0), pl.program_id(1)

    @pl.when((i == 0) & (j < n_plain))
    def _():
        w_bf[...] = w_ref[...].astype(BF16)

    if shift:
        @pl.when((i == 0) & (j >= n_plain))
        def _():
            w = jnp.concatenate([w_ref[...], wn_ref[...]], axis=0)[shift:shift + tn]
            w_bf[...] = w.astype(BF16)

    sub = min(u_ref.shape[0], MATMUL_SUB_ROWS)
    for r in range(0, u_ref.shape[0], sub):
        o_ref[r:r + sub, :] = _dot_nt(u_ref[r:r + sub, :], w_bf[...]).astype(o_ref.dtype)
    if side:
        os_ref[...] = _dot_nt(u_ref[...], ws_ref[...].astype(BF16))


def _proj(u, wt, windows, out_dtype, tn=1024, side=None):
    m, d = u.shape
    shift = windows[-1][0] % tn
    starts = []
    n_plain = 0
    for c0, n in windows:
        assert c0 % tn in (0, shift) and n % tn == 0 and shift % SUBLANES == 0
        if c0 % tn == 0 and n_plain == len(starts):
            n_plain += n // tn
        else:
            assert c0 % tn == shift
        starts += [c0 // tn + t for t in range(n // tn)]
    n_tiles = len(starts)
    tm = 2048 if n_tiles >= 3 else 1024

    def blk(j):
        b = jnp.int32(starts[0])
        for t in range(1, n_tiles):
            b = jnp.where(j >= t, starts[t], b)
        return b

    in_specs = [pl.BlockSpec((tm, d), lambda j, i: (i, 0)),
                pl.BlockSpec((None, tn, d), lambda j, i: (0, blk(j), 0))]
    args = [u, wt]
    if shift:
        assert tn % shift == 0
        per_tile = tn // shift
        in_specs.append(pl.BlockSpec((None, shift, d), lambda j, i: (0, (blk(j) + 1) * per_tile, 0)))
        args.append(wt)
    out_shape = [jax.ShapeDtypeStruct((m, n_tiles * tn), out_dtype)]
    out_specs = [pl.BlockSpec((tm, tn), lambda j, i: (i, j))]
    if side:
        s0, sn = side
        assert n_tiles == 1 and s0 % sn == 0
        in_specs.append(pl.BlockSpec((None, sn, d), lambda j, i: (0, s0 // sn, 0)))
        args.append(wt)
        out_shape.append(jax.ShapeDtypeStruct((m, sn), F32))
        out_specs.append(pl.BlockSpec((tm, sn), lambda j, i: (i, 0)))
    outs = pl.pallas_call(
        functools.partial(_proj_kernel, shift=shift, n_plain=n_plain, tn=tn, side=bool(side)),
        out_shape=out_shape,
        grid=(n_tiles, m // tm),
        in_specs=in_specs,
        out_specs=out_specs,
        scratch_shapes=[pltpu.VMEM((tn, d), BF16)],
        compiler_params=_cparams(2),
        name="in_proj",
    )(*args)
    return outs if side else outs[0]


def _rows_kernel(x_ref, w_ref, res_ref, post_ref, *rest, tn, res_scale, emit_next):
    if emit_next:
        next_ref, h_ref, u_ref, acc_ref = rest
    else:
        h_ref, acc_ref = rest
    tm, d = h_ref.shape
    col_tiles = [slice(c, c + tn) for c in range(0, d, tn)]

    for r in range(0, tm, ROWS_SUB):
        rows = slice(r, r + ROWS_SUB)
        x = x_ref[rows, :]
        for cols in col_tiles:
            acc_ref[rows, cols] = jnp.dot(x, w_ref[:, cols], preferred_element_type=F32)

        ssq = None
        for cols in col_tiles:
            a = acc_ref[rows, cols]
            s = _fold_lanes(a * a)
            ssq = s if ssq is None else ssq + s
        ssq = jnp.sum(ssq, axis=-1, keepdims=True)
        scale = lax.rsqrt(ssq / d + EPS) * res_scale
        hsq = None
        for cols in col_tiles:
            h = res_ref[rows, cols] + acc_ref[rows, cols] * scale * post_ref[:, cols]
            h_ref[rows, cols] = h
            if emit_next:
                s = _fold_lanes(h * h)
                hsq = s if hsq is None else hsq + s
        if emit_next:
            hsq = jnp.sum(hsq, axis=-1, keepdims=True)
            nscale = lax.rsqrt(hsq / d + EPS)
            for cols in col_tiles:
                u_ref[rows, cols] = (h_ref[rows, cols] * nscale
                                     * next_ref[:, cols]).astype(u_ref.dtype)


ROWS_SUB = 256


def _rows(x, w, res, post_w, next_w, res_scale, tm=512, tn=512):
    m, k = x.shape
    d = w.shape[-1]
    emit_next = next_w is not None
    row_spec = pl.BlockSpec((tm, d), lambda i: (i, 0))
    vec_spec = pl.BlockSpec((1, d), lambda i: (0, 0))
    in_specs = [pl.BlockSpec((tm, k), lambda i: (i, 0)),
                pl.BlockSpec((k, d), lambda i: (0, 0), pipeline_mode=pl.Buffered(1)),
                row_spec, vec_spec]
    args = [x, w, res, post_w]
    out_shape = [jax.ShapeDtypeStruct((m, d), F32)]
    out_specs = [row_spec]
    if emit_next:
        in_specs.append(vec_spec)
        args.append(next_w)
        out_shape.append(jax.ShapeDtypeStruct((m, d), BF16))
        out_specs.append(row_spec)
    outs = pl.pallas_call(
        functools.partial(_rows_kernel, tn=tn, res_scale=res_scale, emit_next=emit_next),
        out_shape=out_shape,
        grid=(m // tm,),
        in_specs=in_specs,
        out_specs=out_specs,
        scratch_shapes=[pltpu.VMEM((tm, d), F32)],
        compiler_params=_cparams(1),
        name="rows_matmul_norm",
    )(*args)
    return outs if emit_next else (outs[0], None)


def _split_bf16(x):
    hi = x.astype(BF16)
    lo = (x - hi.astype(F32)).astype(BF16)
    return hi, lo


def _dot_nt(a, b):
    return lax.dot_general(a, b, (((1,), (1,)), ((), ())), preferred_element_type=F32)


def _dot_tn(a, b):
    return lax.dot_general(a, b, (((0,), (0,)), ((), ())), preferred_element_type=F32)


N_SUB = CHUNK // SUB


def _chunk_consts():
    i = lax.broadcasted_iota(jnp.int32, (CHUNK, CHUNK), 0)
    j = lax.broadcasted_iota(jnp.int32, (CHUNK, CHUNK), 1)
    lo = (i // SUB) * SUB
    hi = lo + SUB
    groups = [(j >= lo) & (j <= i), (j > i) & (j < hi), j < lo, j >= hi]
    sel = jnp.concatenate([jnp.where(g, 1.0, 0.0) for g in groups], axis=0).astype(BF16)
    return jnp.concatenate([sel, sel], axis=1), i >= j


def _decay_sums(sel2, g):
    g_hi, g_lo = _split_bf16(g)
    return jnp.dot(sel2, jnp.concatenate([g_hi, g_lo], axis=0), preferred_element_type=F32)


def _gated_chunks(heads, causal):
    prep = [_chunk_operands(*h) for h in heads]
    scores = [_dot_nt(p[0], p[1]) for p in prep]
    kv = [jnp.dot(p[4], p[3], preferred_element_type=F32) for p in prep]
    outs, states = [], []
    for (q_t, k_t, q_st, k_st, v_t, st_bf, st_decay), s, upd, h in zip(prep, scores, kv, heads):
        s = jnp.where(causal, s, 0.0).astype(BF16)
        outs.append(_dot_nt(jnp.concatenate([q_st, s], axis=1),
                            jnp.concatenate([st_bf, v_t], axis=1)))
        states.append(h[4] * st_decay + upd)
    return outs, states


def _chunk_operands(q, k, v, dec, st):
    within, rest, before, after = dec
    dk = q.shape[1]
    cum = within + before
    q_st = (q * jnp.exp2(cum)).astype(BF16)
    k_st = (k * jnp.exp2(rest + after)).astype(BF16)
    q_in = (q * jnp.exp2(within)).astype(BF16)
    k_diag = (k * jnp.exp2(jnp.minimum(-within, EXP2_CLAMP))).astype(BF16)
    k_end = k * jnp.exp2(rest)
    k_end_bf = k_end.astype(BF16)

    blk = lambda x, b: x[b * SUB:(b + 1) * SUB]
    whole = {b: jnp.exp2(within[(b + 1) * SUB - 1:(b + 1) * SUB]) for b in range(1, N_SUB - 1)}
    skip = {}
    for bq in range(N_SUB):
        for bk in range(bq - 1):
            d = whole[bk + 1]
            for mid in range(bk + 2, bq):
                d = d * whole[mid]
            skip[bq, bk] = d
    zero = jnp.zeros((SUB, dk), BF16)
    k_cols, q_cols = [], []
    for bq in range(N_SUB):
        col = []
        for bk in range(N_SUB):
            if bk > bq:
                col.append(zero)
            elif bk == bq:
                col.append(blk(k_diag, bk))
            elif bk == bq - 1:
                col.append(blk(k_end_bf, bk))
            else:
                col.append((blk(k_end, bk) * skip[bq, bk]).astype(BF16))
        k_cols.append(jnp.concatenate(col, axis=0))
        q_cols.append(jnp.concatenate([blk(q_in, b) if b == bq else zero for b in range(N_SUB)],
                                      axis=0))
    q_tilde = jnp.concatenate(q_cols, axis=1)
    k_tilde = jnp.concatenate(k_cols, axis=1)
    v_t = v.T
    return q_tilde, k_tilde, q_st, k_st, v_t, st.astype(BF16), jnp.exp2(cum[CHUNK - 1:CHUNK])


def _head_norm_gate(o, norm_w, gate):
    return (o * _rms_scale(o) * norm_w * _silu(gate.astype(F32))).astype(BF16)


def _gla_kernel(q_ref, k_ref, v_ref, go_ref, code_ref, wgk_ref, bgk_ref, gn_ref, o_ref, st_ref,
                sums_ref, *, n_chunks, dk, dv):
    @pl.when(pl.program_id(1) == 0)
    def _():
        st_ref[...] = jnp.zeros_like(st_ref)

    sel2, causal = _chunk_consts()
    norm_w = gn_ref[...]
    q_scale = dk ** -0.5

    w = wgk_ref[...]
    w = jnp.concatenate([w, jnp.zeros((code_ref.shape[1] - w.shape[0], w.shape[1]), F32)], axis=0)
    w_hi, w_lo = _split_bf16(w)
    c_hi, c_lo = _split_bf16(code_ref[...])
    z = (jnp.dot(c_hi, w_hi, preferred_element_type=F32)
         + jnp.dot(c_lo, w_hi, preferred_element_type=F32)
         + jnp.dot(c_hi, w_lo, preferred_element_type=F32)) + bgk_ref[...]
    log2_a = ((jnp.minimum(z, 0.0) - jnp.log1p(jnp.exp2(jnp.abs(z) * (-LOG2_E))))
              * (LOG2_E / GLA_GATE_NORMALIZER))
    for c in range(n_chunks):
        sums_ref[c] = _decay_sums(sel2, log2_a[c * CHUNK:(c + 1) * CHUNK])

    def chunk_body(c, carry):
        rows = pl.ds(pl.multiple_of(c * CHUNK, CHUNK), CHUNK)
        heads = []
        for h in range(GLA_HEADS):
            kc = slice(h * dk, (h + 1) * dk)
            q = q_ref[rows, kc].astype(F32) * q_scale
            k = k_ref[rows, kc].astype(F32)
            dec = [sums_ref[c, n * CHUNK:(n + 1) * CHUNK, kc] for n in range(4)]
            heads.append((q, k, v_ref[rows, h * dv:(h + 1) * dv], dec, st_ref[h]))
        outs, states = _gated_chunks(heads, causal)
        for h in range(GLA_HEADS):
            vc = slice(h * dv, (h + 1) * dv)
            st_ref[h] = states[h]
            o_ref[rows, vc] = _head_norm_gate(outs[h], norm_w, go_ref[rows, vc])
        return carry

    lax.fori_loop(0, n_chunks, chunk_body, 0, unroll=CHUNK_UNROLL)


def _gla(p_a, p_code, w_gk, b_gk, gla_norm, vw, batch, seq, t_blk=1024):
    rank, kw = w_gk.shape
    dk = kw // GLA_HEADS
    dv = vw // GLA_HEADS
    nt = seq // t_blk
    row = lambda b, t: b * nt + t
    return pl.pallas_call(
        functools.partial(_gla_kernel, n_chunks=t_blk // CHUNK, dk=dk, dv=dv),
        out_shape=jax.ShapeDtypeStruct((batch * seq, vw), BF16),
        grid=(batch, nt),
        in_specs=[pl.BlockSpec((t_blk, kw), lambda b, t: (row(b, t), 0)),
                  pl.BlockSpec((t_blk, kw), lambda b, t: (row(b, t), 1)),
                  pl.BlockSpec((t_blk, vw), lambda b, t: (row(b, t), 1)),
                  pl.BlockSpec((t_blk, vw), lambda b, t: (row(b, t), 2)),
                  pl.BlockSpec((t_blk, LANES), lambda b, t: (row(b, t), 0)),
                  pl.BlockSpec((rank, kw), lambda b, t: (0, 0)),
                  pl.BlockSpec((1, kw), lambda b, t: (0, 0)),
                  pl.BlockSpec((1, dv), lambda b, t: (0, 0))],
        out_specs=pl.BlockSpec((t_blk, vw), lambda b, t: (row(b, t), 0)),
        scratch_shapes=[pltpu.VMEM((GLA_HEADS, dv, dk), F32),
                        pltpu.VMEM((t_blk // CHUNK, 4 * CHUNK, kw), F32)],
        compiler_params=_cparams(2),
        name="gla_mixer",
    )(p_a, p_a, p_a, p_a, p_code, w_gk, b_gk, gla_norm)


def _hgrn_kernel(hq_ref, hi_ref, ho_ref, hf_ref, lbl_ref, hn_ref, o_ref, st_ref,
                 *, n_chunks, n_heads, dk, layer):
    @pl.when(pl.program_id(1) == 0)
    def _():
        st_ref[...] = jnp.zeros_like(st_ref)

    sel2, causal = _chunk_consts()
    logits = lbl_ref[...]
    p = jnp.exp(logits - jnp.max(logits, axis=0, keepdims=True))
    p = p / jnp.sum(p, axis=0, keepdims=True)
    lb = jnp.sum(p[:layer + 1], axis=0, keepdims=True)
    one_m_lb = jnp.sum(p[layer + 1:], axis=0, keepdims=True)
    norm_w = hn_ref[...]

    def chunk_body(c, carry):
        rows = pl.ds(pl.multiple_of(c * CHUNK, CHUNK), CHUNK)
        hf = hf_ref[rows, :]
        e = jnp.exp2(jnp.abs(hf) * (-LOG2_E))
        inv = 1.0 / (1.0 + e)
        pos = hf >= 0.0
        sig = jnp.where(pos, inv, e * inv)
        sig_neg = jnp.where(pos, e * inv, inv)
        log2_f = jnp.log2(lb + one_m_lb * sig)
        k_all = one_m_lb * sig_neg
        sums = _decay_sums(sel2, log2_f)
        heads = []
        for h in range(n_heads):
            hc = slice(h * dk, (h + 1) * dk)
            q = _silu(hq_ref[rows, hc].astype(F32))
            dec = [sums[n * CHUNK:(n + 1) * CHUNK, hc] for n in range(4)]
            heads.append((q, k_all[:, hc], hi_ref[rows, hc], dec, st_ref[h]))
        outs, states = _gated_chunks(heads, causal)
        for h in range(n_heads):
            hc = slice(h * dk, (h + 1) * dk)
            st_ref[h] = states[h]
            o_ref[rows, hc] = _head_norm_gate(outs[h], norm_w, ho_ref[rows, hc])
        return carry

    lax.fori_loop(0, n_chunks, chunk_body, 0, unroll=CHUNK_UNROLL)


def _hgrn(p_r, q_col0, p_f, lb_logits, hgrn_norm, layer, batch, seq, t_blk=1024):
    w = p_f.shape[1]
    dk = HGRN_EXPAND
    n_heads = w // dk
    nt = seq // t_blk
    n_lb = lb_logits.shape[0]
    row = lambda b, t: b * nt + t
    c0 = q_col0 // w
    assert c0 * w == q_col0
    return pl.pallas_call(
        functools.partial(_hgrn_kernel, n_chunks=t_blk // CHUNK, n_heads=n_heads, dk=dk,
                          layer=layer),
        out_shape=jax.ShapeDtypeStruct((batch * seq, w), BF16),
        grid=(batch, nt),
        in_specs=[pl.BlockSpec((t_blk, w), lambda b, t: (row(b, t), c0)),
                  pl.BlockSpec((t_blk, w), lambda b, t: (row(b, t), c0 + 1)),
                  pl.BlockSpec((t_blk, w), lambda b, t: (row(b, t), c0 + 2)),
                  pl.BlockSpec((t_blk, w), lambda b, t: (row(b, t), 0)),
                  pl.BlockSpec((n_lb, w), lambda b, t: (0, 0)),
                  pl.BlockSpec((1, dk), lambda b, t: (0, 0))],
        out_specs=pl.BlockSpec((t_blk, w), lambda b, t: (row(b, t), 0)),
        scratch_shapes=[pltpu.VMEM((n_heads, dk, dk), F32)],
        compiler_params=_cparams(2),
        name="hgrn_mixer",
    )(p_r, p_r, p_r, p_f, lb_logits, hgrn_norm)


def _merge_kernel(og_ref, oh_ref, zg_ref, zh_ref, wg_ref, wh_ref, bg_ref, bh_ref, wo_ref,
                  o_ref, wo_bf_ref, wg_bf, wh_bf):
    @pl.when(pl.program_id(1) == 0)
    def _():
        wg_bf[...] = wg_ref[...].astype(BF16)
        wh_bf[...] = wh_ref[...].astype(BF16)

    wo_bf_ref[...] = wo_ref[...].astype(BF16)
    a = jnp.dot(og_ref[...], wg_bf[...], preferred_element_type=F32)
    b = jnp.dot(oh_ref[...], wh_bf[...], preferred_element_type=F32)
    o_ref[...] = (_sigmoid(zg_ref[...].astype(F32) + bg_ref[...]) * a
                  + _sigmoid(zh_ref[...].astype(F32) + bh_ref[...]) * b).astype(o_ref.dtype)


def _merge(o_gla, o_hgrn, p_b, zg_col0, zh_col0, w_bg, w_bh, b_gates, w_out, tm=1024, tn=1024):
    m, kdim = o_gla.shape
    d = w_bg.shape[-1]
    ni = m // tm
    k_out, d_out = w_out.shape[1:]
    rb = k_out // ((d // tn) * ni)
    assert rb * (d // tn) * ni == k_out and rb % (2 * SUBLANES) == 0
    x_spec = pl.BlockSpec((tm, kdim), lambda j, i: (i, 0))
    w_spec = pl.BlockSpec((None, kdim, tn), lambda j, i: (0, 0, j))
    return pl.pallas_call(
        _merge_kernel,
        out_shape=[jax.ShapeDtypeStruct((m, d), BF16), jax.ShapeDtypeStruct((k_out, d_out), BF16)],
        grid=(d // tn, ni),
        in_specs=[x_spec, x_spec,
                  pl.BlockSpec((tm, tn), lambda j, i: (i, zg_col0 // tn + j)),
                  pl.BlockSpec((tm, tn), lambda j, i: (i, zh_col0 // tn + j)),
                  w_spec, w_spec,
                  pl.BlockSpec((None, 1, tn), lambda j, i: (0, 0, j)),
                  pl.BlockSpec((None, 1, tn), lambda j, i: (1, 0, j)),
                  pl.BlockSpec((None, rb, d_out), lambda j, i: (0, j * ni + i, 0))],
        out_specs=[pl.BlockSpec((tm, tn), lambda j, i: (i, j)),
                   pl.BlockSpec((rb, d_out), lambda j, i: (j * ni + i, 0))],
        scratch_shapes=[pltpu.VMEM((kdim, tn), BF16), pltpu.VMEM((kdim, tn), BF16)],
        compiler_params=_cparams(2),
        name="branch_merge",
    )(o_gla, o_hgrn, p_b, p_b, w_bg, w_bh, b_gates, b_gates, w_out)


def kernel(x, ffn1_pre_norm, ffn1_w_gate, ffn1_w_up, ffn1_w_down, ffn1_post_norm, mix_pre_norm, w_in, gla_w_gk_up, gla_b_gk, gla_norm, hgrn_lb_logits, hgrn_norm, w_branch_gla, w_branch_hgrn, b_branch_gates, w_out, mix_post_norm, ffn2_pre_norm, ffn2_w_gate, ffn2_w_up, ffn2_w_down, ffn2_post_norm):
    batch, seq, d_model = x.shape
    depth = ffn1_w_gate.shape[0]
    m = batch * seq
    kw = gla_w_gk_up.shape[-1]
    vw = d_model // 2
    a_cols = 2 * kw + 2 * vw
    code0 = a_cols
    hq0 = code0 + GLA_GATE_RANK
    hf0, hi0 = hq0 + vw, hq0 + 2 * vw

    h = x.reshape(m, d_model)
    u = _rmsnorm(h, ffn1_pre_norm[0:1])
    for l in range(depth):
        mid, w_down = _gateup(u, ffn1_w_gate[l:l + 1], ffn1_w_up[l:l + 1], ffn1_w_down[l:l + 1])
        h, u = _rows(mid, w_down, h, ffn1_post_norm[l:l + 1],
                     mix_pre_norm[l:l + 1], 0.5)

        w_l = jnp.swapaxes(w_in[l:l + 1], 1, 2)
        p_f, p_code = _proj(u, w_l, [(hf0, vw)], F32, side=(code0, LANES))
        p = _proj(u, w_l, [(0, a_cols), (hq0, vw), (hi0, 2 * vw + 2 * d_model)], BF16)
        o_gla = _gla(p, p_code, gla_w_gk_up[l], gla_b_gk[l:l + 1], gla_norm[l:l + 1], vw, batch,
                     seq)
        o_hgrn = _hgrn(p, a_cols, p_f, hgrn_lb_logits, hgrn_norm[l:l + 1], l, batch, seq)
        merged, w_out_bf = _merge(o_gla, o_hgrn, p, a_cols + 3 * vw, a_cols + 3 * vw + d_model,
                                  w_branch_gla[l:l + 1], w_branch_hgrn[l:l + 1],
                                  b_branch_gates[l].reshape(2, 1, d_model), w_out[l:l + 1])
        h, u = _rows(merged, w_out_bf, h, mix_post_norm[l:l + 1],
                     ffn2_pre_norm[l:l + 1], 1.0, tm=512)

        mid, w_down = _gateup(u, ffn2_w_gate[l:l + 1], ffn2_w_up[l:l + 1], ffn2_w_down[l:l + 1])
        next_norm = ffn1_pre_norm[l + 1:l + 2] if l + 1 < depth else None
        h, u = _rows(mid, w_down, h, ffn2_post_norm[l:l + 1], next_norm, 0.5)
    return h.reshape(batch, seq, d_model)
```

```python
import functools

import jax
import jax.numpy as jnp
from jax import lax
from jax.experimental import pallas as pl
from jax.experimental.pallas import tpu as pltpu

F32 = jnp.float32
BF16 = jnp.bfloat16

EPS = 1e-6
CHUNK = 64
SUB = 16
GLA_HEADS = 4
GLA_GATE_RANK = 16
GLA_GATE_NORMALIZER = 16.0
HGRN_EXPAND = 128
CHUNK_UNROLL = 4
LOG2_E = 1.4426950408889634
EXP2_CLAMP = 115.0

LANES = 128
SUBLANES = 8
VMEM_LIMIT = 62 * 1024 * 1024
MATMUL_SUB_ROWS = 1024


def _cparams(n_axes):
    return pltpu.CompilerParams(
        dimension_semantics=("arbitrary",) * n_axes, vmem_limit_bytes=VMEM_LIMIT)


def _sigmoid(x):
    return 1.0 / (1.0 + jnp.exp2(x * (-LOG2_E)))


def _silu(x):
    return x * _sigmoid(x)


def _fold_lanes(x):
    acc = x[:, :LANES]
    for c in range(LANES, x.shape[1], LANES):
        acc = acc + x[:, c:c + LANES]
    return acc


def _rms_scale(x):
    ssq = jnp.sum(_fold_lanes(x * x), axis=-1, keepdims=True)
    return lax.rsqrt(ssq / x.shape[-1] + EPS)


def _rmsnorm_kernel(x_ref, w_ref, o_ref):
    x = x_ref[...]
    o_ref[...] = (x * _rms_scale(x) * w_ref[...]).astype(o_ref.dtype)


def _rmsnorm(x, w, tm=1024):
    m, d = x.shape
    return pl.pallas_call(
        _rmsnorm_kernel,
        out_shape=jax.ShapeDtypeStruct((m, d), BF16),
        grid=(m // tm,),
        in_specs=[pl.BlockSpec((tm, d), lambda i: (i, 0)),
                  pl.BlockSpec((1, d), lambda i: (0, 0))],
        out_specs=pl.BlockSpec((tm, d), lambda i: (i, 0)),
        compiler_params=_cparams(1),
        name="rmsnorm",
    )(x, w)


def _gateup_kernel(u_ref, wg_ref, wu_ref, wd_ref, o_ref, wd_bf_ref, wg_bf, wu_bf):
    @pl.when(pl.program_id(1) == 0)
    def _():
        wg_bf[...] = wg_ref[...].astype(BF16)
        wu_bf[...] = wu_ref[...].astype(BF16)
        wd_bf_ref[...] = wd_ref[...].astype(BF16)

    for r in range(0, u_ref.shape[0], MATMUL_SUB_ROWS):
        rows = slice(r, r + MATMUL_SUB_ROWS)
        u = u_ref[rows, :]
        g = jnp.dot(u, wg_bf[...], preferred_element_type=F32)
        up = jnp.dot(u, wu_bf[...], preferred_element_type=F32)
        o_ref[rows, :] = (_silu(g) * up).astype(o_ref.dtype)


def _gateup(u, w_gate, w_up, w_down, tm=2048, tn=512):
    m, d = u.shape
    f = w_gate.shape[-1]
    w_spec = pl.BlockSpec((None, d, tn), lambda j, i: (0, 0, j))
    return pl.pallas_call(
        _gateup_kernel,
        out_shape=[jax.ShapeDtypeStruct((m, f), BF16), jax.ShapeDtypeStruct((f, d), BF16)],
        grid=(pl.cdiv(f, tn), m // tm),
        in_specs=[pl.BlockSpec((tm, d), lambda j, i: (i, 0)), w_spec, w_spec,
                  pl.BlockSpec((None, tn, d), lambda j, i: (0, j, 0))],
        out_specs=[pl.BlockSpec((tm, tn), lambda j, i: (i, j)),
                   pl.BlockSpec((tn, d), lambda j, i: (j, 0))],
        scratch_shapes=[pltpu.VMEM((d, tn), BF16), pltpu.VMEM((d, tn), BF16)],
        compiler_params=_cparams(2),
        name="ffn_gateup",
    )(u, w_gate, w_up, w_down)


def _proj_kernel(u_ref, w_ref, *rest, shift, n_plain, tn, side):
    rest = list(rest)
    wn_ref = rest.pop(0) if shift else None
    ws_ref = rest.pop(0) if side else None
    o_ref = rest.pop(0)
    os_ref = rest.pop(0) if side else None
    (w_bf,) = rest
    j, i = pl.program_id(0), pl.program_id(1)

    @pl.when((i == 0) & (j < n_plain))
    def _():
        w_bf[...] = w_ref[...].astype(BF16)

    if shift:
        @pl.when((i == 0) & (j >= n_plain))
        def _():
            w = jnp.concatenate([w_ref[...], wn_ref[...]], axis=0)[shift:shift + tn]
            w_bf[...] = w.astype(BF16)

    sub = min(u_ref.shape[0], MATMUL_SUB_ROWS)
    for r in range(0, u_ref.shape[0], sub):
        o_ref[r:r + sub, :] = _dot_nt(u_ref[r:r + sub, :], w_bf[...]).astype(o_ref.dtype)
    if side:
        os_ref[...] = _dot_nt(u_ref[...], ws_ref[...].astype(BF16))


def _proj(u, wt, windows, out_dtype, tn=1024, side=None):
    m, d = u.shape
    shift = windows[-1][0] % tn
    starts = []
    n_plain = 0
    for c0, n in windows:
        assert c0 % tn in (0, shift) and n % tn == 0 and shift % SUBLANES == 0
        if c0 % tn == 0 and n_plain == len(starts):
            n_plain += n // tn
        else:
            assert c0 % tn == shift
        starts += [c0 // tn + t for t in range(n // tn)]
    n_tiles = len(starts)
    tm = 2048 if n_tiles >= 3 else 1024

    def blk(j):
        b = jnp.int32(starts[0])
        for t in range(1, n_tiles):
            b = jnp.where(j >= t, starts[t], b)
        return b

    in_specs = [pl.BlockSpec((tm, d), lambda j, i: (i, 0)),
                pl.BlockSpec((None, tn, d), lambda j, i: (0, blk(j), 0))]
    args = [u, wt]
    if shift:
        assert tn % shift == 0
        per_tile = tn // shift
        in_specs.append(pl.BlockSpec((None, shift, d), lambda j, i: (0, (blk(j) + 1) * per_tile, 0)))
        args.append(wt)
    out_shape = [jax.ShapeDtypeStruct((m, n_tiles * tn), out_dtype)]
    out_specs = [pl.BlockSpec((tm, tn), lambda j, i: (i, j))]
    if side:
        s0, sn = side
        assert n_tiles == 1 and s0 % sn == 0
        in_specs.append(pl.BlockSpec((None, sn, d), lambda j, i: (0, s0 // sn, 0)))
        args.append(wt)
        out_shape.append(jax.ShapeDtypeStruct((m, sn), F32))
        out_specs.append(pl.BlockSpec((tm, sn), lambda j, i: (i, 0)))
    outs = pl.pallas_call(
        functools.partial(_proj_kernel, shift=shift, n_plain=n_plain, tn=tn, side=bool(side)),
        out_shape=out_shape,
        grid=(n_tiles, m // tm),
        in_specs=in_specs,
        out_specs=out_specs,
        scratch_shapes=[pltpu.VMEM((tn, d), BF16)],
        compiler_params=_cparams(2),
        name="in_proj",
    )(*args)
    return outs if side else outs[0]


def _rows_kernel(x_ref, w_ref, res_ref, post_ref, *rest, tn, res_scale, emit_next):
    if emit_next:
        next_ref, h_ref, u_ref, acc_ref = rest
    else:
        h_ref, acc_ref = rest
    tm, d = h_ref.shape
    col_tiles = [slice(c, c + tn) for c in range(0, d, tn)]

    for r in range(0, tm, ROWS_SUB):
        rows = slice(r, r + ROWS_SUB)
        x = x_ref[rows, :]
        for cols in col_tiles:
            acc_ref[rows, cols] = jnp.dot(x, w_ref[:, cols], preferred_element_type=F32)

        ssq = None
        for cols in col_tiles:
            a = acc_ref[rows, cols]
            s = _fold_lanes(a * a)
            ssq = s if ssq is None else ssq + s
        ssq = jnp.sum(ssq, axis=-1, keepdims=True)
        scale = lax.rsqrt(ssq / d + EPS) * res_scale
        hsq = None
        for cols in col_tiles:
            h = res_ref[rows, cols] + acc_ref[rows, cols] * scale * post_ref[:, cols]
            h_ref[rows, cols] = h
            if emit_next:
                s = _fold_lanes(h * h)
                hsq = s if hsq is None else hsq + s
        if emit_next:
            hsq = jnp.sum(hsq, axis=-1, keepdims=True)
            nscale = lax.rsqrt(hsq / d + EPS)
            for cols in col_tiles:
                u_ref[rows, cols] = (h_ref[rows, cols] * nscale
                                     * next_ref[:, cols]).astype(u_ref.dtype)


ROWS_SUB = 256


def _rows(x, w, res, post_w, next_w, res_scale, tm=512, tn=512):
    m, k = x.shape
    d = w.shape[-1]
    emit_next = next_w is not None
    row_spec = pl.BlockSpec((tm, d), lambda i: (i, 0))
    vec_spec = pl.BlockSpec((1, d), lambda i: (0, 0))
    in_specs = [pl.BlockSpec((tm, k), lambda i: (i, 0)),
                pl.BlockSpec((k, d), lambda i: (0, 0), pipeline_mode=pl.Buffered(1)),
                row_spec, vec_spec]
    args = [x, w, res, post_w]
    out_shape = [jax.ShapeDtypeStruct((m, d), F32)]
    out_specs = [row_spec]
    if emit_next:
        in_specs.append(vec_spec)
        args.append(next_w)
        out_shape.append(jax.ShapeDtypeStruct((m, d), BF16))
        out_specs.append(row_spec)
    outs = pl.pallas_call(
        functools.partial(_rows_kernel, tn=tn, res_scale=res_scale, emit_next=emit_next),
        out_shape=out_shape,
        grid=(m // tm,),
        in_specs=in_specs,
        out_specs=out_specs,
        scratch_shapes=[pltpu.VMEM((tm, d), F32)],
        compiler_params=_cparams(1),
        name="rows_matmul_norm",
    )(*args)
    return outs if emit_next else (outs[0], None)


def _split_bf16(x):
    hi = x.astype(BF16)
    lo = (x - hi.astype(F32)).astype(BF16)
    return hi, lo


def _dot_nt(a, b):
    return lax.dot_general(a, b, (((1,), (1,)), ((), ())), preferred_element_type=F32)


def _dot_tn(a, b):
    return lax.dot_general(a, b, (((0,), (0,)), ((), ())), preferred_element_type=F32)


N_SUB = CHUNK // SUB


def _chunk_consts():
    i = lax.broadcasted_iota(jnp.int32, (CHUNK, CHUNK), 0)
    j = lax.broadcasted_iota(jnp.int32, (CHUNK, CHUNK), 1)
    lo = (i // SUB) * SUB
    hi = lo + SUB
    groups = [(j >= lo) & (j <= i), (j > i) & (j < hi), j < lo, j >= hi]
    sel = jnp.concatenate([jnp.where(g, 1.0, 0.0) for g in groups], axis=0).astype(BF16)
    return jnp.concatenate([sel, sel], axis=1), i >= j


def _decay_sums(sel2, g):
    g_hi, g_lo = _split_bf16(g)
    return jnp.dot(sel2, jnp.concatenate([g_hi, g_lo], axis=0), preferred_element_type=F32)


def _gated_chunks(heads, causal):
    prep = [_chunk_operands(*h) for h in heads]
    scores = [_dot_nt(p[0], p[1]) for p in prep]
    kv = [jnp.dot(p[4], p[3], preferred_element_type=F32) for p in prep]
    outs, states = [], []
    for (q_t, k_t, q_st, k_st, v_t, st_bf, st_decay), s, upd, h in zip(prep, scores, kv, heads):
        s = jnp.where(causal, s, 0.0).astype(BF16)
        outs.append(_dot_nt(jnp.concatenate([q_st, s], axis=1),
                            jnp.concatenate([st_bf, v_t], axis=1)))
        states.append(h[4] * st_decay + upd)
    return outs, states


def _chunk_operands(q, k, v, dec, st):
    within, rest, before, after = dec
    dk = q.shape[1]
    cum = within + before
    q_st = (q * jnp.exp2(cum)).astype(BF16)
    k_st = (k * jnp.exp2(rest + after)).astype(BF16)
    q_in = (q * jnp.exp2(within)).astype(BF16)
    k_diag = (k * jnp.exp2(jnp.minimum(-within, EXP2_CLAMP))).astype(BF16)
    k_end = k * jnp.exp2(rest)
    k_end_bf = k_end.astype(BF16)

    blk = lambda x, b: x[b * SUB:(b + 1) * SUB]
    whole = {b: jnp.exp2(within[(b + 1) * SUB - 1:(b + 1) * SUB]) for b in range(1, N_SUB - 1)}
    skip = {}
    for bq in range(N_SUB):
        for bk in range(bq - 1):
            d = whole[bk + 1]
            for mid in range(bk + 2, bq):
                d = d * whole[mid]
            skip[bq, bk] = d
    zero = jnp.zeros((SUB, dk), BF16)
    k_cols, q_cols = [], []
    for bq in range(N_SUB):
        col = []
        for bk in range(N_SUB):
            if bk > bq:
                col.append(zero)
            elif bk == bq:
                col.append(blk(k_diag, bk))
            elif bk == bq - 1:
                col.append(blk(k_end_bf, bk))
            else:
                col.append((blk(k_end, bk) * skip[bq, bk]).astype(BF16))
        k_cols.append(jnp.concatenate(col, axis=0))
        q_cols.append(jnp.concatenate([blk(q_in, b) if b == bq else zero for b in range(N_SUB)],
                                      axis=0))
    q_tilde = jnp.concatenate(q_cols, axis=1)
    k_tilde = jnp.concatenate(k_cols, axis=1)
    v_t = v.T
    return q_tilde, k_tilde, q_st, k_st, v_t, st.astype(BF16), jnp.exp2(cum[CHUNK - 1:CHUNK])


def _head_norm_gate(o, norm_w, gate):
    return (o * _rms_scale(o) * norm_w * _silu(gate.astype(F32))).astype(BF16)


def _gla_kernel(q_ref, k_ref, v_ref, go_ref, code_ref, wgk_ref, bgk_ref, gn_ref, o_ref, st_ref,
                sums_ref, *, n_chunks, dk, dv):
    @pl.when(pl.program_id(1) == 0)
    def _():
        st_ref[...] = jnp.zeros_like(st_ref)

    sel2, causal = _chunk_consts()
    norm_w = gn_ref[...]
    q_scale = dk ** -0.5

    w = wgk_ref[...]
    w = jnp.concatenate([w, jnp.zeros((code_ref.shape[1] - w.shape[0], w.shape[1]), F32)], axis=0)
    w_hi, w_lo = _split_bf16(w)
    c_hi, c_lo = _split_bf16(code_ref[...])
    z = (jnp.dot(c_hi, w_hi, preferred_element_type=F32)
         + jnp.dot(c_lo, w_hi, preferred_element_type=F32)
         + jnp.dot(c_hi, w_lo, preferred_element_type=F32)) + bgk_ref[...]
    log2_a = ((jnp.minimum(z, 0.0) - jnp.log1p(jnp.exp2(jnp.abs(z) * (-LOG2_E))))
              * (LOG2_E / GLA_GATE_NORMALIZER))
    for c in range(n_chunks):
        sums_ref[c] = _decay_sums(sel2, log2_a[c * CHUNK:(c + 1) * CHUNK])

    def chunk_body(c, carry):
        rows = pl.ds(pl.multiple_of(c * CHUNK, CHUNK), CHUNK)
        heads = []
        for h in range(GLA_HEADS):
            kc = slice(h * dk, (h + 1) * dk)
            q = q_ref[rows, kc].astype(F32) * q_scale
            k = k_ref[rows, kc].astype(F32)
            dec = [sums_ref[c, n * CHUNK:(n + 1) * CHUNK, kc] for n in range(4)]
            heads.append((q, k, v_ref[rows, h * dv:(h + 1) * dv], dec, st_ref[h]))
        outs, states = _gated_chunks(heads, causal)
        for h in range(GLA_HEADS):
            vc = slice(h * dv, (h + 1) * dv)
            st_ref[h] = states[h]
            o_ref[rows, vc] = _head_norm_gate(outs[h], norm_w, go_ref[rows, vc])
        return carry

    lax.fori_loop(0, n_chunks, chunk_body, 0, unroll=CHUNK_UNROLL)


def _gla(p_a, p_code, w_gk, b_gk, gla_norm, vw, batch, seq, t_blk=1024):
    rank, kw = w_gk.shape
    dk = kw // GLA_HEADS
    dv = vw // GLA_HEADS
    nt = seq // t_blk
    row = lambda b, t: b * nt + t
    return pl.pallas_call(
        functools.partial(_gla_kernel, n_chunks=t_blk // CHUNK, dk=dk, dv=dv),
        out_shape=jax.ShapeDtypeStruct((batch * seq, vw), BF16),
        grid=(batch, nt),
        in_specs=[pl.BlockSpec((t_blk, kw), lambda b, t: (row(b, t), 0)),
                  pl.BlockSpec((t_blk, kw), lambda b, t: (row(b, t), 1)),
                  pl.BlockSpec((t_blk, vw), lambda b, t: (row(b, t), 1)),
                  pl.BlockSpec((t_blk, vw), lambda b, t: (row(b, t), 2)),
                  pl.BlockSpec((t_blk, LANES), lambda b, t: (row(b, t), 0)),
                  pl.BlockSpec((rank, kw), lambda b, t: (0, 0)),
                  pl.BlockSpec((1, kw), lambda b, t: (0, 0)),
                  pl.BlockSpec((1, dv), lambda b, t: (0, 0))],
        out_specs=pl.BlockSpec((t_blk, vw), lambda b, t: (row(b, t), 0)),
        scratch_shapes=[pltpu.VMEM((GLA_HEADS, dv, dk), F32),
                        pltpu.VMEM((t_blk // CHUNK, 4 * CHUNK, kw), F32)],
        compiler_params=_cparams(2),
        name="gla_mixer",
    )(p_a, p_a, p_a, p_a, p_code, w_gk, b_gk, gla_norm)


def _hgrn_kernel(hq_ref, hi_ref, ho_ref, hf_ref, lbl_ref, hn_ref, o_ref, st_ref,
                 *, n_chunks, n_heads, dk, layer):
    @pl.when(pl.program_id(1) == 0)
    def _():
        st_ref[...] = jnp.zeros_like(st_ref)

    sel2, causal = _chunk_consts()
    logits = lbl_ref[...]
    p = jnp.exp(logits - jnp.max(logits, axis=0, keepdims=True))
    p = p / jnp.sum(p, axis=0, keepdims=True)
    lb = jnp.sum(p[:layer + 1], axis=0, keepdims=True)
    one_m_lb = jnp.sum(p[layer + 1:], axis=0, keepdims=True)
    norm_w = hn_ref[...]

    def chunk_body(c, carry):
        rows = pl.ds(pl.multiple_of(c * CHUNK, CHUNK), CHUNK)
        hf = hf_ref[rows, :]
        e = jnp.exp2(jnp.abs(hf) * (-LOG2_E))
        inv = 1.0 / (1.0 + e)
        pos = hf >= 0.0
        sig = jnp.where(pos, inv, e * inv)
        sig_neg = jnp.where(pos, e * inv, inv)
        log2_f = jnp.log2(lb + one_m_lb * sig)
        k_all = one_m_lb * sig_neg
        sums = _decay_sums(sel2, log2_f)
        heads = []
        for h in range(n_heads):
            hc = slice(h * dk, (h + 1) * dk)
            q = _silu(hq_ref[rows, hc].astype(F32))
            dec = [sums[n * CHUNK:(n + 1) * CHUNK, hc] for n in range(4)]
            heads.append((q, k_all[:, hc], hi_ref[rows, hc], dec, st_ref[h]))
        outs, states = _gated_chunks(heads, causal)
        for h in range(n_heads):
            hc = slice(h * dk, (h + 1) * dk)
            st_ref[h] = states[h]
            o_ref[rows, hc] = _head_norm_gate(outs[h], norm_w, ho_ref[rows, hc])
        return carry

    lax.fori_loop(0, n_chunks, chunk_body, 0, unroll=CHUNK_UNROLL)


def _hgrn(p_r, q_col0, p_f, lb_logits, hgrn_norm, layer, batch, seq, t_blk=1024):
    w = p_f.shape[1]
    dk = HGRN_EXPAND
    n_heads = w // dk
    nt = seq // t_blk
    n_lb = lb_logits.shape[0]
    row = lambda b, t: b * nt + t
    c0 = q_col0 // w
    assert c0 * w == q_col0
    return pl.pallas_call(
        functools.partial(_hgrn_kernel, n_chunks=t_blk // CHUNK, n_heads=n_heads, dk=dk,
                          layer=layer),
        out_shape=jax.ShapeDtypeStruct((batch * seq, w), BF16),
        grid=(batch, nt),
        in_specs=[pl.BlockSpec((t_blk, w), lambda b, t: (row(b, t), c0)),
                  pl.BlockSpec((t_blk, w), lambda b, t: (row(b, t), c0 + 1)),
                  pl.BlockSpec((t_blk, w), lambda b, t: (row(b, t), c0 + 2)),
                  pl.BlockSpec((t_blk, w), lambda b, t: (row(b, t), 0)),
                  pl.BlockSpec((n_lb, w), lambda b, t: (0, 0)),
                  pl.BlockSpec((1, dk), lambda b, t: (0, 0))],
        out_specs=pl.BlockSpec((t_blk, w), lambda b, t: (row(b, t), 0)),
        scratch_shapes=[pltpu.VMEM((n_heads, dk, dk), F32)],
        compiler_params=_cparams(2),
        name="hgrn_mixer",
    )(p_r, p_r, p_r, p_f, lb_logits, hgrn_norm)


def _merge_kernel(og_ref, oh_ref, zg_ref, zh_ref, wg_ref, wh_ref, b_ref, wo_ref,
                  o_ref, wo_bf_ref, wg_bf, wh_bf):
    @pl.when(pl.program_id(1) == 0)
    def _():
        wg_bf[...] = wg_ref[...].astype(BF16)
        wh_bf[...] = wh_ref[...].astype(BF16)

    wo_bf_ref[...] = wo_ref[...].astype(BF16)
    a = jnp.dot(og_ref[...], wg_bf[...], preferred_element_type=F32)
    b = jnp.dot(oh_ref[...], wh_bf[...], preferred_element_type=F32)
    o_ref[...] = (_sigmoid(zg_ref[...].astype(F32) + b_ref[0:1, :]) * a
                  + _sigmoid(zh_ref[...].astype(F32) + b_ref[1:2, :]) * b).astype(o_ref.dtype)


def _merge(o_gla, o_hgrn, p_b, zg_col0, zh_col0, w_bg, w_bh, b_gates, w_out, tm=1024, tn=1024):
    m, kdim = o_gla.shape
    d = w_bg.shape[-1]
    ni = m // tm
    k_out, d_out = w_out.shape[1:]
    rb = k_out // ((d // tn) * ni)
    assert rb * (d // tn) * ni == k_out and rb % (2 * SUBLANES) == 0
    x_spec = pl.BlockSpec((tm, kdim), lambda j, i: (i, 0))
    w_spec = pl.BlockSpec((None, kdim, tn), lambda j, i: (0, 0, j))
    return pl.pallas_call(
        _merge_kernel,
        out_shape=[jax.ShapeDtypeStruct((m, d), BF16), jax.ShapeDtypeStruct((k_out, d_out), BF16)],
        grid=(d // tn, ni),
        in_specs=[x_spec, x_spec,
                  pl.BlockSpec((tm, tn), lambda j, i: (i, zg_col0 // tn + j)),
                  pl.BlockSpec((tm, tn), lambda j, i: (i, zh_col0 // tn + j)),
                  w_spec, w_spec,
                  pl.BlockSpec((None, 2, tn), lambda j, i: (0, 0, j)),
                  pl.BlockSpec((None, rb, d_out), lambda j, i: (0, j * ni + i, 0))],
        out_specs=[pl.BlockSpec((tm, tn), lambda j, i: (i, j)),
                   pl.BlockSpec((rb, d_out), lambda j, i: (j * ni + i, 0))],
        scratch_shapes=[pltpu.VMEM((kdim, tn), BF16), pltpu.VMEM((kdim, tn), BF16)],
        compiler_params=_cparams(2),
        name="branch_merge",
    )(o_gla, o_hgrn, p_b, p_b, w_bg, w_bh, b_gates, w_out)


def kernel(x, ffn1_pre_norm, ffn1_w_gate, ffn1_w_up, ffn1_w_down, ffn1_post_norm, mix_pre_norm, w_in, gla_w_gk_up, gla_b_gk, gla_norm, hgrn_lb_logits, hgrn_norm, w_branch_gla, w_branch_hgrn, b_branch_gates, w_out, mix_post_norm, ffn2_pre_norm, ffn2_w_gate, ffn2_w_up, ffn2_w_down, ffn2_post_norm):
    batch, seq, d_model = x.shape
    depth = ffn1_w_gate.shape[0]
    m = batch * seq
    kw = gla_w_gk_up.shape[-1]
    vw = d_model // 2
    a_cols = 2 * kw + 2 * vw
    code0 = a_cols
    hq0 = code0 + GLA_GATE_RANK
    hf0, hi0 = hq0 + vw, hq0 + 2 * vw

    h = x.reshape(m, d_model)
    u = _rmsnorm(h, ffn1_pre_norm[0:1])
    for l in range(depth):
        mid, w_down = _gateup(u, ffn1_w_gate[l:l + 1], ffn1_w_up[l:l + 1], ffn1_w_down[l:l + 1])
        h, u = _rows(mid, w_down, h, ffn1_post_norm[l:l + 1],
                     mix_pre_norm[l:l + 1], 0.5)

        w_l = jnp.swapaxes(w_in[l:l + 1], 1, 2)
        p_f, p_code = _proj(u, w_l, [(hf0, vw)], F32, side=(code0, LANES))
        p = _proj(u, w_l, [(0, a_cols), (hq0, vw), (hi0, 2 * vw + 2 * d_model)], BF16)
        o_gla = _gla(p, p_code, gla_w_gk_up[l], gla_b_gk[l:l + 1], gla_norm[l:l + 1], vw, batch,
                     seq)
        o_hgrn = _hgrn(p, a_cols, p_f, hgrn_lb_logits, hgrn_norm[l:l + 1], l, batch, seq)
        merged, w_out_bf = _merge(o_gla, o_hgrn, p, a_cols + 3 * vw, a_cols + 3 * vw + d_model,
                                  w_branch_gla[l:l + 1], w_branch_hgrn[l:l + 1],
                                  b_branch_gates[l:l + 1], w_out[l:l + 1])
        h, u = _rows(merged, w_out_bf, h, mix_post_norm[l:l + 1],
                     ffn2_pre_norm[l:l + 1], 1.0, tm=512)

        mid, w_down = _gateup(u, ffn2_w_gate[l:l + 1], ffn2_w_up[l:l + 1], ffn2_w_down[l:l + 1])
        next_norm = ffn1_pre_norm[l + 1:l + 2] if l + 1 < depth else None
        h, u = _rows(mid, w_down, h, ffn2_post_norm[l:l + 1], next_norm, 0.5)
    return h.reshape(batch, seq, d_model)
```

```python
import functools

import jax
import jax.numpy as jnp
from jax import lax
from jax.experimental import pallas as pl
from jax.experimental.pallas import tpu as pltpu

F32 = jnp.float32
BF16 = jnp.bfloat16

EPS = 1e-6
CHUNK = 64
SUB = 16
GLA_HEADS = 4
GLA_GATE_RANK = 16
GLA_GATE_NORMALIZER = 16.0
HGRN_EXPAND = 128
CHUNK_UNROLL = 4
LOG2_E = 1.4426950408889634
EXP2_CLAMP = 115.0

LANES = 128
SUBLANES = 8
VMEM_LIMIT = 62 * 1024 * 1024
MATMUL_SUB_ROWS = 1024


def _cparams(n_axes):
    return pltpu.CompilerParams(
        dimension_semantics=("arbitrary",) * n_axes, vmem_limit_bytes=VMEM_LIMIT)


def _sigmoid(x):
    return 1.0 / (1.0 + jnp.exp2(x * (-LOG2_E)))


def _silu(x):
    return x * _sigmoid(x)


def _fold_lanes(x):
    acc = x[:, :LANES]
    for c in range(LANES, x.shape[1], LANES):
        acc = acc + x[:, c:c + LANES]
    return acc


def _rms_scale(x):
    ssq = jnp.sum(_fold_lanes(x * x), axis=-1, keepdims=True)
    return lax.rsqrt(ssq / x.shape[-1] + EPS)


def _rmsnorm_kernel(x_ref, w_ref, o_ref):
    x = x_ref[...]
    o_ref[...] = (x * _rms_scale(x) * w_ref[...]).astype(o_ref.dtype)


def _rmsnorm(x, w, tm=1024):
    m, d = x.shape
    return pl.pallas_call(
        _rmsnorm_kernel,
        out_shape=jax.ShapeDtypeStruct((m, d), BF16),
        grid=(m // tm,),
        in_specs=[pl.BlockSpec((tm, d), lambda i: (i, 0)),
                  pl.BlockSpec((1, d), lambda i: (0, 0))],
        out_specs=pl.BlockSpec((tm, d), lambda i: (i, 0)),
        compiler_params=_cparams(1),
        name="rmsnorm",
    )(x, w)


def _gateup_kernel(u_ref, wg_ref, wu_ref, wd_ref, o_ref, wd_bf_ref, wg_bf, wu_bf):
    @pl.when(pl.program_id(1) == 0)
    def _():
        wg_bf[...] = wg_ref[...].astype(BF16)
        wu_bf[...] = wu_ref[...].astype(BF16)
        wd_bf_ref[...] = wd_ref[...].astype(BF16)

    for r in range(0, u_ref.shape[0], MATMUL_SUB_ROWS):
        rows = slice(r, r + MATMUL_SUB_ROWS)
        u = u_ref[rows, :]
        g = jnp.dot(u, wg_bf[...], preferred_element_type=F32)
        up = jnp.dot(u, wu_bf[...], preferred_element_type=F32)
        o_ref[rows, :] = (_silu(g) * up).astype(o_ref.dtype)


def _gateup(u, w_gate, w_up, w_down, tm=2048, tn=512):
    m, d = u.shape
    f = w_gate.shape[-1]
    w_spec = pl.BlockSpec((None, d, tn), lambda j, i: (0, 0, j))
    return pl.pallas_call(
        _gateup_kernel,
        out_shape=[jax.ShapeDtypeStruct((m, f), BF16), jax.ShapeDtypeStruct((f, d), BF16)],
        grid=(pl.cdiv(f, tn), m // tm),
        in_specs=[pl.BlockSpec((tm, d), lambda j, i: (i, 0)), w_spec, w_spec,
                  pl.BlockSpec((None, tn, d), lambda j, i: (0, j, 0))],
        out_specs=[pl.BlockSpec((tm, tn), lambda j, i: (i, j)),
                   pl.BlockSpec((tn, d), lambda j, i: (j, 0))],
        scratch_shapes=[pltpu.VMEM((d, tn), BF16), pltpu.VMEM((d, tn), BF16)],
        compiler_params=_cparams(2),
        name="ffn_gateup",
    )(u, w_gate, w_up, w_down)


def _proj_kernel(u_ref, w_ref, *rest, shift, n_plain, tn, side):
    rest = list(rest)
    wn_ref = rest.pop(0) if shift else None
    ws_ref = rest.pop(0) if side else None
    o_ref = rest.pop(0)
    os_ref = rest.pop(0) if side else None
    (w_bf,) = rest
    j, i = pl.program_id(0), pl.program_id(1)

    @pl.when((i == 0) & (j < n_plain))
    def _():
        w_bf[...] = w_ref[...].astype(BF16)

    if shift:
        @pl.when((i == 0) & (j >= n_plain))
        def _():
            w = jnp.concatenate([w_ref[...], wn_ref[...]], axis=0)[shift:shift + tn]
            w_bf[...] = w.astype(BF16)

    sub = min(u_ref.shape[0], MATMUL_SUB_ROWS)
    for r in range(0, u_ref.shape[0], sub):
        o_ref[r:r + sub, :] = _dot_nt(u_ref[r:r + sub, :], w_bf[...]).astype(o_ref.dtype)
    if side:
        os_ref[...] = _dot_nt(u_ref[...], ws_ref[...].astype(BF16))


def _proj(u, wt, windows, out_dtype, tn=1024, side=None):
    m, d = u.shape
    shift = windows[-1][0] % tn
    starts = []
    n_plain = 0
    for c0, n in windows:
        assert c0 % tn in (0, shift) and n % tn == 0 and shift % SUBLANES == 0
        if c0 % tn == 0 and n_plain == len(starts):
            n_plain += n // tn
        else:
            assert c0 % tn == shift
        starts += [c0 // tn + t for t in range(n // tn)]
    n_tiles = len(starts)
    tm = 2048 if n_tiles >= 3 else 1024

    def blk(j):
        b = jnp.int32(starts[0])
        for t in range(1, n_tiles):
            b = jnp.where(j >= t, starts[t], b)
        return b

    in_specs = [pl.BlockSpec((tm, d), lambda j, i: (i, 0)),
                pl.BlockSpec((None, tn, d), lambda j, i: (0, blk(j), 0))]
    args = [u, wt]
    if shift:
        assert tn % shift == 0
        per_tile = tn // shift
        in_specs.append(pl.BlockSpec((None, shift, d), lambda j, i: (0, (blk(j) + 1) * per_tile, 0)))
        args.append(wt)
    out_shape = [jax.ShapeDtypeStruct((m, n_tiles * tn), out_dtype)]
    out_specs = [pl.BlockSpec((tm, tn), lambda j, i: (i, j))]
    if side:
        s0, sn = side
        assert n_tiles == 1 and s0 % sn == 0
        in_specs.append(pl.BlockSpec((None, sn, d), lambda j, i: (0, s0 // sn, 0)))
        args.append(wt)
        out_shape.append(jax.ShapeDtypeStruct((m, sn), F32))
        out_specs.append(pl.BlockSpec((tm, sn), lambda j, i: (i, 0)))
    outs = pl.pallas_call(
        functools.partial(_proj_kernel, shift=shift, n_plain=n_plain, tn=tn, side=bool(side)),
        out_shape=out_shape,
        grid=(n_tiles, m // tm),
        in_specs=in_specs,
        out_specs=out_specs,
        scratch_shapes=[pltpu.VMEM((tn, d), BF16)],
        compiler_params=_cparams(2),
        name="in_proj",
    )(*args)
    return outs if side else outs[0]


def _rows_kernel(x_ref, w_ref, res_ref, post_ref, *rest, tn, res_scale, emit_next):
    if emit_next:
        next_ref, h_ref, u_ref, acc_ref = rest
    else:
        h_ref, acc_ref = rest
    tm, d = h_ref.shape
    col_tiles = [slice(c, c + tn) for c in range(0, d, tn)]

    for r in range(0, tm, ROWS_SUB):
        rows = slice(r, r + ROWS_SUB)
        x = x_ref[rows, :]
        for cols in col_tiles:
            acc_ref[rows, cols] = jnp.dot(x, w_ref[:, cols], preferred_element_type=F32)

        ssq = None
        for cols in col_tiles:
            a = acc_ref[rows, cols]
            s = _fold_lanes(a * a)
            ssq = s if ssq is None else ssq + s
        ssq = jnp.sum(ssq, axis=-1, keepdims=True)
        scale = lax.rsqrt(ssq / d + EPS) * res_scale
        hsq = None
        for cols in col_tiles:
            h = res_ref[rows, cols] + acc_ref[rows, cols] * scale * post_ref[:, cols]
            h_ref[rows, cols] = h
            if emit_next:
                s = _fold_lanes(h * h)
                hsq = s if hsq is None else hsq + s
        if emit_next:
            hsq = jnp.sum(hsq, axis=-1, keepdims=True)
            nscale = lax.rsqrt(hsq / d + EPS)
            for cols in col_tiles:
                u_ref[rows, cols] = (h_ref[rows, cols] * nscale
                                     * next_ref[:, cols]).astype(u_ref.dtype)


ROWS_SUB = 256


def _rows(x, w, res, post_w, next_w, res_scale, tm=512, tn=512):
    m, k = x.shape
    d = w.shape[-1]
    emit_next = next_w is not None
    row_spec = pl.BlockSpec((tm, d), lambda i: (i, 0))
    vec_spec = pl.BlockSpec((1, d), lambda i: (0, 0))
    in_specs = [pl.BlockSpec((tm, k), lambda i: (i, 0)),
                pl.BlockSpec((k, d), lambda i: (0, 0), pipeline_mode=pl.Buffered(1)),
                row_spec, vec_spec]
    args = [x, w, res, post_w]
    out_shape = [jax.ShapeDtypeStruct((m, d), F32)]
    out_specs = [row_spec]
    if emit_next:
        in_specs.append(vec_spec)
        args.append(next_w)
        out_shape.append(jax.ShapeDtypeStruct((m, d), BF16))
        out_specs.append(row_spec)
    outs = pl.pallas_call(
        functools.partial(_rows_kernel, tn=tn, res_scale=res_scale, emit_next=emit_next),
        out_shape=out_shape,
        grid=(m // tm,),
        in_specs=in_specs,
        out_specs=out_specs,
        scratch_shapes=[pltpu.VMEM((tm, d), F32)],
        compiler_params=_cparams(1),
        name="rows_matmul_norm",
    )(*args)
    return outs if emit_next else (outs[0], None)


def _split_bf16(x):
    hi = x.astype(BF16)
    lo = (x - hi.astype(F32)).astype(BF16)
    return hi, lo


def _dot_nt(a, b):
    return lax.dot_general(a, b, (((1,), (1,)), ((), ())), preferred_element_type=F32)


def _dot_tn(a, b):
    return lax.dot_general(a, b, (((0,), (0,)), ((), ())), preferred_element_type=F32)


N_SUB = CHUNK // SUB


def _chunk_consts():
    i = lax.broadcasted_iota(jnp.int32, (CHUNK, CHUNK), 0)
    j = lax.broadcasted_iota(jnp.int32, (CHUNK, CHUNK), 1)
    lo = (i // SUB) * SUB
    hi = lo + SUB
    groups = [(j >= lo) & (j <= i), (j > i) & (j < hi), j < lo, j >= hi]
    sel = jnp.concatenate([jnp.where(g, 1.0, 0.0) for g in groups], axis=0).astype(BF16)
    return jnp.concatenate([sel, sel], axis=1), i >= j


def _decay_sums(sel2, g):
    g_hi, g_lo = _split_bf16(g)
    return jnp.dot(sel2, jnp.concatenate([g_hi, g_lo], axis=0), preferred_element_type=F32)


def _gated_chunks(heads, causal):
    prep = [_chunk_operands(*h) for h in heads]
    scores = [_dot_nt(p[0], p[1]) for p in prep]
    kv = [jnp.dot(p[4], p[3], preferred_element_type=F32) for p in prep]
    outs, states = [], []
    for (q_t, k_t, q_st, k_st, v_t, st_bf, st_decay), s, upd, h in zip(prep, scores, kv, heads):
        s = jnp.where(causal, s, 0.0).astype(BF16)
        outs.append(_dot_nt(jnp.concatenate([q_st, s], axis=1),
                            jnp.concatenate([st_bf, v_t], axis=1)))
        states.append(h[4] * st_decay + upd)
    return outs, states


def _chunk_operands(q, k, v, dec, st):
    within, rest, before, after = dec
    dk = q.shape[1]
    cum = within + before
    q_st = (q * jnp.exp2(cum)).astype(BF16)
    k_st = (k * jnp.exp2(rest + after)).astype(BF16)
    q_in = (q * jnp.exp2(within)).astype(BF16)
    k_diag = (k * jnp.exp2(jnp.minimum(-within, EXP2_CLAMP))).astype(BF16)
    k_end = k * jnp.exp2(rest)
    k_end_bf = k_end.astype(BF16)

    blk = lambda x, b: x[b * SUB:(b + 1) * SUB]
    whole = {b: jnp.exp2(within[(b + 1) * SUB - 1:(b + 1) * SUB]) for b in range(1, N_SUB - 1)}
    skip = {}
    for bq in range(N_SUB):
        for bk in range(bq - 1):
            d = whole[bk + 1]
            for mid in range(bk + 2, bq):
                d = d * whole[mid]
            skip[bq, bk] = d
    zero = jnp.zeros((SUB, dk), BF16)
    k_cols, q_cols = [], []
    for bq in range(N_SUB):
        col = []
        for bk in range(N_SUB):
            if bk > bq:
                col.append(zero)
            elif bk == bq:
                col.append(blk(k_diag, bk))
            elif bk == bq - 1:
                col.append(blk(k_end_bf, bk))
            else:
                col.append((blk(k_end, bk) * skip[bq, bk]).astype(BF16))
        k_cols.append(jnp.concatenate(col, axis=0))
        q_cols.append(jnp.concatenate([blk(q_in, b) if b == bq else zero for b in range(N_SUB)],
                                      axis=0))
    q_tilde = jnp.concatenate(q_cols, axis=1)
    k_tilde = jnp.concatenate(k_cols, axis=1)
    v_t = v.T
    return q_tilde, k_tilde, q_st, k_st, v_t, st.astype(BF16), jnp.exp2(cum[CHUNK - 1:CHUNK])


def _head_norm_gate(o, norm_w, gate):
    return (o * _rms_scale(o) * norm_w * _silu(gate.astype(F32))).astype(BF16)


def _gla_kernel(q_ref, k_ref, v_ref, go_ref, code_ref, wgk_ref, bgk_ref, gn_ref, o_ref, st_ref,
                sums_ref, *, n_chunks, dk, dv):
    @pl.when(pl.program_id(1) == 0)
    def _():
        st_ref[...] = jnp.zeros_like(st_ref)

    sel2, causal = _chunk_consts()
    norm_w = gn_ref[...]
    q_scale = dk ** -0.5

    w = wgk_ref[...]
    w = jnp.concatenate([w, jnp.zeros((code_ref.shape[1] - w.shape[0], w.shape[1]), F32)], axis=0)
    w_hi, w_lo = _split_bf16(w)
    c_hi, c_lo = _split_bf16(code_ref[...])
    z = (jnp.dot(c_hi, w_hi, preferred_element_type=F32)
         + jnp.dot(c_lo, w_hi, preferred_element_type=F32)
         + jnp.dot(c_hi, w_lo, preferred_element_type=F32)) + bgk_ref[...]
    log2_a = ((jnp.minimum(z, 0.0) - jnp.log1p(jnp.exp2(jnp.abs(z) * (-LOG2_E))))
              * (LOG2_E / GLA_GATE_NORMALIZER))
    for c in range(n_chunks):
        sums_ref[c] = _decay_sums(sel2, log2_a[c * CHUNK:(c + 1) * CHUNK])

    def chunk_body(c, carry):
        rows = pl.ds(pl.multiple_of(c * CHUNK, CHUNK), CHUNK)
        heads = []
        for h in range(GLA_HEADS):
            kc = slice(h * dk, (h + 1) * dk)
            q = q_ref[rows, kc].astype(F32) * q_scale
            k = k_ref[rows, kc].astype(F32)
            dec = [sums_ref[c, n * CHUNK:(n + 1) * CHUNK, kc] for n in range(4)]
            heads.append((q, k, v_ref[rows, h * dv:(h + 1) * dv], dec, st_ref[h]))
        outs, states = _gated_chunks(heads, causal)
        for h in range(GLA_HEADS):
            vc = slice(h * dv, (h + 1) * dv)
            st_ref[h] = states[h]
            o_ref[rows, vc] = _head_norm_gate(outs[h], norm_w, go_ref[rows, vc])
        return carry

    lax.fori_loop(0, n_chunks, chunk_body, 0, unroll=CHUNK_UNROLL)


def _gla(p_a, p_code, w_gk, b_gk, gla_norm, vw, batch, seq, t_blk=1024):
    rank, kw = w_gk.shape
    dk = kw // GLA_HEADS
    dv = vw // GLA_HEADS
    nt = seq // t_blk
    row = lambda b, t: b * nt + t
    return pl.pallas_call(
        functools.partial(_gla_kernel, n_chunks=t_blk // CHUNK, dk=dk, dv=dv),
        out_shape=jax.ShapeDtypeStruct((batch * seq, vw), BF16),
        grid=(batch, nt),
        in_specs=[pl.BlockSpec((t_blk, kw), lambda b, t: (row(b, t), 0)),
                  pl.BlockSpec((t_blk, kw), lambda b, t: (row(b, t), 1)),
                  pl.BlockSpec((t_blk, vw), lambda b, t: (row(b, t), 1)),
                  pl.BlockSpec((t_blk, vw), lambda b, t: (row(b, t), 2)),
                  pl.BlockSpec((t_blk, LANES), lambda b, t: (row(b, t), 0)),
                  pl.BlockSpec((rank, kw), lambda b, t: (0, 0)),
                  pl.BlockSpec((1, kw), lambda b, t: (0, 0)),
                  pl.BlockSpec((1, dv), lambda b, t: (0, 0))],
        out_specs=pl.BlockSpec((t_blk, vw), lambda b, t: (row(b, t), 0)),
        scratch_shapes=[pltpu.VMEM((GLA_HEADS, dv, dk), F32),
                        pltpu.VMEM((t_blk // CHUNK, 4 * CHUNK, kw), F32)],
        compiler_params=_cparams(2),
        name="gla_mixer",
    )(p_a, p_a, p_a, p_a, p_code, w_gk, b_gk, gla_norm)


def _hgrn_kernel(hq_ref, hi_ref, ho_ref, hf_ref, lbl_ref, hn_ref, o_ref, st_ref,
                 *, n_chunks, n_heads, dk, layer):
    @pl.when(pl.program_id(1) == 0)
    def _():
        st_ref[...] = jnp.zeros_like(st_ref)

    sel2, causal = _chunk_consts()
    logits = lbl_ref[...]
    p = jnp.exp(logits - jnp.max(logits, axis=0, keepdims=True))
    p = p / jnp.sum(p, axis=0, keepdims=True)
    lb = jnp.sum(p[:layer + 1], axis=0, keepdims=True)
    one_m_lb = jnp.sum(p[layer + 1:], axis=0, keepdims=True)
    norm_w = hn_ref[...]

    def chunk_body(c, carry):
        rows = pl.ds(pl.multiple_of(c * CHUNK, CHUNK), CHUNK)
        hf = hf_ref[rows, :]
        e = jnp.exp2(jnp.abs(hf) * (-LOG2_E))
        inv = 1.0 / (1.0 + e)
        pos = hf >= 0.0
        sig = jnp.where(pos, inv, e * inv)
        sig_neg = jnp.where(pos, e * inv, inv)
        log2_f = jnp.log2(lb + one_m_lb * sig)
        k_all = one_m_lb * sig_neg
        sums = _decay_sums(sel2, log2_f)
        heads = []
        for h in range(n_heads):
            hc = slice(h * dk, (h + 1) * dk)
            q = _silu(hq_ref[rows, hc].astype(F32))
            dec = [sums[n * CHUNK:(n + 1) * CHUNK, hc] for n in range(4)]
            heads.append((q, k_all[:, hc], hi_ref[rows, hc], dec, st_ref[h]))
        outs, states = _gated_chunks(heads, causal)
        for h in range(n_heads):
            hc = slice(h * dk, (h + 1) * dk)
            st_ref[h] = states[h]
            o_ref[rows, hc] = _head_norm_gate(outs[h], norm_w, ho_ref[rows, hc])
        return carry

    lax.fori_loop(0, n_chunks, chunk_body, 0, unroll=CHUNK_UNROLL)


def _hgrn(p_r, q_col0, p_f, lb_logits, hgrn_norm, layer, batch, seq, t_blk=1024):
    w = p_f.shape[1]
    dk = HGRN_EXPAND
    n_heads = w // dk
    nt = seq // t_blk
    n_lb = lb_logits.shape[0]
    row = lambda b, t: b * nt + t
    c0 = q_col0 // w
    assert c0 * w == q_col0
    return pl.pallas_call(
        functools.partial(_hgrn_kernel, n_chunks=t_blk // CHUNK, n_heads=n_heads, dk=dk,
                          layer=layer),
        out_shape=jax.ShapeDtypeStruct((batch * seq, w), BF16),
        grid=(batch, nt),
        in_specs=[pl.BlockSpec((t_blk, w), lambda b, t: (row(b, t), c0)),
                  pl.BlockSpec((t_blk, w), lambda b, t: (row(b, t), c0 + 1)),
                  pl.BlockSpec((t_blk, w), lambda b, t: (row(b, t), c0 + 2)),
                  pl.BlockSpec((t_blk, w), lambda b, t: (row(b, t), 0)),
                  pl.BlockSpec((n_lb, w), lambda b, t: (0, 0)),
                  pl.BlockSpec((1, dk), lambda b, t: (0, 0))],
        out_specs=pl.BlockSpec((t_blk, w), lambda b, t: (row(b, t), 0)),
        scratch_shapes=[pltpu.VMEM((n_heads, dk, dk), F32)],
        compiler_params=_cparams(2),
        name="hgrn_mixer",
    )(p_r, p_r, p_r, p_f, lb_logits, hgrn_norm)


def _merge_kernel(og_ref, oh_ref, zg_ref, zh_ref, wg_ref, wh_ref, b_ref, wo_ref,
                  o_ref, wo_bf_ref, wg_bf, wh_bf):
    @pl.when(pl.program_id(1) == 0)
    def _():
        wg_bf[...] = wg_ref[...].astype(BF16)
        wh_bf[...] = wh_ref[...].astype(BF16)

    wo_bf_ref[...] = wo_ref[...].astype(BF16)
    for r in range(0, o_ref.shape[0], MERGE_SUB_ROWS):
        rows = slice(r, r + MERGE_SUB_ROWS)
        a = jnp.dot(og_ref[rows, :], wg_bf[...], preferred_element_type=F32)
        b = jnp.dot(oh_ref[rows, :], wh_bf[...], preferred_element_type=F32)
        o_ref[rows, :] = (_sigmoid(zg_ref[rows, :].astype(F32) + b_ref[0:1, :]) * a
                          + _sigmoid(zh_ref[rows, :].astype(F32) + b_ref[1:2, :]) * b
                          ).astype(o_ref.dtype)


MERGE_SUB_ROWS = 512


def _merge(o_gla, o_hgrn, p_b, zg_col0, zh_col0, w_bg, w_bh, b_gates, w_out, tm=1024, tn=1024):
    m, kdim = o_gla.shape
    d = w_bg.shape[-1]
    ni = m // tm
    k_out, d_out = w_out.shape[1:]
    rb = k_out // ((d // tn) * ni)
    assert rb * (d // tn) * ni == k_out and rb % (2 * SUBLANES) == 0
    x_spec = pl.BlockSpec((tm, kdim), lambda j, i: (i, 0))
    w_spec = pl.BlockSpec((None, kdim, tn), lambda j, i: (0, 0, j))
    return pl.pallas_call(
        _merge_kernel,
        out_shape=[jax.ShapeDtypeStruct((m, d), BF16), jax.ShapeDtypeStruct((k_out, d_out), BF16)],
        grid=(d // tn, ni),
        in_specs=[x_spec, x_spec,
                  pl.BlockSpec((tm, tn), lambda j, i: (i, zg_col0 // tn + j)),
                  pl.BlockSpec((tm, tn), lambda j, i: (i, zh_col0 // tn + j)),
                  w_spec, w_spec,
                  pl.BlockSpec((None, 2, tn), lambda j, i: (0, 0, j)),
                  pl.BlockSpec((None, rb, d_out), lambda j, i: (0, j * ni + i, 0))],
        out_specs=[pl.BlockSpec((tm, tn), lambda j, i: (i, j)),
                   pl.BlockSpec((rb, d_out), lambda j, i: (j * ni + i, 0))],
        scratch_shapes=[pltpu.VMEM((kdim, tn), BF16), pltpu.VMEM((kdim, tn), BF16)],
        compiler_params=_cparams(2),
        name="branch_merge",
    )(o_gla, o_hgrn, p_b, p_b, w_bg, w_bh, b_gates, w_out)


def kernel(x, ffn1_pre_norm, ffn1_w_gate, ffn1_w_up, ffn1_w_down, ffn1_post_norm, mix_pre_norm, w_in, gla_w_gk_up, gla_b_gk, gla_norm, hgrn_lb_logits, hgrn_norm, w_branch_gla, w_branch_hgrn, b_branch_gates, w_out, mix_post_norm, ffn2_pre_norm, ffn2_w_gate, ffn2_w_up, ffn2_w_down, ffn2_post_norm):
    batch, seq, d_model = x.shape
    depth = ffn1_w_gate.shape[0]
    m = batch * seq
    kw = gla_w_gk_up.shape[-1]
    vw = d_model // 2
    a_cols = 2 * kw + 2 * vw
    code0 = a_cols
    hq0 = code0 + GLA_GATE_RANK
    hf0, hi0 = hq0 + vw, hq0 + 2 * vw

    h = x.reshape(m, d_model)
    u = _rmsnorm(h, ffn1_pre_norm[0:1])
    for l in range(depth):
        mid, w_down = _gateup(u, ffn1_w_gate[l:l + 1], ffn1_w_up[l:l + 1], ffn1_w_down[l:l + 1])
        h, u = _rows(mid, w_down, h, ffn1_post_norm[l:l + 1],
                     mix_pre_norm[l:l + 1], 0.5)

        w_l = jnp.swapaxes(w_in[l:l + 1], 1, 2)
        p_f, p_code = _proj(u, w_l, [(hf0, vw)], F32, side=(code0, LANES))
        p = _proj(u, w_l, [(0, a_cols), (hq0, vw), (hi0, 2 * vw + 2 * d_model)], BF16)
        o_gla = _gla(p, p_code, gla_w_gk_up[l], gla_b_gk[l:l + 1], gla_norm[l:l + 1], vw, batch,
                     seq)
        o_hgrn = _hgrn(p, a_cols, p_f, hgrn_lb_logits, hgrn_norm[l:l + 1], l, batch, seq)
        merged, w_out_bf = _merge(o_gla, o_hgrn, p, a_cols + 3 * vw, a_cols + 3 * vw + d_model,
                                  w_branch_gla[l:l + 1], w_branch_hgrn[l:l + 1],
                                  b_branch_gates[l:l + 1], w_out[l:l + 1])
        h, u = _rows(merged, w_out_bf, h, mix_post_norm[l:l + 1],
                     ffn2_pre_norm[l:l + 1], 1.0, tm=512)

        mid, w_down = _gateup(u, ffn2_w_gate[l:l + 1], ffn2_w_up[l:l + 1], ffn2_w_down[l:l + 1])
        next_norm = ffn1_pre_norm[l + 1:l + 2] if l + 1 < depth else None
        h, u = _rows(mid, w_down, h, ffn2_post_norm[l:l + 1], next_norm, 0.5)
    return h.reshape(batch, seq, d_model)
```
